```python
import math
import jax, jax.numpy as jnp
from jax import lax
import numpy as np

D_MODEL = 1024
BATCH = 4
SEQ = 8192
DEPTH = 4

N_A = DEPTH // 2
N_B = DEPTH - N_A
EXPAND = 2
A_WIDTH = EXPAND * D_MODEL
POOL_WINDOWS = (2, 4, 8, 16)
N_POOL_GROUPS = len(POOL_WINDOWS)
GROUP_WIDTH = A_WIDTH // N_POOL_GROUPS
HEAD_DIM = 64
N_HEADS = D_MODEL // HEAD_DIM
N_KV_HEADS = max(1, N_HEADS // 8)
GQA_GROUPS = N_HEADS // N_KV_HEADS
B_WIDTH = N_HEADS * HEAD_DIM
WINDOW = 128
BLOCK = 128
N_BUCKETS = 32
MAX_DISTANCE = 128
EPS = 1e-6
NEG_INF = -1e30

kernel_name = "yoco_pool_swa_sink_hybrid"


def rmsnorm(x, g):
    xf = x.astype(jnp.float32)
    y = xf * lax.rsqrt(jnp.mean(xf * xf, axis=-1, keepdims=True) + EPS)
    return y.astype(x.dtype) * g


def modulate(h, shift, scale):
    return h * (1 + scale[:, None, :]) + shift[:, None, :]


def t5_causal_buckets():
    i = np.arange(BLOCK)[:, None]
    j = np.arange(2 * BLOCK)[None, :]
    n = np.maximum(i + BLOCK - j, 0)
    max_exact = N_BUCKETS // 2
    large = max_exact + (np.log(np.maximum(n, 1) / max_exact) / math.log(MAX_DISTANCE / max_exact)
                         * (N_BUCKETS - max_exact)).astype(np.int32)
    large = np.minimum(large, N_BUCKETS - 1)
    return np.where(n < max_exact, n, large).astype(np.int32)


def band_mask(n_blocks):
    i = jnp.arange(BLOCK)[:, None]
    j = jnp.arange(2 * BLOCK)[None, :]
    rel = i + BLOCK - j
    band = (rel >= 0) & (rel < WINDOW)
    blk = jnp.arange(n_blocks)[:, None, None]
    return band[None] & ((blk > 0) | (j[None] >= BLOCK))


def causal_multiscale_pool(u):
    S = u.shape[1]
    uf = u.astype(jnp.float32)
    csp = jnp.pad(jnp.cumsum(uf, axis=1), ((0, 0), (1, 0), (0, 0)))
    t = jnp.arange(S)
    outs = []
    for g, w in enumerate(POOL_WINDOWS):
        c_g = csp[..., g * GROUP_WIDTH:(g + 1) * GROUP_WIDTH]
        upper = c_g[:, 1:]
        lower = jnp.pad(c_g, ((0, 0), (w, 0), (0, 0)))[:, 1:S + 1]
        count = jnp.minimum(t + 1, w).astype(jnp.float32)[None, :, None]
        outs.append((upper - lower) / count)
    return (jnp.concatenate(outs, axis=-1) - uf).astype(u.dtype)


def pool_mixer(h, w_in, w_group, scale, w_out):
    B, S, _ = h.shape
    u, z = jnp.split(h @ w_in, 2, axis=-1)
    p = causal_multiscale_pool(u).reshape(B, S, N_POOL_GROUPS, GROUP_WIDTH)
    y = jnp.einsum('bsgc,gcd->bsgd', p, w_group).reshape(B, S, A_WIDTH) * scale
    return (y * jax.nn.silu(z)) @ w_out


def shared_kv_blocks(x, c_act, kv_norm_g, kv_ada_w, kv_ada_b, w_kv):
    B, S, _ = x.shape
    nb = S // BLOCK
    shift, scale = jnp.split(c_act @ kv_ada_w + kv_ada_b, 2, axis=-1)
    hk = modulate(rmsnorm(x, kv_norm_g), shift, scale)
    kv = (hk @ w_kv).reshape(B, S, 2, N_KV_HEADS, HEAD_DIM)
    k, v = kv[:, :, 0], kv[:, :, 1]

    def to_band(t):
        prev = jnp.pad(t, ((0, 0), (BLOCK, 0), (0, 0), (0, 0)))[:, :S]
        prev = prev.reshape(B, nb, BLOCK, N_KV_HEADS, HEAD_DIM)
        cur = t.reshape(B, nb, BLOCK, N_KV_HEADS, HEAD_DIM)
        return jnp.concatenate([prev, cur], axis=2).transpose(1, 0, 2, 3, 4)

    return to_band(k), to_band(v)


def swa_sink_mixer(h, kb, vb, pos_bias, mask, w_in, sinks, w_out):
    B, S, _ = h.shape
    nb = S // BLOCK
    q, z = jnp.split(h @ w_in, 2, axis=-1)
    q = q * (HEAD_DIM ** -0.5)
    qb = q.reshape(B, nb, BLOCK, N_KV_HEADS, GQA_GROUPS, HEAD_DIM).transpose(1, 0, 2, 3, 4, 5)
    sink = sinks.astype(jnp.float32).reshape(N_KV_HEADS, GQA_GROUPS)[None, :, :, None, None]

    def block_fn(args):
        qn, kn, vn, mn = args
        s = jnp.einsum('bqhgd,bkhd->bhgqk', qn, kn).astype(jnp.float32) + pos_bias
        s = jnp.where(mn[None, None, None], s, NEG_INF)
        m = jnp.maximum(jnp.max(s, axis=-1, keepdims=True), sink)
        p = jnp.exp(s - m)
        p = p / (jnp.sum(p, axis=-1, keepdims=True) + jnp.exp(sink - m))
        return jnp.einsum('bhgqk,bkhd->bqhgd', p.astype(vn.dtype), vn)

    o = lax.map(block_fn, (qb, kb, vb, mask))
    o = o.transpose(1, 0, 2, 3, 4, 5).reshape(B, S, B_WIDTH)
    return (o * jax.nn.silu(z)) @ w_out


def setup_inputs(seed: int = 0) -> dict:
    key = jax.random.key(seed)
    ks = jax.random.split(key, 18)
    D, E, GW = D_MODEL, A_WIDTH, GROUP_WIDTH
    nrm = jax.random.normal
    f32 = jnp.float32
    return {
        "x": nrm(ks[0], (BATCH, SEQ, D), f32),
        "c": nrm(ks[1], (BATCH, D), f32),
        "norm_g": 1.0 + 0.05 * nrm(ks[2], (DEPTH, D), f32),
        "ada_w": 0.5 * D ** -0.5 * nrm(ks[3], (DEPTH, D, 3 * D), f32),
        "ada_b": 0.02 * nrm(ks[4], (DEPTH, 3 * D), f32),
        "a_w_in": D ** -0.5 * nrm(ks[5], (N_A, D, 2 * E), f32),
        "a_w_group": GW ** -0.5 * nrm(ks[6], (N_A, N_POOL_GROUPS, GW, GW), f32),
        "a_scale": 1.0 + 0.1 * nrm(ks[7], (N_A, E), f32),
        "a_w_out": E ** -0.5 * nrm(ks[8], (N_A, E, D), f32),
        "kv_norm_g": 1.0 + 0.05 * nrm(ks[9], (D,), f32),
        "kv_ada_w": 0.5 * D ** -0.5 * nrm(ks[10], (D, 2 * D), f32),
        "kv_ada_b": 0.02 * nrm(ks[11], (2 * D,), f32),
        "w_kv": D ** -0.5 * nrm(ks[12], (D, 2 * N_KV_HEADS * HEAD_DIM), f32),
        "b_w_in": D ** -0.5 * nrm(ks[13], (N_B, D, 2 * B_WIDTH), f32),
        "b_sinks": 0.5 * nrm(ks[14], (N_B, N_HEADS), f32),
        "b_w_out": B_WIDTH ** -0.5 * nrm(ks[15], (N_B, B_WIDTH, D), f32),
        "rel_bias": 0.5 * nrm(ks[16], (N_BUCKETS, N_HEADS), f32),
        "final_g": 1.0 + 0.05 * nrm(ks[17], (D,), f32),
    }


def reference(x, c, norm_g, ada_w, ada_b, a_w_in, a_w_group, a_scale, a_w_out,
              kv_norm_g, kv_ada_w, kv_ada_b, w_kv, b_w_in, b_sinks, b_w_out,
              rel_bias, final_g):
    S = x.shape[1]
    nb = S // BLOCK
    c_act = jax.nn.silu(c)
    buckets = t5_causal_buckets()
    pos_bias = rel_bias[buckets].astype(jnp.float32).transpose(2, 0, 1)
    pos_bias = pos_bias.reshape(N_KV_HEADS, GQA_GROUPS, BLOCK, 2 * BLOCK)[None]
    mask = band_mask(nb)

    kb = vb = None
    for l in range(DEPTH):
        shift, scale, gate = jnp.split(c_act @ ada_w[l] + ada_b[l], 3, axis=-1)
        if l < N_A:
            h = modulate(rmsnorm(x, norm_g[l]), shift, scale)
            y = pool_mixer(h, a_w_in[l], a_w_group[l], a_scale[l], a_w_out[l])
        else:
            if l == N_A:
                kb, vb = shared_kv_blocks(x, c_act, kv_norm_g, kv_ada_w, kv_ada_b, w_kv)
            j = l - N_A
            h = modulate(rmsnorm(x, norm_g[l]), shift, scale)
            y = swa_sink_mixer(h, kb, vb, pos_bias, mask, b_w_in[j], b_sinks[j], b_w_out[j])
        x = x + gate[:, None, :] * y
    return rmsnorm(x, final_g)
```

```python
import functools
import math

import numpy as np
import jax
import jax.numpy as jnp
from jax import lax
from jax.experimental import pallas as pl
from jax.experimental.pallas import tpu as pltpu

F32 = jnp.float32
BF16 = jnp.bfloat16

D_MODEL = 1024
A_WIDTH = 2048
POOL_WINDOWS = (2, 4, 8, 16)
N_GROUPS = len(POOL_WINDOWS)
GROUP_WIDTH = A_WIDTH // N_GROUPS
HEAD_DIM = 64
N_HEADS = 16
N_KV_HEADS = 2
GQA_GROUPS = N_HEADS // N_KV_HEADS
N_PAIRS = GQA_GROUPS // 2
B_WIDTH = N_HEADS * HEAD_DIM
BLOCK = 128
N_BUCKETS = 32
MAX_DISTANCE = 128
EPS = 1e-6
NEG_INF = -1e30

LANES = 128
SUBLANES = 8
VMEM_LIMIT_BYTES = 56 * 1024 * 1024

MAX_HALO = 16
SEQ_TILE = 512
KV_TILE = 1024
NORM_ROWS = 32
POOL_ROWS = 64


def _sigmoid(v):
    return 1.0 / (1.0 + jnp.exp(-v))


def _const_spec(shape):
    zeros = (0,) * len(shape)
    return pl.BlockSpec(shape, lambda *_: zeros, pipeline_mode=pl.Buffered(1))


def _mod_kernel(c_ref, w_ref, b_ref, o_ref):
    c = c_ref[...]
    c_act = c * _sigmoid(c)
    o_ref[0] = jnp.dot(c_act, w_ref[0], precision=lax.Precision.HIGHEST,
                       preferred_element_type=F32) + b_ref[0]


def _modulation(c_pad, w, b):
    n_layers, d, n = w.shape
    rows = c_pad.shape[0]
    nt = D_MODEL
    return pl.pallas_call(
        _mod_kernel,
        grid=(n_layers, n // nt),
        in_specs=[
            pl.BlockSpec((rows, d), lambda l, j: (0, 0)),
            pl.BlockSpec((1, d, nt), lambda l, j: (l, 0, j)),
            pl.BlockSpec((1, 1, nt), lambda l, j: (l, 0, j)),
        ],
        out_specs=pl.BlockSpec((1, rows, nt), lambda l, j: (l, 0, j)),
        out_shape=jax.ShapeDtypeStruct((n_layers, rows, n), F32),
        compiler_params=pltpu.CompilerParams(
            dimension_semantics=("arbitrary", "arbitrary"), vmem_limit_bytes=VMEM_LIMIT_BYTES),
        name="adaln_modulation",
    )(c_pad, w, b)


def _norm_modulate(x_ref, g_ref, shift_ref, scale_ref, h_ref, rows):
    g = g_ref[...]
    scale1 = 1.0 + scale_ref[0]
    shift = shift_ref[0]
    for r in range(0, rows, NORM_ROWS):
        xv = x_ref[0, r:r + NORM_ROWS, :]
        ms = jnp.mean(xv * xv, axis=-1, keepdims=True)
        xn = xv * lax.rsqrt(ms + EPS)
        h_ref[r:r + NORM_ROWS, :] = ((xn * g) * scale1 + shift).astype(BF16)


def _pool_kernel(x_ref, shift_ref, scale_ref, gate_ref, g_ref, win_ref, wg_ref, asc_ref, wout_ref,
                 o_ref, h_buf, u_ext, z_buf, p_buf, y_buf, gated_buf, carry):
    t = pl.program_id(1)
    ts = SEQ_TILE

    @pl.when(t == 0)
    def _():
        carry[...] = jnp.zeros_like(carry)

    _norm_modulate(x_ref, g_ref, shift_ref, scale_ref, h_buf, ts)
    h = h_buf[...]

    for g, w in enumerate(POOL_WINDOWS):
        cols = slice(g * GROUP_WIDTH, (g + 1) * GROUP_WIDTH)
        zcols = slice(A_WIDTH + g * GROUP_WIDTH, A_WIDTH + (g + 1) * GROUP_WIDTH)
        u_ext[g, 0:MAX_HALO, :] = carry[g]
        u_ext[g, MAX_HALO:, :] = jnp.dot(h, win_ref[:, cols], preferred_element_type=F32)
        carry[g] = u_ext[g, ts:ts + MAX_HALO, :]
        z_buf[g] = jnp.dot(h, win_ref[:, zcols], preferred_element_type=F32)

        halo = MAX_HALO if w > SUBLANES else SUBLANES
        for r in range(0, ts, POOL_ROWS):
            ev = u_ext[g, MAX_HALO + r - halo:MAX_HALO + r + POOL_ROWS, :]
            s = ev
            k = 1
            while k < w:
                s = s + pltpu.roll(s, k, 0)
                k *= 2
            s = s[halo:]
            u = ev[halo:]
            if r == 0:
                pos = t * ts + lax.broadcasted_iota(jnp.int32, (POOL_ROWS, GROUP_WIDTH), 0)
                mean = s * (1.0 / jnp.minimum(pos + 1, w).astype(F32))
            else:
                mean = s * (1.0 / w)
            p_buf[g, r:r + POOL_ROWS, :] = (mean - u).astype(BF16)

        y_buf[g] = jnp.dot(p_buf[g], wg_ref[g], preferred_element_type=F32)
        a_scale = asc_ref[:, cols]
        for r in range(0, ts, NORM_ROWS):
            z = z_buf[g, r:r + NORM_ROWS, :]
            y = y_buf[g, r:r + NORM_ROWS, :] * a_scale
            gated_buf[r:r + NORM_ROWS, cols] = (y * (z * _sigmoid(z))).astype(BF16)

    y = jnp.dot(gated_buf[...], wout_ref[...], preferred_element_type=F32)
    o_ref[0] = x_ref[0] + gate_ref[0] * y


def _pool_layer(x, shift, scale, gate, norm_g, w_in, w_group, a_scale, w_out):
    b, s, d = x.shape
    ts = SEQ_TILE
    vec_spec = pl.BlockSpec((1, 1, d), lambda i, j: (i, 0, 0))
    return pl.pallas_call(
        _pool_kernel,
        grid=(b, s // ts),
        in_specs=[
            pl.BlockSpec((1, ts, d), lambda i, j: (i, j, 0)),
            vec_spec, vec_spec, vec_spec,
            _const_spec((1, d)),
            _const_spec((d, 2 * A_WIDTH)),
            _const_spec((N_GROUPS, GROUP_WIDTH, GROUP_WIDTH)),
            _const_spec((1, A_WIDTH)),
            _const_spec((A_WIDTH, d)),
        ],
        out_specs=pl.BlockSpec((1, ts, d), lambda i, j: (i, j, 0)),
        out_shape=jax.ShapeDtypeStruct((b, s, d), F32),
        scratch_shapes=[
            pltpu.VMEM((ts, d), BF16),
            pltpu.VMEM((N_GROUPS, ts + MAX_HALO, GROUP_WIDTH), F32),
            pltpu.VMEM((N_GROUPS, ts, GROUP_WIDTH), F32),
            pltpu.VMEM((N_GROUPS, ts, GROUP_WIDTH), BF16),
            pltpu.VMEM((N_GROUPS, ts, GROUP_WIDTH), F32),
            pltpu.VMEM((ts, A_WIDTH), BF16),
            pltpu.VMEM((N_GROUPS, MAX_HALO, GROUP_WIDTH), F32),
        ],
        compiler_params=pltpu.CompilerParams(
            dimension_semantics=("arbitrary", "arbitrary"), vmem_limit_bytes=VMEM_LIMIT_BYTES),
        name="pool_layer",
    )(x, shift, scale, gate, norm_g, w_in, w_group, a_scale, w_out)


def _kv_kernel(x_ref, shift_ref, scale_ref, g_ref, wkv_ref, o_ref, h_buf):
    _norm_modulate(x_ref, g_ref, shift_ref, scale_ref, h_buf, KV_TILE)
    kv = jnp.dot(h_buf[...], wkv_ref[...], preferred_element_type=F32)
    k = kv[:, :LANES]
    v = kv[:, LANES:]
    o_ref[0, :, 0 * LANES:1 * LANES] = k.astype(BF16)
    o_ref[0, :, 1 * LANES:2 * LANES] = pltpu.roll(k, HEAD_DIM, 1).astype(BF16)
    o_ref[0, :, 2 * LANES:3 * LANES] = v.astype(BF16)
    o_ref[0, :, 3 * LANES:4 * LANES] = pltpu.roll(v, HEAD_DIM, 1).astype(BF16)


def _kv_layer(x, shift, scale, norm_g, w_kv):
    b, s, d = x.shape
    ts = KV_TILE
    vec_spec = pl.BlockSpec((1, 1, d), lambda i, j: (i, 0, 0))
    return pl.pallas_call(
        _kv_kernel,
        grid=(b, s // ts),
        in_specs=[
            pl.BlockSpec((1, ts, d), lambda i, j: (i, j, 0)),
            vec_spec, vec_spec,
            _const_spec((1, d)),
            _const_spec((d, 2 * N_KV_HEADS * HEAD_DIM)),
        ],
        out_specs=pl.BlockSpec((1, ts, 4 * LANES), lambda i, j: (i, j, 0)),
        out_shape=jax.ShapeDtypeStruct((b, s, 4 * LANES), BF16),
        scratch_shapes=[pltpu.VMEM((ts, d), BF16)],
        compiler_params=pltpu.CompilerParams(
            dimension_semantics=("arbitrary", "arbitrary"), vmem_limit_bytes=VMEM_LIMIT_BYTES),
        name="shared_kv",
    )(x, shift, scale, norm_g, w_kv)


def _t5_causal_buckets():
    i = np.arange(BLOCK)[:, None]
    j = np.arange(2 * BLOCK)[None, :]
    n = np.maximum(i + BLOCK - j, 0)
    max_exact = N_BUCKETS // 2
    large = max_exact + (np.log(np.maximum(n, 1) / max_exact) / math.log(MAX_DISTANCE / max_exact)
                         * (N_BUCKETS - max_exact)).astype(np.int32)
    large = np.minimum(large, N_BUCKETS - 1)
    return np.where(n < max_exact, n, large).astype(np.int32)


def _band_codes():
    i = np.arange(BLOCK)[:, None]
    j = np.arange(2 * BLOCK)[None, :]
    rel = i + BLOCK - j
    band = (rel >= 0) & (rel < BLOCK)
    code = np.where(band, np.where(j < BLOCK, 1.0, 0.0), 2.0).astype(np.float32)
    return np.tile(code, (N_PAIRS, 2))


def _bias_kernel(bucket_ref, rel_ref, o_ref):
    bucket = bucket_ref[...]
    for head in range(N_HEADS):
        acc = jnp.zeros((BLOCK, 2 * BLOCK), F32)
        for b in range(N_BUCKETS):
            acc = jnp.where(bucket == b, rel_ref[b, head], acc)
        kv_head, g = divmod(head, GQA_GROUPS)
        pair, half = divmod(g, 2)
        o_ref[kv_head, pair * BLOCK:(pair + 1) * BLOCK, half * 2 * BLOCK:(half + 1) * 2 * BLOCK] = acc


def _position_bias(rel_bias):
    buckets = jnp.asarray(_t5_causal_buckets())
    return pl.pallas_call(
        _bias_kernel,
        in_specs=[
            pl.BlockSpec(memory_space=pltpu.VMEM),
            pl.BlockSpec(memory_space=pltpu.SMEM),
        ],
        out_specs=pl.BlockSpec(memory_space=pltpu.VMEM),
        out_shape=jax.ShapeDtypeStruct((N_KV_HEADS, N_PAIRS * BLOCK, 4 * BLOCK), F32),
        name="position_bias",
    )(buckets, rel_bias)


def _attn_kernel(x_ref, shift_ref, scale_ref, gate_ref, g_ref, win_ref, kvp_ref, kvc_ref, bias_ref,
                 code_ref, sink_ref, wout_ref, fg_ref, o_ref, h_buf, q_buf, z_buf, og_buf, *, final_norm):
    t = pl.program_id(1)
    ts = SEQ_TILE

    _norm_modulate(x_ref, g_ref, shift_ref, scale_ref, h_buf, ts)
    h = h_buf[...]
    q = jnp.dot(h, win_ref[:, :B_WIDTH], preferred_element_type=F32)
    q_buf[...] = (q * (HEAD_DIM ** -0.5)).astype(BF16)
    z_buf[...] = jnp.dot(h, win_ref[:, B_WIDTH:], preferred_element_type=F32)

    lane = lax.broadcasted_iota(jnp.int32, (2 * BLOCK, LANES), 1)
    low = lane < HEAD_DIM
    out_lane_low = lax.broadcasted_iota(jnp.int32, (BLOCK, LANES), 1) < HEAD_DIM
    codes = code_ref[...]

    for i in range(ts // BLOCK):
        rows = slice(i * BLOCK, (i + 1) * BLOCK)
        prev = kvp_ref[0] if i == 0 else kvc_ref[0, (i - 1) * BLOCK:i * BLOCK, :]
        kv = jnp.concatenate([prev, kvc_ref[0, rows, :]], axis=0)
        limit = jnp.where(t == 0, 1.0, 2.0) if i == 0 else 2.0
        valid = codes < limit
        for kvh in range(N_KV_HEADS):
            own, other = (0, 1) if kvh == 0 else (1, 0)
            zeros = jnp.zeros((2 * BLOCK, LANES), BF16)
            k_bd = jnp.concatenate([
                jnp.where(low, kv[:, own * LANES:(own + 1) * LANES], zeros),
                jnp.where(low, zeros, kv[:, other * LANES:(other + 1) * LANES])], axis=0)
            v_bd = jnp.concatenate([
                jnp.where(low, kv[:, (2 + own) * LANES:(3 + own) * LANES], zeros),
                jnp.where(low, zeros, kv[:, (2 + other) * LANES:(3 + other) * LANES])], axis=0)
            slabs = [kvh * N_PAIRS + p for p in range(N_PAIRS)]
            qs = jnp.concatenate([q_buf[rows, sl * LANES:(sl + 1) * LANES] for sl in slabs], axis=0)
            s = lax.dot_general(qs, k_bd, (((1,), (1,)), ((), ())), preferred_element_type=F32)
            s = jnp.where(valid, s + bias_ref[kvh], NEG_INF)
            probs = []
            inv_den = []
            for p in range(N_PAIRS):
                row_p = []
                for half in range(2):
                    sink = sink_ref[kvh * GQA_GROUPS + 2 * p + half]
                    sp = s[p * BLOCK:(p + 1) * BLOCK, half * 2 * BLOCK:(half + 1) * 2 * BLOCK]
                    m = jnp.maximum(jnp.max(sp, axis=-1, keepdims=True), sink)
                    e = jnp.exp(sp - m)
                    den = jnp.sum(e, axis=-1, keepdims=True) + jnp.exp(sink - m)
                    row_p.append(e.astype(BF16))
                    inv_den.append(1.0 / den)
                probs.append(jnp.concatenate(row_p, axis=1))
            pm = jnp.concatenate(probs, axis=0)
            o = jnp.dot(pm, v_bd, preferred_element_type=F32)
            for p in range(N_PAIRS):
                sl = slabs[p]
                r = jnp.where(out_lane_low, inv_den[2 * p], inv_den[2 * p + 1])
                z = z_buf[rows, sl * LANES:(sl + 1) * LANES]
                og = (o[p * BLOCK:(p + 1) * BLOCK] * r) * (z * _sigmoid(z))
                og_buf[rows, sl * LANES:(sl + 1) * LANES] = og.astype(BF16)

    y = jnp.dot(og_buf[...], wout_ref[...], preferred_element_type=F32)
    out = x_ref[0] + gate_ref[0] * y
    if final_norm:
        ms = jnp.mean(out * out, axis=-1, keepdims=True)
        out = (out * lax.rsqrt(ms + EPS)) * fg_ref[...]
    o_ref[0] = out


def _attn_layer(x, shift, scale, gate, norm_g, w_in, kv, bias, codes, sinks, w_out, final_g, final_norm):
    b, s, d = x.shape
    ts = SEQ_TILE
    blocks_per_tile = ts // BLOCK
    vec_spec = pl.BlockSpec((1, 1, d), lambda i, j: (i, 0, 0))
    return pl.pallas_call(
        functools.partial(_attn_kernel, final_norm=final_norm),
        grid=(b, s // ts),
        in_specs=[
            pl.BlockSpec((1, ts, d), lambda i, j: (i, j, 0)),
            vec_spec, vec_spec, vec_spec,
            _const_spec((1, d)),
            _const_spec((d, 2 * B_WIDTH)),
            pl.BlockSpec((1, BLOCK, 4 * LANES),
                         lambda i, j: (i, jnp.maximum(j * blocks_per_tile - 1, 0), 0)),
            pl.BlockSpec((1, ts, 4 * LANES), lambda i, j: (i, j, 0)),
            _const_spec((N_KV_HEADS, N_PAIRS * BLOCK, 4 * BLOCK)),
            _const_spec((N_PAIRS * BLOCK, 4 * BLOCK)),
            pl.BlockSpec(memory_space=pltpu.SMEM),
            _const_spec((B_WIDTH, d)),
            _const_spec((1, d)),
        ],
        out_specs=pl.BlockSpec((1, ts, d), lambda i, j: (i, j, 0)),
        out_shape=jax.ShapeDtypeStruct((b, s, d), F32),
        scratch_shapes=[
            pltpu.VMEM((ts, d), BF16),
            pltpu.VMEM((ts, B_WIDTH), BF16),
            pltpu.VMEM((ts, B_WIDTH), F32),
            pltpu.VMEM((ts, B_WIDTH), BF16),
        ],
        compiler_params=pltpu.CompilerParams(
            dimension_semantics=("arbitrary", "arbitrary"), vmem_limit_bytes=VMEM_LIMIT_BYTES),
        name="attn_layer",
    )(x, shift, scale, gate, norm_g, w_in, kv, kv, bias, codes, sinks, w_out, final_g)


def kernel(x, c, norm_g, ada_w, ada_b, a_w_in, a_w_group, a_scale, a_w_out, kv_norm_g, kv_ada_w,
           kv_ada_b, w_kv, b_w_in, b_sinks, b_w_out, rel_bias, final_g):
    b, s, d = x.shape
    depth = norm_g.shape[0]
    n_a = a_w_in.shape[0]
    assert (d, s % SEQ_TILE, s % KV_TILE) == (D_MODEL, 0, 0) and b <= SUBLANES

    c_pad = jnp.pad(c, ((0, SUBLANES - b), (0, 0)))
    mod = _modulation(c_pad, ada_w, ada_b.reshape(depth, 1, 3 * d))[:, :b]
    mod_kv = _modulation(c_pad, kv_ada_w[None], kv_ada_b.reshape(1, 1, 2 * d))[0, :b]

    def vec(m, k):
        return m[:, k * d:(k + 1) * d].reshape(b, 1, d)

    for l in range(n_a):
        x = _pool_layer(x, vec(mod[l], 0), vec(mod[l], 1), vec(mod[l], 2), norm_g[l].reshape(1, d),
                        a_w_in[l].astype(BF16), a_w_group[l].astype(BF16),
                        a_scale[l].reshape(1, A_WIDTH), a_w_out[l].astype(BF16))

    kv = _kv_layer(x, vec(mod_kv, 0), vec(mod_kv, 1), kv_norm_g.reshape(1, d), w_kv.astype(BF16))
    bias = _position_bias(rel_bias)
    codes = jnp.asarray(_band_codes())
    for l in range(n_a, depth):
        j = l - n_a
        x = _attn_layer(x, vec(mod[l], 0), vec(mod[l], 1), vec(mod[l], 2), norm_g[l].reshape(1, d),
                        b_w_in[j].astype(BF16), kv, bias, codes, b_sinks[j], b_w_out[j].astype(BF16),
                        final_g.reshape(1, d), final_norm=(l == depth - 1))
    return x
```

```python
import functools
import math

import numpy as np
import jax
import jax.numpy as jnp
from jax import lax
from jax.experimental import pallas as pl
from jax.experimental.pallas import tpu as pltpu

F32 = jnp.float32
BF16 = jnp.bfloat16

D_MODEL = 1024
A_WIDTH = 2048
POOL_WINDOWS = (2, 4, 8, 16)
N_GROUPS = len(POOL_WINDOWS)
GROUP_WIDTH = A_WIDTH // N_GROUPS
HEAD_DIM = 64
N_HEADS = 16
N_KV_HEADS = 2
GQA_GROUPS = N_HEADS // N_KV_HEADS
KV_WIDTH = N_KV_HEADS * HEAD_DIM
B_WIDTH = N_HEADS * HEAD_DIM
BLOCK = 128
N_BUCKETS = 32
MAX_DISTANCE = 128
EPS = 1e-6
NEG_INF = -1e30
LOG2E = math.log2(math.e)

LANES = 128
SUBLANES = 8
VMEM_LIMIT_BYTES = 56 * 1024 * 1024

MAX_HALO = 16
SEQ_TILE = 512
ATTN_TILE = 1024
ATTN_CHUNK = 256
PROJ_PIECE = 256
KV_TILE = 1024
NORM_ROWS = 32
POOL_ROWS = 64

_NT_DIMS = (((1,), (1,)), ((), ()))


def _sigmoid(v):
    return 1.0 / (1.0 + jnp.exp(-v))


def _const_spec(shape):
    zeros = (0,) * len(shape)
    return pl.BlockSpec(shape, lambda *_: zeros, pipeline_mode=pl.Buffered(1))


def _mod_kernel(c_ref, w_ref, b_ref, o_ref):
    c = c_ref[...]
    c_act = c * _sigmoid(c)
    o_ref[0] = jnp.dot(c_act, w_ref[0], precision=lax.Precision.HIGHEST,
                       preferred_element_type=F32) + b_ref[0]


def _modulation(c_pad, w, b):
    n_layers, d, n = w.shape
    rows = c_pad.shape[0]
    nt = D_MODEL
    return pl.pallas_call(
        _mod_kernel,
        grid=(n_layers, n // nt),
        in_specs=[
            pl.BlockSpec((rows, d), lambda l, j: (0, 0)),
            pl.BlockSpec((1, d, nt), lambda l, j: (l, 0, j)),
            pl.BlockSpec((1, 1, nt), lambda l, j: (l, 0, j)),
        ],
        out_specs=pl.BlockSpec((1, rows, nt), lambda l, j: (l, 0, j)),
        out_shape=jax.ShapeDtypeStruct((n_layers, rows, n), F32),
        compiler_params=pltpu.CompilerParams(
            dimension_semantics=("arbitrary", "arbitrary"), vmem_limit_bytes=VMEM_LIMIT_BYTES),
        name="adaln_modulation",
    )(c_pad, w, b)


def _zero_after(v):
    u = pltpu.bitcast(v, jnp.uint32)
    return pltpu.bitcast((u >> 16) >> 16, F32)


def _norm_modulate(x_ref, g_ref, shift_ref, scale_ref, h_ref, rows, start=0, after=None):
    g = g_ref[...]
    scale1 = 1.0 + scale_ref[0]
    shift = shift_ref[0]
    if after is not None:
        shift = shift + _zero_after(after)
    for r in range(start, start + rows, NORM_ROWS):
        xv = x_ref[0, r:r + NORM_ROWS, :]
        ms = jnp.mean(xv * xv, axis=-1, keepdims=True)
        xn = xv * lax.rsqrt(ms + EPS)
        h_ref[r:r + NORM_ROWS, :] = ((xn * g) * scale1 + shift).astype(BF16)


def _pool_kernel(x_ref, shift_ref, scale_ref, gate_ref, g_ref, win_ref, wg_ref, asc_ref, wout_ref,
                 o_ref, h_buf, u_ext, z_buf, p_buf, y_buf, gated_buf, carry):
    t = pl.program_id(1)
    ts = SEQ_TILE

    @pl.when(t == 0)
    def _():
        carry[...] = jnp.zeros_like(carry)

    _norm_modulate(x_ref, g_ref, shift_ref, scale_ref, h_buf, ts)
    h = h_buf[...]

    for g, w in enumerate(POOL_WINDOWS):
        cols = slice(g * GROUP_WIDTH, (g + 1) * GROUP_WIDTH)
        zcols = slice(A_WIDTH + g * GROUP_WIDTH, A_WIDTH + (g + 1) * GROUP_WIDTH)
        u_ext[g, 0:MAX_HALO, :] = carry[g]
        u_ext[g, MAX_HALO:, :] = jnp.dot(h, win_ref[:, cols], preferred_element_type=F32)
        carry[g] = u_ext[g, ts:ts + MAX_HALO, :]
        z_buf[g] = jnp.dot(h, win_ref[:, zcols], preferred_element_type=F32)

        halo = MAX_HALO if w > SUBLANES else SUBLANES
        for r in range(0, ts, POOL_ROWS):
            ev = u_ext[g, MAX_HALO + r - halo:MAX_HALO + r + POOL_ROWS, :]
            s = ev
            k = 1
            while k < w:
                s = s + pltpu.roll(s, k, 0)
                k *= 2
            s = s[halo:]
            u = ev[halo:]
            if r == 0:
                pos = t * ts + lax.broadcasted_iota(jnp.int32, (POOL_ROWS, GROUP_WIDTH), 0)
                mean = s * (1.0 / jnp.minimum(pos + 1, w).astype(F32))
            else:
                mean = s * (1.0 / w)
            p_buf[g, r:r + POOL_ROWS, :] = (mean - u).astype(BF16)

        y_buf[g] = jnp.dot(p_buf[g], wg_ref[g], preferred_element_type=F32)
        a_scale = asc_ref[:, cols]
        for r in range(0, ts, NORM_ROWS):
            z = z_buf[g, r:r + NORM_ROWS, :]
            y = y_buf[g, r:r + NORM_ROWS, :] * a_scale
            gated_buf[r:r + NORM_ROWS, cols] = (y * (z * _sigmoid(z))).astype(BF16)

    y = jnp.dot(gated_buf[...], wout_ref[...], preferred_element_type=F32)
    o_ref[0] = x_ref[0] + gate_ref[0] * y


def _pool_layer(x, shift, scale, gate, norm_g, w_in, w_group, a_scale, w_out):
    b, s, d = x.shape
    ts = SEQ_TILE
    vec_spec = pl.BlockSpec((1, 1, d), lambda i, j: (i, 0, 0))
    return pl.pallas_call(
        _pool_kernel,
        grid=(b, s // ts),
        in_specs=[
            pl.BlockSpec((1, ts, d), lambda i, j: (i, j, 0)),
            vec_spec, vec_spec, vec_spec,
            _const_spec((1, d)),
            _const_spec((d, 2 * A_WIDTH)),
            _const_spec((N_GROUPS, GROUP_WIDTH, GROUP_WIDTH)),
            _const_spec((1, A_WIDTH)),
            _const_spec((A_WIDTH, d)),
        ],
        out_specs=pl.BlockSpec((1, ts, d), lambda i, j: (i, j, 0)),
        out_shape=jax.ShapeDtypeStruct((b, s, d), F32),
        scratch_shapes=[
            pltpu.VMEM((ts, d), BF16),
            pltpu.VMEM((N_GROUPS, ts + MAX_HALO, GROUP_WIDTH), F32),
            pltpu.VMEM((N_GROUPS, ts, GROUP_WIDTH), F32),
            pltpu.VMEM((N_GROUPS, ts, GROUP_WIDTH), BF16),
            pltpu.VMEM((N_GROUPS, ts, GROUP_WIDTH), F32),
            pltpu.VMEM((ts, A_WIDTH), BF16),
            pltpu.VMEM((N_GROUPS, MAX_HALO, GROUP_WIDTH), F32),
        ],
        compiler_params=pltpu.CompilerParams(
            dimension_semantics=("arbitrary", "arbitrary"), vmem_limit_bytes=VMEM_LIMIT_BYTES),
        name="pool_layer",
    )(x, shift, scale, gate, norm_g, w_in, w_group, a_scale, w_out)


def _kv_kernel(x_ref, shift_ref, scale_ref, g_ref, wk_ref, wvt_ref, k_ref, vt_ref, h_buf):
    _norm_modulate(x_ref, g_ref, shift_ref, scale_ref, h_buf, KV_TILE)
    h = h_buf[...]
    k_ref[0] = jnp.dot(h, wk_ref[...], preferred_element_type=F32).astype(BF16)
    vt_ref[0] = lax.dot_general(wvt_ref[...], h, _NT_DIMS,
                                preferred_element_type=F32).astype(BF16)


def _kv_layer(x, shift, scale, norm_g, w_k, w_v_t):
    b, s, d = x.shape
    ts = KV_TILE
    vec_spec = pl.BlockSpec((1, 1, d), lambda i, j: (i, 0, 0))
    return pl.pallas_call(
        _kv_kernel,
        grid=(b, s // ts),
        in_specs=[
            pl.BlockSpec((1, ts, d), lambda i, j: (i, j, 0)),
            vec_spec, vec_spec,
            _const_spec((1, d)),
            _const_spec((d, KV_WIDTH)),
            _const_spec((KV_WIDTH, d)),
        ],
        out_specs=[
            pl.BlockSpec((1, ts, KV_WIDTH), lambda i, j: (i, j, 0)),
            pl.BlockSpec((1, KV_WIDTH, ts), lambda i, j: (i, 0, j)),
        ],
        out_shape=[
            jax.ShapeDtypeStruct((b, s, KV_WIDTH), BF16),
            jax.ShapeDtypeStruct((b, KV_WIDTH, s), BF16),
        ],
        scratch_shapes=[pltpu.VMEM((ts, d), BF16)],
        compiler_params=pltpu.CompilerParams(
            dimension_semantics=("arbitrary", "arbitrary"), vmem_limit_bytes=VMEM_LIMIT_BYTES),
        name="shared_kv",
    )(x, shift, scale, norm_g, w_k, w_v_t)


def _t5_causal_buckets():
    i = np.arange(BLOCK)[:, None]
    j = np.arange(2 * BLOCK)[None, :]
    n = np.maximum(i + BLOCK - j, 0)
    max_exact = N_BUCKETS // 2
    large = max_exact + (np.log(np.maximum(n, 1) / max_exact) / math.log(MAX_DISTANCE / max_exact)
                         * (N_BUCKETS - max_exact)).astype(np.int32)
    large = np.minimum(large, N_BUCKETS - 1)
    return np.where(n < max_exact, n, large).astype(np.int32)


def _band_codes_t():
    i = np.arange(BLOCK)[None, :]
    j = np.arange(2 * BLOCK)[:, None]
    rel = i + BLOCK - j
    band = (rel >= 0) & (rel < BLOCK)
    code = np.where(band, np.where(j < BLOCK, 1.0, 0.0), 2.0).astype(np.float32)
    return np.tile(code, (1, GQA_GROUPS))


def _bias_kernel(bucket_ref, rel_ref, o_ref):
    bucket = bucket_ref[...]
    for head in range(N_HEADS):
        acc = jnp.zeros((2 * BLOCK, BLOCK), F32)
        for b in range(N_BUCKETS):
            acc = jnp.where(bucket == b, rel_ref[b, head], acc)
        kv_head, g = divmod(head, GQA_GROUPS)
        o_ref[kv_head, :, g * BLOCK:(g + 1) * BLOCK] = acc * LOG2E


def _position_bias_t(rel_bias):
    buckets_t = jnp.asarray(np.ascontiguousarray(_t5_causal_buckets().T))
    return pl.pallas_call(
        _bias_kernel,
        in_specs=[
            pl.BlockSpec(memory_space=pltpu.VMEM),
            pl.BlockSpec(memory_space=pltpu.SMEM),
        ],
        out_specs=pl.BlockSpec(memory_space=pltpu.VMEM),
        out_shape=jax.ShapeDtypeStruct((N_KV_HEADS, 2 * BLOCK, GQA_GROUPS * BLOCK), F32),
        name="position_bias",
    )(buckets_t, rel_bias)


def _attn_kernel(x_ref, shift_ref, scale_ref, gate_ref, g_ref, wqt_ref, wz_ref, kp_ref, kc_ref,
                 vtp_ref, vtc_ref, bias_ref, code_ref, sink_ref, wout_ref, fg_ref, o_ref,
                 h_buf, qt_buf, z_buf, og_buf, s_buf, p_buf, r_buf, *, final_norm):
    t = pl.program_id(1)
    n_chunks = ATTN_TILE // ATTN_CHUNK
    pieces = B_WIDTH // PROJ_PIECE
    steps_per_chunk = (ATTN_CHUNK // BLOCK) * N_KV_HEADS
    n_steps = n_chunks * steps_per_chunk
    norm_rows = ATTN_CHUNK // steps_per_chunk

    def norm_slice(c, j):
        _norm_modulate(x_ref, g_ref, shift_ref, scale_ref, h_buf, norm_rows,
                       start=c * ATTN_CHUNK + j * norm_rows)

    def project_piece(c, n):
        chunk = slice(c * ATTN_CHUNK, (c + 1) * ATTN_CHUNK)
        h = h_buf[chunk, :]
        if n < pieces:
            sl = slice(n * PROJ_PIECE, (n + 1) * PROJ_PIECE)
            qt = lax.dot_general(wqt_ref[sl, :], h, _NT_DIMS, preferred_element_type=F32)
            qt_buf[sl, chunk] = (qt * (HEAD_DIM ** -0.5 * LOG2E)).astype(BF16)
        else:
            sl = slice((n - pieces) * PROJ_PIECE, (n - pieces + 1) * PROJ_PIECE)
            z_buf[chunk, sl] = jnp.dot(h, wz_ref[:, sl], preferred_element_type=F32)

    def out_piece(c, n):
        chunk = slice(c * ATTN_CHUNK, (c + 1) * ATTN_CHUNK)
        sl = slice(n * PROJ_PIECE, (n + 1) * PROJ_PIECE)
        y = jnp.dot(og_buf[chunk, :], wout_ref[:, sl], preferred_element_type=F32)
        o_ref[0, chunk, sl] = x_ref[0, chunk, sl] + gate_ref[0, :, sl] * y

    def stage_scores(step):
        i, kvh = divmod(step, N_KV_HEADS)
        _scores(i, kvh, qt_buf, kp_ref, kc_ref, s_buf.at[step % 2])

    def stage_softmax(step):
        i, kvh = divmod(step, N_KV_HEADS)
        _softmax(i, kvh, t, s_buf.at[step % 2], code_ref, bias_ref, sink_ref,
                 p_buf.at[step % 2], r_buf.at[step % 2])

    def stage_pv(step):
        i, kvh = divmod(step, N_KV_HEADS)
        _pv_gate(i, kvh, p_buf.at[step % 2], r_buf.at[step % 2], vtp_ref, vtc_ref, z_buf, og_buf)

    norm_slices = [(c, j) for c in range(n_chunks) for j in range(steps_per_chunk)]
    for c, j in norm_slices[:2 * steps_per_chunk]:
        norm_slice(c, j)
    for n in range(2 * pieces):
        project_piece(0, n)

    def out_slices(ko):
        co, no = divmod(ko, steps_per_chunk)
        for n in range(no * pieces // steps_per_chunk, (no + 1) * pieces // steps_per_chunk):
            out_piece(co, n)

    out_lag = steps_per_chunk + 2
    for k in range(n_steps + 2):
        c, j = divmod(k, steps_per_chunk)
        if k < n_steps:
            stage_scores(k)
        if 1 <= k <= n_steps:
            stage_softmax(k - 1)
        if 2 <= k <= n_steps + 1:
            stage_pv(k - 2)
        if c + 2 < n_chunks:
            norm_slice(c + 2, j)
        if c + 1 < n_chunks:
            for n in range(j * 2 * pieces // steps_per_chunk, (j + 1) * 2 * pieces // steps_per_chunk):
                project_piece(c + 1, n)
        if k >= out_lag:
            out_slices(k - out_lag)
    for ko in range(n_steps + 2 - out_lag, n_steps):
        out_slices(ko)

    if final_norm:
        for r in range(0, ATTN_TILE, NORM_ROWS):
            out = o_ref[0, r:r + NORM_ROWS, :]
            ms = jnp.mean(out * out, axis=-1, keepdims=True)
            o_ref[0, r:r + NORM_ROWS, :] = (out * lax.rsqrt(ms + EPS)) * fg_ref[...]


def _scores(i, kvh, qt_buf, kp_ref, kc_ref, s_ref):
    rows = slice(i * BLOCK, (i + 1) * BLOCK)
    k_prev = kp_ref[0] if i == 0 else kc_ref[0, (i - 1) * BLOCK:i * BLOCK, :]
    k2 = jnp.concatenate([k_prev, kc_ref[0, rows, :]], axis=0)
    q_cat = jnp.concatenate(
        [qt_buf[(kvh * GQA_GROUPS + g) * HEAD_DIM:(kvh * GQA_GROUPS + g + 1) * HEAD_DIM, rows]
         for g in range(GQA_GROUPS)], axis=1)
    q_zeros = jnp.zeros((HEAD_DIM, GQA_GROUPS * BLOCK), BF16)
    q_sel = jnp.concatenate([q_cat, q_zeros] if kvh == 0 else [q_zeros, q_cat], axis=0)
    s_ref[...] = jnp.dot(k2, q_sel, preferred_element_type=F32)


def _softmax(i, kvh, t, s_ref, code_ref, bias_ref, sink_ref, p_ref, r_ref):
    limit = jnp.where(t == 0, 1.0, 2.0) if i == 0 else 2.0
    for g in range(GQA_GROUPS):
        cols = slice(g * BLOCK, (g + 1) * BLOCK)
        sink = sink_ref[kvh, :, cols]
        sg = jnp.where(code_ref[:, cols] < limit, s_ref[:, cols] + bias_ref[kvh, :, cols], NEG_INF)
        m = jnp.maximum(jnp.max(sg, axis=0, keepdims=True), sink)
        e = jnp.exp2(sg - m)
        den = jnp.sum(e, axis=0, keepdims=True) + jnp.exp2(sink - m)
        p_ref[:, cols] = e.astype(BF16)
        r_ref[:, cols] = 1.0 / den


def _pv_gate(i, kvh, p_ref, r_ref, vtp_ref, vtc_ref, z_buf, og_buf):
    rows = slice(i * BLOCK, (i + 1) * BLOCK)
    vt_prev = vtp_ref[0] if i == 0 else vtc_ref[0, :, (i - 1) * BLOCK:i * BLOCK]
    vt2 = jnp.concatenate([vt_prev, vtc_ref[0, :, rows]], axis=1)
    vt = vt2[kvh * HEAD_DIM:(kvh + 1) * HEAD_DIM, :]
    ot = jnp.dot(vt, p_ref[...], preferred_element_type=F32)
    ot = ot * r_ref[...]
    for p in range(GQA_GROUPS // 2):
        pair_t = jnp.concatenate([ot[:, (2 * p) * BLOCK:(2 * p + 1) * BLOCK],
                                  ot[:, (2 * p + 1) * BLOCK:(2 * p + 2) * BLOCK]], axis=0)
        slab = kvh * (GQA_GROUPS // 2) + p
        lanes = slice(slab * LANES, (slab + 1) * LANES)
        z = z_buf[rows, lanes]
        og_buf[rows, lanes] = (pair_t.T * (z * _sigmoid(z))).astype(BF16)


def _attn_layer(x, shift, scale, gate, norm_g, w_q_t, w_z, k, v_t, bias, codes, sinks, w_out, final_g,
                final_norm):
    b, s, d = x.shape
    ts = ATTN_TILE
    blocks_per_tile = ts // BLOCK
    vec_spec = pl.BlockSpec((1, 1, d), lambda i, j: (i, 0, 0))
    score_shape = (2, 2 * BLOCK, GQA_GROUPS * BLOCK)

    def prev_block(j):
        return jnp.maximum(j * blocks_per_tile - 1, 0)

    return pl.pallas_call(
        functools.partial(_attn_kernel, final_norm=final_norm),
        grid=(b, s // ts),
        in_specs=[
            pl.BlockSpec((1, ts, d), lambda i, j: (i, j, 0)),
            vec_spec, vec_spec, vec_spec,
            _const_spec((1, d)),
            _const_spec((B_WIDTH, d)),
            _const_spec((d, B_WIDTH)),
            pl.BlockSpec((1, BLOCK, KV_WIDTH), lambda i, j: (i, prev_block(j), 0)),
            pl.BlockSpec((1, ts, KV_WIDTH), lambda i, j: (i, j, 0)),
            pl.BlockSpec((1, KV_WIDTH, BLOCK), lambda i, j: (i, 0, prev_block(j))),
            pl.BlockSpec((1, KV_WIDTH, ts), lambda i, j: (i, 0, j)),
            _const_spec((N_KV_HEADS, 2 * BLOCK, GQA_GROUPS * BLOCK)),
            _const_spec((2 * BLOCK, GQA_GROUPS * BLOCK)),
            _const_spec((N_KV_HEADS, 1, GQA_GROUPS * BLOCK)),
            _const_spec((B_WIDTH, d)),
            _const_spec((1, d)),
        ],
        out_specs=pl.BlockSpec((1, ts, d), lambda i, j: (i, j, 0)),
        out_shape=jax.ShapeDtypeStruct((b, s, d), F32),
        scratch_shapes=[
            pltpu.VMEM((ts, d), BF16),
            pltpu.VMEM((B_WIDTH, ts), BF16),
            pltpu.VMEM((ts, B_WIDTH), F32),
            pltpu.VMEM((ts, B_WIDTH), BF16),
            pltpu.VMEM(score_shape, F32),
            pltpu.VMEM(score_shape, BF16),
            pltpu.VMEM((2, 1, GQA_GROUPS * BLOCK), F32),
        ],
        compiler_params=pltpu.CompilerParams(
            dimension_semantics=("arbitrary", "arbitrary"), vmem_limit_bytes=VMEM_LIMIT_BYTES),
        name="attn_layer",
    )(x, shift, scale, gate, norm_g, w_q_t, w_z, k, k, v_t, v_t, bias, codes, sinks, w_out, final_g)


def kernel(x, c, norm_g, ada_w, ada_b, a_w_in, a_w_group, a_scale, a_w_out, kv_norm_g, kv_ada_w,
           kv_ada_b, w_kv, b_w_in, b_sinks, b_w_out, rel_bias, final_g):
    b, s, d = x.shape
    depth = norm_g.shape[0]
    n_a = a_w_in.shape[0]
    assert (d, s % SEQ_TILE, s % ATTN_TILE, s % KV_TILE) == (D_MODEL, 0, 0, 0) and b <= SUBLANES

    c_pad = jnp.pad(c, ((0, SUBLANES - b), (0, 0)))
    mod = _modulation(c_pad, ada_w, ada_b.reshape(depth, 1, 3 * d))[:, :b]
    mod_kv = _modulation(c_pad, kv_ada_w[None], kv_ada_b.reshape(1, 1, 2 * d))[0, :b]

    def vec(m, k):
        return m[:, k * d:(k + 1) * d].reshape(b, 1, d)

    for l in range(n_a):
        x = _pool_layer(x, vec(mod[l], 0), vec(mod[l], 1), vec(mod[l], 2), norm_g[l].reshape(1, d),
                        a_w_in[l].astype(BF16), a_w_group[l].astype(BF16),
                        a_scale[l].reshape(1, A_WIDTH), a_w_out[l].astype(BF16))

    k, v_t = _kv_layer(x, vec(mod_kv, 0), vec(mod_kv, 1), kv_norm_g.reshape(1, d),
                       w_kv[:, :KV_WIDTH].astype(BF16), w_kv[:, KV_WIDTH:].T.astype(BF16))
    bias = _position_bias_t(rel_bias)
    codes = jnp.asarray(_band_codes_t())
    for l in range(n_a, depth):
        j = l - n_a
        sinks = jnp.repeat(b_sinks[j] * LOG2E, BLOCK).reshape(N_KV_HEADS, 1, GQA_GROUPS * BLOCK)
        x = _attn_layer(x, vec(mod[l], 0), vec(mod[l], 1), vec(mod[l], 2), norm_g[l].reshape(1, d),
                        b_w_in[j, :, :B_WIDTH].T.astype(BF16), b_w_in[j, :, B_WIDTH:].astype(BF16),
                        k, v_t, bias, codes, sinks, b_w_out[j].astype(BF16),
                        final_g.reshape(1, d), final_norm=(l == depth - 1))
    return x
```

```python
import functools
import math

import numpy as np
import jax
import jax.numpy as jnp
from jax import lax
from jax.experimental import pallas as pl
from jax.experimental.pallas import tpu as pltpu

F32 = jnp.float32
BF16 = jnp.bfloat16

D_MODEL = 1024
A_WIDTH = 2048
POOL_WINDOWS = (2, 4, 8, 16)
N_GROUPS = len(POOL_WINDOWS)
GROUP_WIDTH = A_WIDTH // N_GROUPS
HEAD_DIM = 64
N_HEADS = 16
N_KV_HEADS = 2
GQA_GROUPS = N_HEADS // N_KV_HEADS
KV_WIDTH = N_KV_HEADS * HEAD_DIM
B_WIDTH = N_HEADS * HEAD_DIM
BLOCK = 128
N_BUCKETS = 32
MAX_DISTANCE = 128
EPS = 1e-6
NEG_INF = -1e30
LOG2E = math.log2(math.e)

LANES = 128
SUBLANES = 8
VMEM_LIMIT_BYTES = 56 * 1024 * 1024

MAX_HALO = 16
SEQ_TILE = 512
ATTN_TILE = 1024
ATTN_CHUNK = 256
PROJ_PIECE = 256
KV_TILE = 1024
NORM_ROWS = 32
POOL_ROWS = 64

_NT_DIMS = (((1,), (1,)), ((), ()))


def _sigmoid(v):
    return 1.0 / (1.0 + jnp.exp(-v))


def _const_spec(shape):
    zeros = (0,) * len(shape)
    return pl.BlockSpec(shape, lambda *_: zeros, pipeline_mode=pl.Buffered(1))


def _mod_kernel(c_ref, w_ref, b_ref, o_ref):
    c = c_ref[...]
    c_act = c * _sigmoid(c)
    o_ref[0] = jnp.dot(c_act, w_ref[0], precision=lax.Precision.HIGHEST,
                       preferred_element_type=F32) + b_ref[0]


def _modulation(c_pad, w, b):
    n_layers, d, n = w.shape
    rows = c_pad.shape[0]
    nt = D_MODEL
    return pl.pallas_call(
        _mod_kernel,
        grid=(n_layers, n // nt),
        in_specs=[
            pl.BlockSpec((rows, d), lambda l, j: (0, 0)),
            pl.BlockSpec((1, d, nt), lambda l, j: (l, 0, j)),
            pl.BlockSpec((1, 1, nt), lambda l, j: (l, 0, j)),
        ],
        out_specs=pl.BlockSpec((1, rows, nt), lambda l, j: (l, 0, j)),
        out_shape=jax.ShapeDtypeStruct((n_layers, rows, n), F32),
        compiler_params=pltpu.CompilerParams(
            dimension_semantics=("arbitrary", "arbitrary"), vmem_limit_bytes=VMEM_LIMIT_BYTES),
        name="adaln_modulation",
    )(c_pad, w, b)


def _zero_after(v):
    u = pltpu.bitcast(v, jnp.uint32)
    return pltpu.bitcast((u >> 16) >> 16, F32)


def _norm_modulate(x_ref, g_ref, shift_ref, scale_ref, h_ref, rows, start=0, after=None):
    g = g_ref[...]
    scale1 = 1.0 + scale_ref[0]
    shift = shift_ref[0]
    if after is not None:
        shift = shift + _zero_after(after)
    for r in range(start, start + rows, NORM_ROWS):
        xv = x_ref[0, r:r + NORM_ROWS, :]
        ms = jnp.mean(xv * xv, axis=-1, keepdims=True)
        xn = xv * lax.rsqrt(ms + EPS)
        h_ref[r:r + NORM_ROWS, :] = ((xn * g) * scale1 + shift).astype(BF16)


def _pool_kernel(x_ref, shift_ref, scale_ref, gate_ref, g_ref, win_ref, wg_ref, asc_ref, wout_ref,
                 o_ref, h_buf, u_ext, z_buf, p_buf, y_buf, gated_buf, carry):
    t = pl.program_id(1)
    ts = SEQ_TILE

    @pl.when(t == 0)
    def _():
        carry[...] = jnp.zeros_like(carry)

    _norm_modulate(x_ref, g_ref, shift_ref, scale_ref, h_buf, ts)
    h = h_buf[...]

    for g, w in enumerate(POOL_WINDOWS):
        cols = slice(g * GROUP_WIDTH, (g + 1) * GROUP_WIDTH)
        zcols = slice(A_WIDTH + g * GROUP_WIDTH, A_WIDTH + (g + 1) * GROUP_WIDTH)
        u_ext[g, 0:MAX_HALO, :] = carry[g]
        u_ext[g, MAX_HALO:, :] = jnp.dot(h, win_ref[:, cols], preferred_element_type=F32)
        carry[g] = u_ext[g, ts:ts + MAX_HALO, :]
        z_buf[g] = jnp.dot(h, win_ref[:, zcols], preferred_element_type=F32)

        halo = MAX_HALO if w > SUBLANES else SUBLANES
        for r in range(0, ts, POOL_ROWS):
            ev = u_ext[g, MAX_HALO + r - halo:MAX_HALO + r + POOL_ROWS, :]
            s = ev
            k = 1
            while k < w:
                s = s + pltpu.roll(s, k, 0)
                k *= 2
            s = s[halo:]
            u = ev[halo:]
            if r == 0:
                pos = t * ts + lax.broadcasted_iota(jnp.int32, (POOL_ROWS, GROUP_WIDTH), 0)
                mean = s * (1.0 / jnp.minimum(pos + 1, w).astype(F32))
            else:
                mean = s * (1.0 / w)
            p_buf[g, r:r + POOL_ROWS, :] = (mean - u).astype(BF16)

        y_buf[g] = jnp.dot(p_buf[g], wg_ref[g], preferred_element_type=F32)
        a_scale = asc_ref[:, cols]
        for r in range(0, ts, NORM_ROWS):
            z = z_buf[g, r:r + NORM_ROWS, :]
            y = y_buf[g, r:r + NORM_ROWS, :] * a_scale
            gated_buf[r:r + NORM_ROWS, cols] = (y * (z * _sigmoid(z))).astype(BF16)

    y = jnp.dot(gated_buf[...], wout_ref[...], preferred_element_type=F32)
    o_ref[0] = x_ref[0] + gate_ref[0] * y


def _pool_layer(x, shift, scale, gate, norm_g, w_in, w_group, a_scale, w_out):
    b, s, d = x.shape
    ts = SEQ_TILE
    vec_spec = pl.BlockSpec((1, 1, d), lambda i, j: (i, 0, 0))
    return pl.pallas_call(
        _pool_kernel,
        grid=(b, s // ts),
        in_specs=[
            pl.BlockSpec((1, ts, d), lambda i, j: (i, j, 0)),
            vec_spec, vec_spec, vec_spec,
            _const_spec((1, d)),
            _const_spec((d, 2 * A_WIDTH)),
            _const_spec((N_GROUPS, GROUP_WIDTH, GROUP_WIDTH)),
            _const_spec((1, A_WIDTH)),
            _const_spec((A_WIDTH, d)),
        ],
        out_specs=pl.BlockSpec((1, ts, d), lambda i, j: (i, j, 0)),
        out_shape=jax.ShapeDtypeStruct((b, s, d), F32),
        scratch_shapes=[
            pltpu.VMEM((ts, d), BF16),
            pltpu.VMEM((N_GROUPS, ts + MAX_HALO, GROUP_WIDTH), F32),
            pltpu.VMEM((N_GROUPS, ts, GROUP_WIDTH), F32),
            pltpu.VMEM((N_GROUPS, ts, GROUP_WIDTH), BF16),
            pltpu.VMEM((N_GROUPS, ts, GROUP_WIDTH), F32),
            pltpu.VMEM((ts, A_WIDTH), BF16),
            pltpu.VMEM((N_GROUPS, MAX_HALO, GROUP_WIDTH), F32),
        ],
        compiler_params=pltpu.CompilerParams(
            dimension_semantics=("arbitrary", "arbitrary"), vmem_limit_bytes=VMEM_LIMIT_BYTES),
        name="pool_layer",
    )(x, shift, scale, gate, norm_g, w_in, w_group, a_scale, w_out)


def _kv_kernel(x_ref, shift_ref, scale_ref, g_ref, wk_ref, wvt_ref, k_ref, vt_ref, h_buf):
    _norm_modulate(x_ref, g_ref, shift_ref, scale_ref, h_buf, KV_TILE)
    h = h_buf[...]
    k_ref[0] = jnp.dot(h, wk_ref[...], preferred_element_type=F32).astype(BF16)
    vt_ref[0] = lax.dot_general(wvt_ref[...], h, _NT_DIMS,
                                preferred_element_type=F32).astype(BF16)


def _kv_layer(x, shift, scale, norm_g, w_k, w_v_t):
    b, s, d = x.shape
    ts = KV_TILE
    vec_spec = pl.BlockSpec((1, 1, d), lambda i, j: (i, 0, 0))
    return pl.pallas_call(
        _kv_kernel,
        grid=(b, s // ts),
        in_specs=[
            pl.BlockSpec((1, ts, d), lambda i, j: (i, j, 0)),
            vec_spec, vec_spec,
            _const_spec((1, d)),
            _const_spec((d, KV_WIDTH)),
            _const_spec((KV_WIDTH, d)),
        ],
        out_specs=[
            pl.BlockSpec((1, ts, KV_WIDTH), lambda i, j: (i, j, 0)),
            pl.BlockSpec((1, KV_WIDTH, ts), lambda i, j: (i, 0, j)),
        ],
        out_shape=[
            jax.ShapeDtypeStruct((b, s, KV_WIDTH), BF16),
            jax.ShapeDtypeStruct((b, KV_WIDTH, s), BF16),
        ],
        scratch_shapes=[pltpu.VMEM((ts, d), BF16)],
        compiler_params=pltpu.CompilerParams(
            dimension_semantics=("arbitrary", "arbitrary"), vmem_limit_bytes=VMEM_LIMIT_BYTES),
        name="shared_kv",
    )(x, shift, scale, norm_g, w_k, w_v_t)


def _t5_causal_buckets():
    i = np.arange(BLOCK)[:, None]
    j = np.arange(2 * BLOCK)[None, :]
    n = np.maximum(i + BLOCK - j, 0)
    max_exact = N_BUCKETS // 2
    large = max_exact + (np.log(np.maximum(n, 1) / max_exact) / math.log(MAX_DISTANCE / max_exact)
                         * (N_BUCKETS - max_exact)).astype(np.int32)
    large = np.minimum(large, N_BUCKETS - 1)
    return np.where(n < max_exact, n, large).astype(np.int32)


def _band_codes_t():
    i = np.arange(BLOCK)[None, :]
    j = np.arange(2 * BLOCK)[:, None]
    rel = i + BLOCK - j
    band = (rel >= 0) & (rel < BLOCK)
    return np.where(band, np.where(j < BLOCK, 1.0, 0.0), 2.0).astype(np.float32)


def _bias_kernel(bucket_ref, rel_ref, o_ref):
    bucket = bucket_ref[...]
    for head in range(N_HEADS):
        acc = jnp.zeros((2 * BLOCK, BLOCK), F32)
        for b in range(N_BUCKETS):
            acc = jnp.where(bucket == b, rel_ref[b, head], acc)
        kv_head, g = divmod(head, GQA_GROUPS)
        o_ref[kv_head, g] = acc * LOG2E


def _position_bias_t(rel_bias):
    buckets_t = jnp.asarray(np.ascontiguousarray(_t5_causal_buckets().T))
    return pl.pallas_call(
        _bias_kernel,
        in_specs=[
            pl.BlockSpec(memory_space=pltpu.VMEM),
            pl.BlockSpec(memory_space=pltpu.SMEM),
        ],
        out_specs=pl.BlockSpec(memory_space=pltpu.VMEM),
        out_shape=jax.ShapeDtypeStruct((N_KV_HEADS, GQA_GROUPS, 2 * BLOCK, BLOCK), F32),
        name="position_bias",
    )(buckets_t, rel_bias)


def _attn_kernel(x_ref, shift_ref, scale_ref, gate_ref, g_ref, wqt_ref, wz_ref, kp_ref, kc_ref,
                 vtp_ref, vtc_ref, bias_ref, code_ref, sink_ref, wout_ref, fg_ref, o_ref,
                 h_buf, qt_buf, z_buf, og_buf, s_buf, p_buf, r_buf, y_buf, *, final_norm):
    t = pl.program_id(1)
    n_chunks = ATTN_TILE // ATTN_CHUNK
    pieces = B_WIDTH // PROJ_PIECE
    steps_per_chunk = (ATTN_CHUNK // BLOCK) * N_KV_HEADS
    n_steps = n_chunks * steps_per_chunk
    norm_rows = ATTN_CHUNK // steps_per_chunk

    def norm_slice(c, j):
        _norm_modulate(x_ref, g_ref, shift_ref, scale_ref, h_buf, norm_rows,
                       start=c * ATTN_CHUNK + j * norm_rows)

    def project_piece(c, n):
        chunk = slice(c * ATTN_CHUNK, (c + 1) * ATTN_CHUNK)
        h = h_buf[chunk, :]
        if n < pieces:
            sl = slice(n * PROJ_PIECE, (n + 1) * PROJ_PIECE)
            qt = lax.dot_general(wqt_ref[sl, :], h, _NT_DIMS, preferred_element_type=F32)
            qt = (qt * (HEAD_DIM ** -0.5 * LOG2E)).astype(BF16)
            for blk in range(ATTN_CHUNK // BLOCK):
                qt_buf[c * (ATTN_CHUNK // BLOCK) + blk, sl, :] = qt[:, blk * BLOCK:(blk + 1) * BLOCK]
        else:
            sl = slice((n - pieces) * PROJ_PIECE, (n - pieces + 1) * PROJ_PIECE)
            z = jnp.dot(h, wz_ref[:, sl], preferred_element_type=F32)
            for sub in range(PROJ_PIECE // LANES):
                z_buf[(n - pieces) * (PROJ_PIECE // LANES) + sub, chunk, :] = z[:, sub * LANES:(sub + 1) * LANES]

    def out_piece(c, n):
        chunk = slice(c * ATTN_CHUNK, (c + 1) * ATTN_CHUNK)
        sl = slice(n * PROJ_PIECE, (n + 1) * PROJ_PIECE)
        og = jnp.concatenate([og_buf[sb, chunk, :] for sb in range(B_WIDTH // LANES)], axis=1)
        y_buf[c % 2, n] = jnp.dot(og, wout_ref[:, sl], preferred_element_type=F32)
        if n == pieces - 1:
            for r in range(0, ATTN_CHUNK, NORM_ROWS):
                rows = slice(c * ATTN_CHUNK + r, c * ATTN_CHUNK + r + NORM_ROWS)
                y = jnp.concatenate([y_buf[c % 2, m, r:r + NORM_ROWS, :] for m in range(pieces)], axis=1)
                out = x_ref[0, rows, :] + gate_ref[0] * y
                if final_norm:
                    ms = jnp.mean(out * out, axis=-1, keepdims=True)
                    out = (out * lax.rsqrt(ms + EPS)) * fg_ref[...]
                o_ref[0, rows, :] = out

    def stage_scores(step):
        i, kvh = divmod(step, N_KV_HEADS)
        _scores(i, kvh, qt_buf, kp_ref, kc_ref, s_buf.at[step % 2])

    def stage_softmax(step):
        i, kvh = divmod(step, N_KV_HEADS)
        _softmax(i, kvh, t, s_buf.at[step % 2], code_ref, bias_ref, sink_ref,
                 p_buf.at[step % 2], r_buf.at[step % 2])

    def stage_pv(step):
        i, kvh = divmod(step, N_KV_HEADS)
        _pv_gate(i, kvh, p_buf.at[step % 2], r_buf.at[step % 2], vtp_ref, vtc_ref, z_buf, og_buf)

    norm_slices = [(c, j) for c in range(n_chunks) for j in range(steps_per_chunk)]
    for c, j in norm_slices[:2 * steps_per_chunk]:
        norm_slice(c, j)
    for n in range(2 * pieces):
        project_piece(0, n)

    def out_slices(ko):
        co, no = divmod(ko, steps_per_chunk)
        for n in range(no * pieces // steps_per_chunk, (no + 1) * pieces // steps_per_chunk):
            out_piece(co, n)

    out_lag = steps_per_chunk + 2
    for k in range(n_steps + 2):
        c, j = divmod(k, steps_per_chunk)
        if k < n_steps:
            stage_scores(k)
        if 1 <= k <= n_steps:
            stage_softmax(k - 1)
        if 2 <= k <= n_steps + 1:
            stage_pv(k - 2)
        if c + 2 < n_chunks:
            norm_slice(c + 2, j)
        if c + 1 < n_chunks:
            for n in range(j * 2 * pieces // steps_per_chunk, (j + 1) * 2 * pieces // steps_per_chunk):
                project_piece(c + 1, n)
        if k >= out_lag:
            out_slices(k - out_lag)
    for ko in range(n_steps + 2 - out_lag, n_steps):
        out_slices(ko)


def _scores(i, kvh, qt_buf, kp_ref, kc_ref, s_ref):
    rows = slice(i * BLOCK, (i + 1) * BLOCK)
    k_prev = kp_ref[0] if i == 0 else kc_ref[0, (i - 1) * BLOCK:i * BLOCK, :]
    k2 = jnp.concatenate([k_prev, kc_ref[0, rows, :]], axis=0)
    q_cat = jnp.concatenate(
        [qt_buf[i, (kvh * GQA_GROUPS + g) * HEAD_DIM:(kvh * GQA_GROUPS + g + 1) * HEAD_DIM, :]
         for g in range(GQA_GROUPS)], axis=1)
    q_zeros = jnp.zeros((HEAD_DIM, GQA_GROUPS * BLOCK), BF16)
    q_sel = jnp.concatenate([q_cat, q_zeros] if kvh == 0 else [q_zeros, q_cat], axis=0)
    s = jnp.dot(k2, q_sel, preferred_element_type=F32)
    for g in range(GQA_GROUPS):
        s_ref[g] = s[:, g * BLOCK:(g + 1) * BLOCK]


def _softmax(i, kvh, t, s_ref, code_ref, bias_ref, sink_ref, p_ref, r_ref):
    limit = jnp.where(t == 0, 1.0, 2.0) if i == 0 else 2.0
    for g in range(GQA_GROUPS):
        sink = sink_ref[kvh, :, g * BLOCK:(g + 1) * BLOCK]
        sg = jnp.where(code_ref[...] < limit, s_ref[g] + bias_ref[kvh, g], NEG_INF)
        m = jnp.maximum(jnp.max(sg, axis=0, keepdims=True), sink)
        e = jnp.exp2(sg - m)
        den = jnp.sum(e, axis=0, keepdims=True) + jnp.exp2(sink - m)
        p_ref[g] = e.astype(BF16)
        r_ref[g] = 1.0 / den


def _pv_gate(i, kvh, p_ref, r_ref, vtp_ref, vtc_ref, z_buf, og_buf):
    rows = slice(i * BLOCK, (i + 1) * BLOCK)
    vt_prev = vtp_ref[0] if i == 0 else vtc_ref[0, :, (i - 1) * BLOCK:i * BLOCK]
    vt2 = jnp.concatenate([vt_prev, vtc_ref[0, :, rows]], axis=1)
    vt = vt2[kvh * HEAD_DIM:(kvh + 1) * HEAD_DIM, :]
    pt = jnp.concatenate([p_ref[g] for g in range(GQA_GROUPS)], axis=1)
    ot = jnp.dot(vt, pt, preferred_element_type=F32)
    for p in range(GQA_GROUPS // 2):
        pair_t = jnp.concatenate(
            [ot[:, g * BLOCK:(g + 1) * BLOCK] * r_ref[g] for g in (2 * p, 2 * p + 1)], axis=0)
        slab = kvh * (GQA_GROUPS // 2) + p
        z = z_buf[slab, rows, :]
        og_buf[slab, rows, :] = (pair_t.T * (z * _sigmoid(z))).astype(BF16)


def _attn_layer(x, shift, scale, gate, norm_g, w_q_t, w_z, k, v_t, bias, codes, sinks, w_out, final_g,
                final_norm):
    b, s, d = x.shape
    ts = ATTN_TILE
    blocks_per_tile = ts // BLOCK
    vec_spec = pl.BlockSpec((1, 1, d), lambda i, j: (i, 0, 0))
    score_shape = (2, GQA_GROUPS, 2 * BLOCK, BLOCK)

    def prev_block(j):
        return jnp.maximum(j * blocks_per_tile - 1, 0)

    return pl.pallas_call(
        functools.partial(_attn_kernel, final_norm=final_norm),
        grid=(b, s // ts),
        in_specs=[
            pl.BlockSpec((1, ts, d), lambda i, j: (i, j, 0)),
            vec_spec, vec_spec, vec_spec,
            _const_spec((1, d)),
            _const_spec((B_WIDTH, d)),
            _const_spec((d, B_WIDTH)),
            pl.BlockSpec((1, BLOCK, KV_WIDTH), lambda i, j: (i, prev_block(j), 0)),
            pl.BlockSpec((1, ts, KV_WIDTH), lambda i, j: (i, j, 0)),
            pl.BlockSpec((1, KV_WIDTH, BLOCK), lambda i, j: (i, 0, prev_block(j))),
            pl.BlockSpec((1, KV_WIDTH, ts), lambda i, j: (i, 0, j)),
            _const_spec((N_KV_HEADS, GQA_GROUPS, 2 * BLOCK, BLOCK)),
            _const_spec((2 * BLOCK, BLOCK)),
            _const_spec((N_KV_HEADS, 1, GQA_GROUPS * BLOCK)),
            _const_spec((B_WIDTH, d)),
            _const_spec((1, d)),
        ],
        out_specs=pl.BlockSpec((1, ts, d), lambda i, j: (i, j, 0)),
        out_shape=jax.ShapeDtypeStruct((b, s, d), F32),
        scratch_shapes=[
            pltpu.VMEM((ts, d), BF16),
            pltpu.VMEM((blocks_per_tile, B_WIDTH, BLOCK), BF16),
            pltpu.VMEM((B_WIDTH // LANES, ts, LANES), F32),
            pltpu.VMEM((B_WIDTH // LANES, ts, LANES), BF16),
            pltpu.VMEM(score_shape, F32),
            pltpu.VMEM(score_shape, BF16),
            pltpu.VMEM((2, GQA_GROUPS, 1, BLOCK), F32),
            pltpu.VMEM((2, d // PROJ_PIECE, ATTN_CHUNK, PROJ_PIECE), F32),
        ],
        compiler_params=pltpu.CompilerParams(
            dimension_semantics=("arbitrary", "arbitrary"), vmem_limit_bytes=VMEM_LIMIT_BYTES),
        name="attn_layer",
    )(x, shift, scale, gate, norm_g, w_q_t, w_z, k, k, v_t, v_t, bias, codes, sinks, w_out, final_g)


def kernel(x, c, norm_g, ada_w, ada_b, a_w_in, a_w_group, a_scale, a_w_out, kv_norm_g, kv_ada_w,
           kv_ada_b, w_kv, b_w_in, b_sinks, b_w_out, rel_bias, final_g):
    b, s, d = x.shape
    depth = norm_g.shape[0]
    n_a = a_w_in.shape[0]
    assert (d, s % SEQ_TILE, s % ATTN_TILE, s % KV_TILE) == (D_MODEL, 0, 0, 0) and b <= SUBLANES

    c_pad = jnp.pad(c, ((0, SUBLANES - b), (0, 0)))
    mod = _modulation(c_pad, ada_w, ada_b.reshape(depth, 1, 3 * d))[:, :b]
    mod_kv = _modulation(c_pad, kv_ada_w[None], kv_ada_b.reshape(1, 1, 2 * d))[0, :b]

    def vec(m, k):
        return m[:, k * d:(k + 1) * d].reshape(b, 1, d)

    for l in range(n_a):
        x = _pool_layer(x, vec(mod[l], 0), vec(mod[l], 1), vec(mod[l], 2), norm_g[l].reshape(1, d),
                        a_w_in[l].astype(BF16), a_w_group[l].astype(BF16),
                        a_scale[l].reshape(1, A_WIDTH), a_w_out[l].astype(BF16))

    k, v_t = _kv_layer(x, vec(mod_kv, 0), vec(mod_kv, 1), kv_norm_g.reshape(1, d),
                       w_kv[:, :KV_WIDTH].astype(BF16), w_kv[:, KV_WIDTH:].T.astype(BF16))
    bias = _position_bias_t(rel_bias)
    codes = jnp.asarray(_band_codes_t())
    for l in range(n_a, depth):
        j = l - n_a
        sinks = jnp.repeat(b_sinks[j] * LOG2E, BLOCK).reshape(N_KV_HEADS, 1, GQA_GROUPS * BLOCK)
        x = _attn_layer(x, vec(mod[l], 0), vec(mod[l], 1), vec(mod[l], 2), norm_g[l].reshape(1, d),
                        b_w_in[j, :, :B_WIDTH].T.astype(BF16), b_w_in[j, :, B_WIDTH:].astype(BF16),
                        k, v_t, bias, codes, sinks, b_w_out[j].astype(BF16),
                        final_g.reshape(1, d), final_norm=(l == depth - 1))
    return x
```

```python
import functools
import math

import numpy as np
import jax
import jax.numpy as jnp
from jax import lax
from jax.experimental import pallas as pl
from jax.experimental.pallas import tpu as pltpu

F32 = jnp.float32
BF16 = jnp.bfloat16

D_MODEL = 1024
A_WIDTH = 2048
POOL_WINDOWS = (2, 4, 8, 16)
N_GROUPS = len(POOL_WINDOWS)
GROUP_WIDTH = A_WIDTH // N_GROUPS
HEAD_DIM = 64
N_HEADS = 16
N_KV_HEADS = 2
GQA_GROUPS = N_HEADS // N_KV_HEADS
KV_WIDTH = N_KV_HEADS * HEAD_DIM
B_WIDTH = N_HEADS * HEAD_DIM
BLOCK = 128
N_BUCKETS = 32
MAX_DISTANCE = 128
EPS = 1e-6
NEG_INF = -1e30
LOG2E = math.log2(math.e)

LANES = 128
SUBLANES = 8
VMEM_LIMIT_BYTES = 56 * 1024 * 1024

MAX_HALO = 16
SEQ_TILE = 512
ATTN_TILE = 1024
ATTN_CHUNK = 256
PROJ_PIECE = 256
KV_TILE = 1024
NORM_ROWS = 32
POOL_ROWS = 64

_NT_DIMS = (((1,), (1,)), ((), ()))


def _sigmoid(v):
    return 1.0 / (1.0 + jnp.exp(-v))


def _const_spec(shape):
    zeros = (0,) * len(shape)
    return pl.BlockSpec(shape, lambda *_: zeros, pipeline_mode=pl.Buffered(1))


def _mod_kernel(c_ref, w_ref, b_ref, o_ref):
    c = c_ref[...]
    c_act = c * _sigmoid(c)
    o_ref[0] = jnp.dot(c_act, w_ref[0], precision=lax.Precision.HIGHEST,
                       preferred_element_type=F32) + b_ref[0]


def _modulation(c_pad, w, b):
    n_layers, d, n = w.shape
    rows = c_pad.shape[0]
    nt = D_MODEL
    return pl.pallas_call(
        _mod_kernel,
        grid=(n_layers, n // nt),
        in_specs=[
            pl.BlockSpec((rows, d), lambda l, j: (0, 0)),
            pl.BlockSpec((1, d, nt), lambda l, j: (l, 0, j)),
            pl.BlockSpec((1, 1, nt), lambda l, j: (l, 0, j)),
        ],
        out_specs=pl.BlockSpec((1, rows, nt), lambda l, j: (l, 0, j)),
        out_shape=jax.ShapeDtypeStruct((n_layers, rows, n), F32),
        compiler_params=pltpu.CompilerParams(
            dimension_semantics=("arbitrary", "arbitrary"), vmem_limit_bytes=VMEM_LIMIT_BYTES),
        name="adaln_modulation",
    )(c_pad, w, b)


def _zero_after(v):
    u = pltpu.bitcast(v, jnp.uint32)
    return pltpu.bitcast((u >> 16) >> 16, F32)


def _norm_modulate(x_ref, g_ref, shift_ref, scale_ref, h_ref, rows, start=0, after=None):
    g = g_ref[...]
    scale1 = 1.0 + scale_ref[0]
    shift = shift_ref[0]
    if after is not None:
        shift = shift + _zero_after(after)
    for r in range(start, start + rows, NORM_ROWS):
        xv = x_ref[0, r:r + NORM_ROWS, :]
        ms = jnp.mean(xv * xv, axis=-1, keepdims=True)
        xn = xv * lax.rsqrt(ms + EPS)
        h_ref[r:r + NORM_ROWS, :] = ((xn * g) * scale1 + shift).astype(BF16)


def _split_bf16(a):
    hi = a.astype(BF16)
    lo = (a - hi.astype(F32)).astype(BF16)
    return hi, lo


def _fuse_kernel(win_ref, wg_ref, o_ref):
    a_hi, a_lo = _split_bf16(win_ref[0])
    b_hi, b_lo = _split_bf16(wg_ref[0, 0])
    acc = jnp.dot(a_hi, b_hi, preferred_element_type=F32)
    acc = acc + jnp.dot(a_hi, b_lo, preferred_element_type=F32)
    acc = acc + jnp.dot(a_lo, b_hi, preferred_element_type=F32)
    o_ref[0] = acc.astype(BF16)


def _fuse_group_weights(a_w_in, a_w_group):
    n_a, d, _ = a_w_in.shape
    return pl.pallas_call(
        _fuse_kernel,
        grid=(n_a, N_GROUPS),
        in_specs=[
            pl.BlockSpec((1, d, GROUP_WIDTH), lambda l, g: (l, 0, g)),
            pl.BlockSpec((1, 1, GROUP_WIDTH, GROUP_WIDTH), lambda l, g: (l, g, 0, 0)),
        ],
        out_specs=pl.BlockSpec((1, d, GROUP_WIDTH), lambda l, g: (l, 0, g)),
        out_shape=jax.ShapeDtypeStruct((n_a, d, A_WIDTH), BF16),
        compiler_params=pltpu.CompilerParams(
            dimension_semantics=("arbitrary", "arbitrary"), vmem_limit_bytes=VMEM_LIMIT_BYTES),
        name="fuse_group_weights",
    )(a_w_in, a_w_group)


def _pool_kernel(x_ref, shift_ref, scale_ref, gate_ref, g_ref, wv_ref, wz_ref, asc_ref, wout_ref,
                 o_ref, h_buf, v_ext, z_buf, gated_buf, carry):
    t = pl.program_id(1)
    ts = SEQ_TILE

    @pl.when(t == 0)
    def _():
        carry[...] = jnp.zeros_like(carry)

    _norm_modulate(x_ref, g_ref, shift_ref, scale_ref, h_buf, ts)
    h = h_buf[...]

    for g, w in enumerate(POOL_WINDOWS):
        cols = slice(g * GROUP_WIDTH, (g + 1) * GROUP_WIDTH)
        v_ext[g, 0:MAX_HALO, :] = carry[g]
        v_ext[g, MAX_HALO:, :] = jnp.dot(h, wv_ref[:, cols], preferred_element_type=F32)
        carry[g] = v_ext[g, ts:ts + MAX_HALO, :]
        z_buf[g] = jnp.dot(h, wz_ref[:, cols], preferred_element_type=F32)

        halo = MAX_HALO if w > SUBLANES else SUBLANES
        a_scale = asc_ref[:, cols]
        for r in range(0, ts, POOL_ROWS):
            ev = v_ext[g, MAX_HALO + r - halo:MAX_HALO + r + POOL_ROWS, :]
            s = ev
            k = 1
            while k < w:
                s = s + pltpu.roll(s, k, 0)
                k *= 2
            s = s[halo:]
            if r == 0:
                pos = t * ts + lax.broadcasted_iota(jnp.int32, (POOL_ROWS, GROUP_WIDTH), 0)
                mean = s * (1.0 / jnp.minimum(pos + 1, w).astype(F32))
            else:
                mean = s * (1.0 / w)
            z = z_buf[g, r:r + POOL_ROWS, :]
            y = (mean - ev[halo:]) * a_scale
            gated_buf[r:r + POOL_ROWS, cols] = (y * (z * _sigmoid(z))).astype(BF16)

    y = jnp.dot(gated_buf[...], wout_ref[...], preferred_element_type=F32)
    o_ref[0] = x_ref[0] + gate_ref[0] * y


def _pool_layer(x, shift, scale, gate, norm_g, w_v, w_z, a_scale, w_out):
    b, s, d = x.shape
    ts = SEQ_TILE
    vec_spec = pl.BlockSpec((1, 1, d), lambda i, j: (i, 0, 0))
    return pl.pallas_call(
        _pool_kernel,
        grid=(b, s // ts),
        in_specs=[
            pl.BlockSpec((1, ts, d), lambda i, j: (i, j, 0)),
            vec_spec, vec_spec, vec_spec,
            _const_spec((1, d)),
            _const_spec((d, A_WIDTH)),
            _const_spec((d, A_WIDTH)),
            _const_spec((1, A_WIDTH)),
            _const_spec((A_WIDTH, d)),
        ],
        out_specs=pl.BlockSpec((1, ts, d), lambda i, j: (i, j, 0)),
        out_shape=jax.ShapeDtypeStruct((b, s, d), F32),
        scratch_shapes=[
            pltpu.VMEM((ts, d), BF16),
            pltpu.VMEM((N_GROUPS, ts + MAX_HALO, GROUP_WIDTH), F32),
            pltpu.VMEM((N_GROUPS, ts, GROUP_WIDTH), F32),
            pltpu.VMEM((ts, A_WIDTH), BF16),
            pltpu.VMEM((N_GROUPS, MAX_HALO, GROUP_WIDTH), F32),
        ],
        compiler_params=pltpu.CompilerParams(
            dimension_semantics=("arbitrary", "arbitrary"), vmem_limit_bytes=VMEM_LIMIT_BYTES),
        name="pool_layer",
    )(x, shift, scale, gate, norm_g, w_v, w_z, a_scale, w_out)


def _kv_kernel(x_ref, shift_ref, scale_ref, g_ref, wk_ref, wvt_ref, k_ref, vt_ref, h_buf):
    _norm_modulate(x_ref, g_ref, shift_ref, scale_ref, h_buf, KV_TILE)
    h = h_buf[...]
    k_ref[0] = jnp.dot(h, wk_ref[...], preferred_element_type=F32).astype(BF16)
    vt_ref[0] = lax.dot_general(wvt_ref[...], h, _NT_DIMS,
                                preferred_element_type=F32).astype(BF16)


def _kv_layer(x, shift, scale, norm_g, w_k, w_v_t):
    b, s, d = x.shape
    ts = KV_TILE
    vec_spec = pl.BlockSpec((1, 1, d), lambda i, j: (i, 0, 0))
    return pl.pallas_call(
        _kv_kernel,
        grid=(b, s // ts),
        in_specs=[
            pl.BlockSpec((1, ts, d), lambda i, j: (i, j, 0)),
            vec_spec, vec_spec,
            _const_spec((1, d)),
            _const_spec((d, KV_WIDTH)),
            _const_spec((KV_WIDTH, d)),
        ],
        out_specs=[
            pl.BlockSpec((1, ts, KV_WIDTH), lambda i, j: (i, j, 0)),
            pl.BlockSpec((1, KV_WIDTH, ts), lambda i, j: (i, 0, j)),
        ],
        out_shape=[
            jax.ShapeDtypeStruct((b, s, KV_WIDTH), BF16),
            jax.ShapeDtypeStruct((b, KV_WIDTH, s), BF16),
        ],
        scratch_shapes=[pltpu.VMEM((ts, d), BF16)],
        compiler_params=pltpu.CompilerParams(
            dimension_semantics=("arbitrary", "arbitrary"), vmem_limit_bytes=VMEM_LIMIT_BYTES),
        name="shared_kv",
    )(x, shift, scale, norm_g, w_k, w_v_t)


def _t5_causal_buckets():
    i = np.arange(BLOCK)[:, None]
    j = np.arange(2 * BLOCK)[None, :]
    n = np.maximum(i + BLOCK - j, 0)
    max_exact = N_BUCKETS // 2
    large = max_exact + (np.log(np.maximum(n, 1) / max_exact) / math.log(MAX_DISTANCE / max_exact)
                         * (N_BUCKETS - max_exact)).astype(np.int32)
    large = np.minimum(large, N_BUCKETS - 1)
    return np.where(n < max_exact, n, large).astype(np.int32)


def _band_codes_t():
    i = np.arange(BLOCK)[None, :]
    j = np.arange(2 * BLOCK)[:, None]
    rel = i + BLOCK - j
    band = (rel >= 0) & (rel < BLOCK)
    return np.where(band, np.where(j < BLOCK, 1.0, 0.0), 2.0).astype(np.float32)


def _bias_kernel(bucket_ref, rel_ref, o_ref):
    bucket = bucket_ref[...]
    for head in range(N_HEADS):
        acc = jnp.zeros((2 * BLOCK, BLOCK), F32)
        for b in range(N_BUCKETS):
            acc = jnp.where(bucket == b, rel_ref[b, head], acc)
        kv_head, g = divmod(head, GQA_GROUPS)
        o_ref[kv_head, g] = acc * LOG2E


def _position_bias_t(rel_bias):
    buckets_t = jnp.asarray(np.ascontiguousarray(_t5_causal_buckets().T))
    return pl.pallas_call(
        _bias_kernel,
        in_specs=[
            pl.BlockSpec(memory_space=pltpu.VMEM),
            pl.BlockSpec(memory_space=pltpu.SMEM),
        ],
        out_specs=pl.BlockSpec(memory_space=pltpu.VMEM),
        out_shape=jax.ShapeDtypeStruct((N_KV_HEADS, GQA_GROUPS, 2 * BLOCK, BLOCK), F32),
        name="position_bias",
    )(buckets_t, rel_bias)


def _attn_kernel(x_ref, shift_ref, scale_ref, gate_ref, g_ref, wqt_ref, wz_ref, kp_ref, kc_ref,
                 vtp_ref, vtc_ref, bias_ref, code_ref, sink_ref, wout_ref, fg_ref, o_ref,
                 h_buf, qt_buf, z_buf, og_buf, s_buf, p_buf, r_buf, y_buf, *, final_norm):
    t = pl.program_id(1)
    n_chunks = ATTN_TILE // ATTN_CHUNK
    pieces = B_WIDTH // PROJ_PIECE
    steps_per_chunk = (ATTN_CHUNK // BLOCK) * N_KV_HEADS
    n_steps = n_chunks * steps_per_chunk
    norm_rows = ATTN_CHUNK // steps_per_chunk

    def norm_slice(c, j):
        _norm_modulate(x_ref, g_ref, shift_ref, scale_ref, h_buf, norm_rows,
                       start=c * ATTN_CHUNK + j * norm_rows)

    def project_piece(c, n):
        chunk = slice(c * ATTN_CHUNK, (c + 1) * ATTN_CHUNK)
        h = h_buf[chunk, :]
        if n < pieces:
            sl = slice(n * PROJ_PIECE, (n + 1) * PROJ_PIECE)
            qt = lax.dot_general(wqt_ref[sl, :], h, _NT_DIMS, preferred_element_type=F32)
            qt = (qt * (HEAD_DIM ** -0.5 * LOG2E)).astype(BF16)
            for blk in range(ATTN_CHUNK // BLOCK):
                qt_buf[c * (ATTN_CHUNK // BLOCK) + blk, sl, :] = qt[:, blk * BLOCK:(blk + 1) * BLOCK]
        else:
            sl = slice((n - pieces) * PROJ_PIECE, (n - pieces + 1) * PROJ_PIECE)
            z = jnp.dot(h, wz_ref[:, sl], preferred_element_type=F32)
            for sub in range(PROJ_PIECE // LANES):
                z_buf[(n - pieces) * (PROJ_PIECE // LANES) + sub, chunk, :] = z[:, sub * LANES:(sub + 1) * LANES]

    def out_piece(c, n):
        chunk = slice(c * ATTN_CHUNK, (c + 1) * ATTN_CHUNK)
        sl = slice(n * PROJ_PIECE, (n + 1) * PROJ_PIECE)
        og = jnp.concatenate([og_buf[sb, chunk, :] for sb in range(B_WIDTH // LANES)], axis=1)
        y_buf[c % 2, n] = jnp.dot(og, wout_ref[:, sl], preferred_element_type=F32)
        if n == pieces - 1:
            for r in range(0, ATTN_CHUNK, NORM_ROWS):
                rows = slice(c * ATTN_CHUNK + r, c * ATTN_CHUNK + r + NORM_ROWS)
                y = jnp.concatenate([y_buf[c % 2, m, r:r + NORM_ROWS, :] for m in range(pieces)], axis=1)
                out = x_ref[0, rows, :] + gate_ref[0] * y
                if final_norm:
                    ms = jnp.mean(out * out, axis=-1, keepdims=True)
                    out = (out * lax.rsqrt(ms + EPS)) * fg_ref[...]
                o_ref[0, rows, :] = out

    def stage_scores(step):
        i, kvh = divmod(step, N_KV_HEADS)
        _scores(i, kvh, qt_buf, kp_ref, kc_ref, s_buf.at[step % 2])

    def stage_softmax(step):
        i, kvh = divmod(step, N_KV_HEADS)
        _softmax(i, kvh, t, s_buf.at[step % 2], code_ref, bias_ref, sink_ref,
                 p_buf.at[step % 2], r_buf.at[step % 2])

    def stage_pv(step):
        i, kvh = divmod(step, N_KV_HEADS)
        _pv_gate(i, kvh, p_buf.at[step % 2], r_buf.at[step % 2], vtp_ref, vtc_ref, z_buf, og_buf)

    norm_slices = [(c, j) for c in range(n_chunks) for j in range(steps_per_chunk)]
    for c, j in norm_slices[:2 * steps_per_chunk]:
        norm_slice(c, j)
    for n in range(2 * pieces):
        project_piece(0, n)

    def out_slices(ko):
        co, no = divmod(ko, steps_per_chunk)
        for n in range(no * pieces // steps_per_chunk, (no + 1) * pieces // steps_per_chunk):
            out_piece(co, n)

    out_lag = steps_per_chunk + 2
    for k in range(n_steps + 2):
        c, j = divmod(k, steps_per_chunk)
        if k < n_steps:
            stage_scores(k)
        if 1 <= k <= n_steps:
            stage_softmax(k - 1)
        if 2 <= k <= n_steps + 1:
            stage_pv(k - 2)
        if c + 2 < n_chunks:
            norm_slice(c + 2, j)
        if c + 1 < n_chunks:
            for n in range(j * 2 * pieces // steps_per_chunk, (j + 1) * 2 * pieces // steps_per_chunk):
                project_piece(c + 1, n)
        if k >= out_lag:
            out_slices(k - out_lag)
    for ko in range(n_steps + 2 - out_lag, n_steps):
        out_slices(ko)


def _scores(i, kvh, qt_buf, kp_ref, kc_ref, s_ref):
    rows = slice(i * BLOCK, (i + 1) * BLOCK)
    k_prev = kp_ref[0] if i == 0 else kc_ref[0, (i - 1) * BLOCK:i * BLOCK, :]
    k2 = jnp.concatenate([k_prev, kc_ref[0, rows, :]], axis=0)
    q_cat = jnp.concatenate(
        [qt_buf[i, (kvh * GQA_GROUPS + g) * HEAD_DIM:(kvh * GQA_GROUPS + g + 1) * HEAD_DIM, :]
         for g in range(GQA_GROUPS)], axis=1)
    q_zeros = jnp.zeros((HEAD_DIM, GQA_GROUPS * BLOCK), BF16)
    q_sel = jnp.concatenate([q_cat, q_zeros] if kvh == 0 else [q_zeros, q_cat], axis=0)
    s = jnp.dot(k2, q_sel, preferred_element_type=F32)
    for g in range(GQA_GROUPS):
        s_ref[g] = s[:, g * BLOCK:(g + 1) * BLOCK]


def _softmax(i, kvh, t, s_ref, code_ref, bias_ref, sink_ref, p_ref, r_ref):
    limit = jnp.where(t == 0, 1.0, 2.0) if i == 0 else 2.0
    for g in range(GQA_GROUPS):
        sink = sink_ref[kvh, :, g * BLOCK:(g + 1) * BLOCK]
        sg = jnp.where(code_ref[...] < limit, s_ref[g] + bias_ref[kvh, g], NEG_INF)
        m = jnp.maximum(jnp.max(sg, axis=0, keepdims=True), sink)
        e = jnp.exp2(sg - m)
        den = jnp.sum(e, axis=0, keepdims=True) + jnp.exp2(sink - m)
        p_ref[g] = e.astype(BF16)
        r_ref[g] = 1.0 / den


def _pv_gate(i, kvh, p_ref, r_ref, vtp_ref, vtc_ref, z_buf, og_buf):
    rows = slice(i * BLOCK, (i + 1) * BLOCK)
    vt_prev = vtp_ref[0] if i == 0 else vtc_ref[0, :, (i - 1) * BLOCK:i * BLOCK]
    vt2 = jnp.concatenate([vt_prev, vtc_ref[0, :, rows]], axis=1)
    vt = vt2[kvh * HEAD_DIM:(kvh + 1) * HEAD_DIM, :]
    pt = jnp.concatenate([p_ref[g] for g in range(GQA_GROUPS)], axis=1)
    ot = jnp.dot(vt, pt, preferred_element_type=F32)
    for p in range(GQA_GROUPS // 2):
        pair_t = jnp.concatenate(
            [ot[:, g * BLOCK:(g + 1) * BLOCK] * r_ref[g] for g in (2 * p, 2 * p + 1)], axis=0)
        slab = kvh * (GQA_GROUPS // 2) + p
        z = z_buf[slab, rows, :]
        og_buf[slab, rows, :] = (pair_t.T * (z * _sigmoid(z))).astype(BF16)


def _attn_layer(x, shift, scale, gate, norm_g, w_q_t, w_z, k, v_t, bias, codes, sinks, w_out, final_g,
                final_norm):
    b, s, d = x.shape
    ts = ATTN_TILE
    blocks_per_tile = ts // BLOCK
    vec_spec = pl.BlockSpec((1, 1, d), lambda i, j: (i, 0, 0))
    score_shape = (2, GQA_GROUPS, 2 * BLOCK, BLOCK)

    def prev_block(j):
        return jnp.maximum(j * blocks_per_tile - 1, 0)

    return pl.pallas_call(
        functools.partial(_attn_kernel, final_norm=final_norm),
        grid=(b, s // ts),
        in_specs=[
            pl.BlockSpec((1, ts, d), lambda i, j: (i, j, 0)),
            vec_spec, vec_spec, vec_spec,
            _const_spec((1, d)),
            _const_spec((B_WIDTH, d)),
            _const_spec((d, B_WIDTH)),
            pl.BlockSpec((1, BLOCK, KV_WIDTH), lambda i, j: (i, prev_block(j), 0)),
            pl.BlockSpec((1, ts, KV_WIDTH), lambda i, j: (i, j, 0)),
            pl.BlockSpec((1, KV_WIDTH, BLOCK), lambda i, j: (i, 0, prev_block(j))),
            pl.BlockSpec((1, KV_WIDTH, ts), lambda i, j: (i, 0, j)),
            _const_spec((N_KV_HEADS, GQA_GROUPS, 2 * BLOCK, BLOCK)),
            _const_spec((2 * BLOCK, BLOCK)),
            _const_spec((N_KV_HEADS, 1, GQA_GROUPS * BLOCK)),
            _const_spec((B_WIDTH, d)),
            _const_spec((1, d)),
        ],
        out_specs=pl.BlockSpec((1, ts, d), lambda i, j: (i, j, 0)),
        out_shape=jax.ShapeDtypeStruct((b, s, d), F32),
        scratch_shapes=[
            pltpu.VMEM((ts, d), BF16),
            pltpu.VMEM((blocks_per_tile, B_WIDTH, BLOCK), BF16),
            pltpu.VMEM((B_WIDTH // LANES, ts, LANES), F32),
            pltpu.VMEM((B_WIDTH // LANES, ts, LANES), BF16),
            pltpu.VMEM(score_shape, F32),
            pltpu.VMEM(score_shape, BF16),
            pltpu.VMEM((2, GQA_GROUPS, 1, BLOCK), F32),
            pltpu.VMEM((2, d // PROJ_PIECE, ATTN_CHUNK, PROJ_PIECE), F32),
        ],
        compiler_params=pltpu.CompilerParams(
            dimension_semantics=("arbitrary", "arbitrary"), vmem_limit_bytes=VMEM_LIMIT_BYTES),
        name="attn_layer",
    )(x, shift, scale, gate, norm_g, w_q_t, w_z, k, k, v_t, v_t, bias, codes, sinks, w_out, final_g)


def kernel(x, c, norm_g, ada_w, ada_b, a_w_in, a_w_group, a_scale, a_w_out, kv_norm_g, kv_ada_w,
           kv_ada_b, w_kv, b_w_in, b_sinks, b_w_out, rel_bias, final_g):
    b, s, d = x.shape
    depth = norm_g.shape[0]
    n_a = a_w_in.shape[0]
    assert (d, s % SEQ_TILE, s % ATTN_TILE, s % KV_TILE) == (D_MODEL, 0, 0, 0) and b <= SUBLANES

    c_pad = jnp.pad(c, ((0, SUBLANES - b), (0, 0)))
    mod = _modulation(c_pad, ada_w, ada_b.reshape(depth, 1, 3 * d))[:, :b]
    mod_kv = _modulation(c_pad, kv_ada_w[None], kv_ada_b.reshape(1, 1, 2 * d))[0, :b]

    def vec(m, k):
        return m[:, k * d:(k + 1) * d].reshape(b, 1, d)

    w_v = _fuse_group_weights(a_w_in, a_w_group)
    for l in range(n_a):
        x = _pool_layer(x, vec(mod[l], 0), vec(mod[l], 1), vec(mod[l], 2), norm_g[l].reshape(1, d),
                        w_v[l], a_w_in[l, :, A_WIDTH:].astype(BF16),
                        a_scale[l].reshape(1, A_WIDTH), a_w_out[l].astype(BF16))

    k, v_t = _kv_layer(x, vec(mod_kv, 0), vec(mod_kv, 1), kv_norm_g.reshape(1, d),
                       w_kv[:, :KV_WIDTH].astype(BF16), w_kv[:, KV_WIDTH:].T.astype(BF16))
    bias = _position_bias_t(rel_bias)
    codes = jnp.asarray(_band_codes_t())
    for l in range(n_a, depth):
        j = l - n_a
        sinks = jnp.repeat(b_sinks[j] * LOG2E, BLOCK).reshape(N_KV_HEADS, 1, GQA_GROUPS * BLOCK)
        x = _attn_layer(x, vec(mod[l], 0), vec(mod[l], 1), vec(mod[l], 2), norm_g[l].reshape(1, d),
                        b_w_in[j, :, :B_WIDTH].T.astype(BF16), b_w_in[j, :, B_WIDTH:].astype(BF16),
                        k, v_t, bias, codes, sinks, b_w_out[j].astype(BF16),
                        final_g.reshape(1, d), final_norm=(l == depth - 1))
    return x
```

```python
import functools
import math

import numpy as np
import jax
import jax.numpy as jnp
from jax import lax
from jax.experimental import pallas as pl
from jax.experimental.pallas import tpu as pltpu

F32 = jnp.float32
BF16 = jnp.bfloat16

D_MODEL = 1024
A_WIDTH = 2048
POOL_WINDOWS = (2, 4, 8, 16)
N_GROUPS = len(POOL_WINDOWS)
GROUP_WIDTH = A_WIDTH // N_GROUPS
HEAD_DIM = 64
N_HEADS = 16
N_KV_HEADS = 2
GQA_GROUPS = N_HEADS // N_KV_HEADS
KV_WIDTH = N_KV_HEADS * HEAD_DIM
B_WIDTH = N_HEADS * HEAD_DIM
BLOCK = 128
N_BUCKETS = 32
MAX_DISTANCE = 128
EPS = 1e-6
NEG_INF = -1e30
LOG2E = math.log2(math.e)

LANES = 128
SUBLANES = 8
VMEM_LIMIT_BYTES = 56 * 1024 * 1024

MAX_HALO = 16
SEQ_TILE = 512
ATTN_TILE = 1024
ATTN_CHUNK = 256
PROJ_PIECE = 256
NORM_ROWS = 32
POOL_ROWS = 64

_NT_DIMS = (((1,), (1,)), ((), ()))


def _sigmoid(v):
    return 1.0 / (1.0 + jnp.exp(-v))


def _const_spec(shape):
    zeros = (0,) * len(shape)
    return pl.BlockSpec(shape, lambda *_: zeros, pipeline_mode=pl.Buffered(1))


def _mod_kernel(c_ref, w_ref, b_ref, o_ref):
    c = c_ref[...]
    c_act = c * _sigmoid(c)
    o_ref[0] = jnp.dot(c_act, w_ref[0], preferred_element_type=F32) + b_ref[0]


def _modulation(c_pad, w, b):
    n_layers, d, n = w.shape
    rows = c_pad.shape[0]
    nt = D_MODEL
    return pl.pallas_call(
        _mod_kernel,
        grid=(n_layers, n // nt),
        in_specs=[
            pl.BlockSpec((rows, d), lambda l, j: (0, 0)),
            pl.BlockSpec((1, d, nt), lambda l, j: (l, 0, j)),
            pl.BlockSpec((1, 1, nt), lambda l, j: (l, 0, j)),
        ],
        out_specs=pl.BlockSpec((1, rows, nt), lambda l, j: (l, 0, j)),
        out_shape=jax.ShapeDtypeStruct((n_layers, rows, n), F32),
        compiler_params=pltpu.CompilerParams(
            dimension_semantics=("arbitrary", "arbitrary"), vmem_limit_bytes=VMEM_LIMIT_BYTES),
        name="adaln_modulation",
    )(c_pad, w, b)


def _norm_modulate(x_ref, g_ref, shift_ref, scale_ref, h_ref, rows, start=0):
    gain = g_ref[...] * (1.0 + scale_ref[0])
    shift = shift_ref[0]
    for r in range(start, start + rows, NORM_ROWS):
        xv = x_ref[0, r:r + NORM_ROWS, :]
        ms = jnp.mean(xv * xv, axis=-1, keepdims=True)
        xn = xv * lax.rsqrt(ms + EPS)
        h_ref[r:r + NORM_ROWS, :] = (xn * gain + shift).astype(BF16)


def _split_bf16(a):
    hi = a.astype(BF16)
    lo = (a - hi.astype(F32)).astype(BF16)
    return hi, lo


def _fuse_kernel(win_ref, wg_ref, asc_ref, o_ref):
    a_hi, a_lo = _split_bf16(win_ref[0])
    b_hi, b_lo = _split_bf16(wg_ref[0, 0])
    acc = jnp.dot(a_hi, b_hi, preferred_element_type=F32)
    acc = acc + jnp.dot(a_hi, b_lo, preferred_element_type=F32)
    acc = acc + jnp.dot(a_lo, b_hi, preferred_element_type=F32)
    o_ref[0] = (acc * asc_ref[0]).astype(BF16)


def _fuse_group_weights(a_w_in, a_w_group, a_scale):
    n_a, d, _ = a_w_in.shape
    return pl.pallas_call(
        _fuse_kernel,
        grid=(n_a, N_GROUPS),
        in_specs=[
            pl.BlockSpec((1, d, GROUP_WIDTH), lambda l, g: (l, 0, g)),
            pl.BlockSpec((1, 1, GROUP_WIDTH, GROUP_WIDTH), lambda l, g: (l, g, 0, 0)),
            pl.BlockSpec((1, 1, GROUP_WIDTH), lambda l, g: (l, 0, g)),
        ],
        out_specs=pl.BlockSpec((1, d, GROUP_WIDTH), lambda l, g: (l, 0, g)),
        out_shape=jax.ShapeDtypeStruct((n_a, d, A_WIDTH), BF16),
        compiler_params=pltpu.CompilerParams(
            dimension_semantics=("arbitrary", "arbitrary"), vmem_limit_bytes=VMEM_LIMIT_BYTES),
        name="fuse_group_weights",
    )(a_w_in, a_w_group, a_scale.reshape(n_a, 1, A_WIDTH))


def _pool_kernel(x_ref, shift_ref, scale_ref, gate_ref, g_ref, wv_ref, wz_ref, wout_ref, *rest, emit_kv):
    if emit_kv:
        kshift_ref, kscale_ref, kg_ref, wk_ref, wvt_ref, o_ref, k_ref, vt_ref = rest[:8]
    else:
        o_ref = rest[0]
    h_buf, v_ext, z_buf, gated_buf, carry = rest[-5:]
    t = pl.program_id(1)
    ts = SEQ_TILE

    @pl.when(t == 0)
    def _():
        carry[...] = jnp.zeros_like(carry)

    _norm_modulate(x_ref, g_ref, shift_ref, scale_ref, h_buf, ts)
    h = h_buf[...]

    for g, w in enumerate(POOL_WINDOWS):
        cols = slice(g * GROUP_WIDTH, (g + 1) * GROUP_WIDTH)
        v_ext[g, 0:MAX_HALO, :] = carry[g]
        v_ext[g, MAX_HALO:, :] = jnp.dot(h, wv_ref[:, cols], preferred_element_type=F32)
        carry[g] = v_ext[g, ts:ts + MAX_HALO, :]
        z_buf[g] = jnp.dot(h, wz_ref[:, cols], preferred_element_type=F32)

        halo = MAX_HALO if w > SUBLANES else SUBLANES
        for r in range(0, ts, POOL_ROWS):
            ev = v_ext[g, MAX_HALO + r - halo:MAX_HALO + r + POOL_ROWS, :]
            s = ev
            k = 1
            while k < w:
                s = s + pltpu.roll(s, k, 0)
                k *= 2
            s = s[halo:]
            if r == 0:
                pos = t * ts + lax.broadcasted_iota(jnp.int32, (POOL_ROWS, GROUP_WIDTH), 0)
                mean = s * (1.0 / jnp.minimum(pos + 1, w).astype(F32))
            else:
                mean = s * (1.0 / w)
            z = z_buf[g, r:r + POOL_ROWS, :]
            gated_buf[r:r + POOL_ROWS, cols] = ((mean - ev[halo:]) * (z * _sigmoid(z))).astype(BF16)

    y = jnp.dot(gated_buf[...], wout_ref[...], preferred_element_type=F32)
    o_ref[0] = x_ref[0] + gate_ref[0] * y

    if emit_kv:
        _norm_modulate(o_ref, kg_ref, kshift_ref, kscale_ref, h_buf, ts)
        hk = h_buf[...]
        k_ref[0] = jnp.dot(hk, wk_ref[...], preferred_element_type=F32).astype(BF16)
        vt_ref[0] = lax.dot_general(wvt_ref[...], hk, _NT_DIMS, preferred_element_type=F32).astype(BF16)


def _pool_layer(x, shift, scale, gate, norm_g, w_v, w_z, w_out, kv=None):
    b, s, d = x.shape
    ts = SEQ_TILE
    vec_spec = pl.BlockSpec((1, 1, d), lambda i, j: (i, 0, 0))
    x_spec = pl.BlockSpec((1, ts, d), lambda i, j: (i, j, 0))
    in_specs = [
        x_spec,
        vec_spec, vec_spec, vec_spec,
        _const_spec((1, d)),
        _const_spec((d, A_WIDTH)),
        _const_spec((d, A_WIDTH)),
        _const_spec((A_WIDTH, d)),
    ]
    out_specs = x_spec
    out_shape = jax.ShapeDtypeStruct((b, s, d), F32)
    args = (x, shift, scale, gate, norm_g, w_v, w_z, w_out)
    if kv is not None:
        in_specs += [vec_spec, vec_spec, _const_spec((1, d)), _const_spec((d, KV_WIDTH)),
                     _const_spec((KV_WIDTH, d))]
        out_specs = [x_spec,
                     pl.BlockSpec((1, ts, KV_WIDTH), lambda i, j: (i, j, 0)),
                     pl.BlockSpec((1, KV_WIDTH, ts), lambda i, j: (i, 0, j))]
        out_shape = [out_shape,
                     jax.ShapeDtypeStruct((b, s, KV_WIDTH), BF16),
                     jax.ShapeDtypeStruct((b, KV_WIDTH, s), BF16)]
        args += tuple(kv)
    return pl.pallas_call(
        functools.partial(_pool_kernel, emit_kv=kv is not None),
        grid=(b, s // ts),
        in_specs=in_specs,
        out_specs=out_specs,
        out_shape=out_shape,
        scratch_shapes=[
            pltpu.VMEM((ts, d), BF16),
            pltpu.VMEM((N_GROUPS, ts + MAX_HALO, GROUP_WIDTH), F32),
            pltpu.VMEM((N_GROUPS, ts, GROUP_WIDTH), F32),
            pltpu.VMEM((ts, A_WIDTH), BF16),
            pltpu.VMEM((N_GROUPS, MAX_HALO, GROUP_WIDTH), F32),
        ],
        compiler_params=pltpu.CompilerParams(
            dimension_semantics=("arbitrary", "arbitrary"), vmem_limit_bytes=VMEM_LIMIT_BYTES),
        name="pool_layer",
    )(*args)


def _t5_causal_buckets():
    i = np.arange(BLOCK)[:, None]
    j = np.arange(2 * BLOCK)[None, :]
    n = np.maximum(i + BLOCK - j, 0)
    max_exact = N_BUCKETS // 2
    large = max_exact + (np.log(np.maximum(n, 1) / max_exact) / math.log(MAX_DISTANCE / max_exact)
                         * (N_BUCKETS - max_exact)).astype(np.int32)
    large = np.minimum(large, N_BUCKETS - 1)
    return np.where(n < max_exact, n, large).astype(np.int32)


def _band_codes_t():
    i = np.arange(BLOCK)[None, :]
    j = np.arange(2 * BLOCK)[:, None]
    rel = i + BLOCK - j
    band = (rel >= 0) & (rel < BLOCK)
    return np.where(band, np.where(j < BLOCK, 1.0, 0.0), 2.0).astype(np.float32)


def _bias_kernel(bucket_ref, rel_ref, o_ref):
    bucket = bucket_ref[...]
    for head in range(N_HEADS):
        acc = jnp.zeros((2 * BLOCK, BLOCK), F32)
        for b in range(N_BUCKETS):
            acc = jnp.where(bucket == b, rel_ref[b, head], acc)
        kv_head, g = divmod(head, GQA_GROUPS)
        o_ref[kv_head, g] = acc * LOG2E


def _position_bias_t(rel_bias):
    buckets_t = jnp.asarray(np.ascontiguousarray(_t5_causal_buckets().T))
    return pl.pallas_call(
        _bias_kernel,
        in_specs=[
            pl.BlockSpec(memory_space=pltpu.VMEM),
            pl.BlockSpec(memory_space=pltpu.SMEM),
        ],
        out_specs=pl.BlockSpec(memory_space=pltpu.VMEM),
        out_shape=jax.ShapeDtypeStruct((N_KV_HEADS, GQA_GROUPS, 2 * BLOCK, BLOCK), F32),
        name="position_bias",
    )(buckets_t, rel_bias)


def _attn_kernel(x_ref, shift_ref, scale_ref, gate_ref, g_ref, wqt_ref, wz_ref, kp_ref, kc_ref,
                 vtp_ref, vtc_ref, bias_ref, code_ref, sink_ref, wout_ref, fg_ref, o_ref,
                 h_buf, qt_buf, z_buf, og_buf, s_buf, p_buf, r_buf, y_buf, *, final_norm):
    t = pl.program_id(1)
    n_chunks = ATTN_TILE // ATTN_CHUNK
    pieces = B_WIDTH // PROJ_PIECE
    steps_per_chunk = (ATTN_CHUNK // BLOCK) * N_KV_HEADS
    n_steps = n_chunks * steps_per_chunk
    norm_rows = ATTN_CHUNK // steps_per_chunk

    def norm_slice(c, j):
        _norm_modulate(x_ref, g_ref, shift_ref, scale_ref, h_buf, norm_rows,
                       start=c * ATTN_CHUNK + j * norm_rows)

    def project_piece(c, n):
        chunk = slice(c * ATTN_CHUNK, (c + 1) * ATTN_CHUNK)
        h = h_buf[chunk, :]
        if n < pieces:
            sl = slice(n * PROJ_PIECE, (n + 1) * PROJ_PIECE)
            qt = lax.dot_general(wqt_ref[sl, :], h, _NT_DIMS, preferred_element_type=F32)
            qt = (qt * (HEAD_DIM ** -0.5 * LOG2E)).astype(BF16)
            for blk in range(ATTN_CHUNK // BLOCK):
                qt_buf[c * (ATTN_CHUNK // BLOCK) + blk, sl, :] = qt[:, blk * BLOCK:(blk + 1) * BLOCK]
        else:
            sl = slice((n - pieces) * PROJ_PIECE, (n - pieces + 1) * PROJ_PIECE)
            z = jnp.dot(h, wz_ref[:, sl], preferred_element_type=F32)
            for sub in range(PROJ_PIECE // LANES):
                z_buf[(n - pieces) * (PROJ_PIECE // LANES) + sub, chunk, :] = z[:, sub * LANES:(sub + 1) * LANES]

    def out_piece(c, n):
        chunk = slice(c * ATTN_CHUNK, (c + 1) * ATTN_CHUNK)
        sl = slice(n * PROJ_PIECE, (n + 1) * PROJ_PIECE)
        og = jnp.concatenate([og_buf[sb, chunk, :] for sb in range(B_WIDTH // LANES)], axis=1)
        y_buf[c % 2, n] = jnp.dot(og, wout_ref[:, sl], preferred_element_type=F32)
        if n == pieces - 1:
            for r in range(0, ATTN_CHUNK, NORM_ROWS):
                rows = slice(c * ATTN_CHUNK + r, c * ATTN_CHUNK + r + NORM_ROWS)
                y = jnp.concatenate([y_buf[c % 2, m, r:r + NORM_ROWS, :] for m in range(pieces)], axis=1)
                out = x_ref[0, rows, :] + gate_ref[0] * y
                if final_norm:
                    ms = jnp.mean(out * out, axis=-1, keepdims=True)
                    out = (out * lax.rsqrt(ms + EPS)) * fg_ref[...]
                o_ref[0, rows, :] = out

    def stage_scores(step):
        i, kvh = divmod(step, N_KV_HEADS)
        _scores(i, kvh, qt_buf, kp_ref, kc_ref, s_buf.at[step % 2])

    def stage_softmax(step):
        i, kvh = divmod(step, N_KV_HEADS)
        _softmax(i, kvh, t, s_buf.at[step % 2], code_ref, bias_ref, sink_ref,
                 p_buf.at[step % 2], r_buf.at[step % 2])

    def stage_pv(step):
        i, kvh = divmod(step, N_KV_HEADS)
        _pv_gate(i, kvh, p_buf.at[step % 2], r_buf.at[step % 2], vtp_ref, vtc_ref, z_buf, og_buf)

    norm_slices = [(c, j) for c in range(n_chunks) for j in range(steps_per_chunk)]
    for c, j in norm_slices[:2 * steps_per_chunk]:
        norm_slice(c, j)
    for n in range(2 * pieces):
        project_piece(0, n)

    def out_slices(ko):
        co, no = divmod(ko, steps_per_chunk)
        for n in range(no * pieces // steps_per_chunk, (no + 1) * pieces // steps_per_chunk):
            out_piece(co, n)

    out_lag = steps_per_chunk + 2
    for k in range(n_steps + 2):
        c, j = divmod(k, steps_per_chunk)
        if k < n_steps:
            stage_scores(k)
        if 1 <= k <= n_steps:
            stage_softmax(k - 1)
        if 2 <= k <= n_steps + 1:
            stage_pv(k - 2)
        if c + 2 < n_chunks:
            norm_slice(c + 2, j)
        if c + 1 < n_chunks:
            for n in range(j * 2 * pieces // steps_per_chunk, (j + 1) * 2 * pieces // steps_per_chunk):
                project_piece(c + 1, n)
        if k >= out_lag:
            out_slices(k - out_lag)
    for ko in range(n_steps + 2 - out_lag, n_steps):
        out_slices(ko)


def _scores(i, kvh, qt_buf, kp_ref, kc_ref, s_ref):
    rows = slice(i * BLOCK, (i + 1) * BLOCK)
    k_prev = kp_ref[0] if i == 0 else kc_ref[0, (i - 1) * BLOCK:i * BLOCK, :]
    k2 = jnp.concatenate([k_prev, kc_ref[0, rows, :]], axis=0)
    q_cat = jnp.concatenate(
        [qt_buf[i, (kvh * GQA_GROUPS + g) * HEAD_DIM:(kvh * GQA_GROUPS + g + 1) * HEAD_DIM, :]
         for g in range(GQA_GROUPS)], axis=1)
    q_zeros = jnp.zeros((HEAD_DIM, GQA_GROUPS * BLOCK), BF16)
    q_sel = jnp.concatenate([q_cat, q_zeros] if kvh == 0 else [q_zeros, q_cat], axis=0)
    s = jnp.dot(k2, q_sel, preferred_element_type=F32)
    for g in range(GQA_GROUPS):
        s_ref[g] = s[:, g * BLOCK:(g + 1) * BLOCK]


def _softmax(i, kvh, t, s_ref, code_ref, bias_ref, sink_ref, p_ref, r_ref):
    limit = jnp.where(t == 0, 1.0, 2.0) if i == 0 else 2.0
    for g in range(GQA_GROUPS):
        sink = sink_ref[kvh, :, g * BLOCK:(g + 1) * BLOCK]
        sg = jnp.where(code_ref[...] < limit, s_ref[g] + bias_ref[kvh, g], NEG_INF)
        m = jnp.maximum(jnp.max(sg, axis=0, keepdims=True), sink)
        e = jnp.exp2(sg - m)
        den = jnp.sum(e, axis=0, keepdims=True) + jnp.exp2(sink - m)
        p_ref[g] = e.astype(BF16)
        r_ref[g] = 1.0 / den


def _pv_gate(i, kvh, p_ref, r_ref, vtp_ref, vtc_ref, z_buf, og_buf):
    rows = slice(i * BLOCK, (i + 1) * BLOCK)
    vt_prev = vtp_ref[0] if i == 0 else vtc_ref[0, :, (i - 1) * BLOCK:i * BLOCK]
    vt2 = jnp.concatenate([vt_prev, vtc_ref[0, :, rows]], axis=1)
    vt = vt2[kvh * HEAD_DIM:(kvh + 1) * HEAD_DIM, :]
    pt = jnp.concatenate([p_ref[g] for g in range(GQA_GROUPS)], axis=1)
    ot = jnp.dot(vt, pt, preferred_element_type=F32)
    for p in range(GQA_GROUPS // 2):
        pair_t = jnp.concatenate(
            [ot[:, g * BLOCK:(g + 1) * BLOCK] * r_ref[g] for g in (2 * p, 2 * p + 1)], axis=0)
        slab = kvh * (GQA_GROUPS // 2) + p
        z = z_buf[slab, rows, :]
        og_buf[slab, rows, :] = (pair_t.T * (z * _sigmoid(z))).astype(BF16)


def _attn_layer(x, shift, scale, gate, norm_g, w_q_t, w_z, k, v_t, bias, codes, sinks, w_out, final_g,
                final_norm):
    b, s, d = x.shape
    ts = ATTN_TILE
    blocks_per_tile = ts // BLOCK
    vec_spec = pl.BlockSpec((1, 1, d), lambda i, j: (i, 0, 0))
    score_shape = (2, GQA_GROUPS, 2 * BLOCK, BLOCK)

    def prev_block(j):
        return jnp.maximum(j * blocks_per_tile - 1, 0)

    return pl.pallas_call(
        functools.partial(_attn_kernel, final_norm=final_norm),
        grid=(b, s // ts),
        in_specs=[
            pl.BlockSpec((1, ts, d), lambda i, j: (i, j, 0)),
            vec_spec, vec_spec, vec_spec,
            _const_spec((1, d)),
            _const_spec((B_WIDTH, d)),
            _const_spec((d, B_WIDTH)),
            pl.BlockSpec((1, BLOCK, KV_WIDTH), lambda i, j: (i, prev_block(j), 0)),
            pl.BlockSpec((1, ts, KV_WIDTH), lambda i, j: (i, j, 0)),
            pl.BlockSpec((1, KV_WIDTH, BLOCK), lambda i, j: (i, 0, prev_block(j))),
            pl.BlockSpec((1, KV_WIDTH, ts), lambda i, j: (i, 0, j)),
            _const_spec((N_KV_HEADS, GQA_GROUPS, 2 * BLOCK, BLOCK)),
            _const_spec((2 * BLOCK, BLOCK)),
            _const_spec((N_KV_HEADS, 1, GQA_GROUPS * BLOCK)),
            _const_spec((B_WIDTH, d)),
            _const_spec((1, d)),
        ],
        out_specs=pl.BlockSpec((1, ts, d), lambda i, j: (i, j, 0)),
        out_shape=jax.ShapeDtypeStruct((b, s, d), F32),
        scratch_shapes=[
            pltpu.VMEM((ts, d), BF16),
            pltpu.VMEM((blocks_per_tile, B_WIDTH, BLOCK), BF16),
            pltpu.VMEM((B_WIDTH // LANES, ts, LANES), F32),
            pltpu.VMEM((B_WIDTH // LANES, ts, LANES), BF16),
            pltpu.VMEM(score_shape, F32),
            pltpu.VMEM(score_shape, BF16),
            pltpu.VMEM((2, GQA_GROUPS, 1, BLOCK), F32),
            pltpu.VMEM((2, d // PROJ_PIECE, ATTN_CHUNK, PROJ_PIECE), F32),
        ],
        compiler_params=pltpu.CompilerParams(
            dimension_semantics=("arbitrary", "arbitrary"), vmem_limit_bytes=VMEM_LIMIT_BYTES),
        name="attn_layer",
    )(x, shift, scale, gate, norm_g, w_q_t, w_z, k, k, v_t, v_t, bias, codes, sinks, w_out, final_g)


def kernel(x, c, norm_g, ada_w, ada_b, a_w_in, a_w_group, a_scale, a_w_out, kv_norm_g, kv_ada_w,
           kv_ada_b, w_kv, b_w_in, b_sinks, b_w_out, rel_bias, final_g):
    b, s, d = x.shape
    depth = norm_g.shape[0]
    n_a = a_w_in.shape[0]
    assert (d, s % SEQ_TILE, s % ATTN_TILE) == (D_MODEL, 0, 0) and b <= SUBLANES

    c_pad = jnp.pad(c, ((0, SUBLANES - b), (0, 0)))
    mod = _modulation(c_pad, ada_w, ada_b.reshape(depth, 1, 3 * d))[:, :b]
    mod_kv = _modulation(c_pad, kv_ada_w[None], kv_ada_b.reshape(1, 1, 2 * d))[0, :b]

    def vec(m, k):
        return m[:, k * d:(k + 1) * d].reshape(b, 1, d)

    w_v = _fuse_group_weights(a_w_in, a_w_group, a_scale)
    kv_args = (vec(mod_kv, 0), vec(mod_kv, 1), kv_norm_g.reshape(1, d),
               w_kv[:, :KV_WIDTH].astype(BF16), w_kv[:, KV_WIDTH:].T.astype(BF16))
    for l in range(n_a):
        res = _pool_layer(x, vec(mod[l], 0), vec(mod[l], 1), vec(mod[l], 2), norm_g[l].reshape(1, d),
                          w_v[l], a_w_in[l, :, A_WIDTH:].astype(BF16), a_w_out[l].astype(BF16),
                          kv=kv_args if l == n_a - 1 else None)
        x = res if l < n_a - 1 else res[0]
    k, v_t = res[1], res[2]
    bias = _position_bias_t(rel_bias)
    codes = jnp.asarray(_band_codes_t())
    for l in range(n_a, depth):
        j = l - n_a
        sinks = jnp.repeat(b_sinks[j] * LOG2E, BLOCK).reshape(N_KV_HEADS, 1, GQA_GROUPS * BLOCK)
        x = _attn_layer(x, vec(mod[l], 0), vec(mod[l], 1), vec(mod[l], 2), norm_g[l].reshape(1, d),
                        b_w_in[j, :, :B_WIDTH].T.astype(BF16), b_w_in[j, :, B_WIDTH:].astype(BF16),
                        k, v_t, bias, codes, sinks, b_w_out[j].astype(BF16),
                        final_g.reshape(1, d), final_norm=(l == depth - 1))
    return x
```

```python
import functools
import math

import numpy as np
import jax
import jax.numpy as jnp
from jax import lax
from jax.experimental import pallas as pl
from jax.experimental.pallas import tpu as pltpu

F32 = jnp.float32
BF16 = jnp.bfloat16

D_MODEL = 1024
A_WIDTH = 2048
POOL_WINDOWS = (2, 4, 8, 16)
N_GROUPS = len(POOL_WINDOWS)
GROUP_WIDTH = A_WIDTH // N_GROUPS
HEAD_DIM = 64
N_HEADS = 16
N_KV_HEADS = 2
GQA_GROUPS = N_HEADS // N_KV_HEADS
KV_WIDTH = N_KV_HEADS * HEAD_DIM
B_WIDTH = N_HEADS * HEAD_DIM
BLOCK = 128
N_BUCKETS = 32
MAX_DISTANCE = 128
EPS = 1e-6
NEG_INF = -1e30
LOG2E = math.log2(math.e)

LANES = 128
SUBLANES = 8
VMEM_LIMIT_BYTES = 56 * 1024 * 1024

MAX_HALO = 16
SEQ_TILE = 1024
ATTN_TILE = 1024
ATTN_CHUNK = 256
PROJ_PIECE = 256
NORM_ROWS = 32
POOL_ROWS = 64

_NT_DIMS = (((1,), (1,)), ((), ()))


def _sigmoid(v):
    return 1.0 / (1.0 + jnp.exp(-v))


def _const_spec(shape):
    zeros = (0,) * len(shape)
    return pl.BlockSpec(shape, lambda *_: zeros, pipeline_mode=pl.Buffered(1))


def _mod_kernel(c_ref, w_ref, b_ref, o_ref):
    c = c_ref[...]
    c_act = c * _sigmoid(c)
    o_ref[0] = jnp.dot(c_act, w_ref[0], preferred_element_type=F32) + b_ref[0]


def _modulation(c_pad, w, b):
    n_layers, d, n = w.shape
    rows = c_pad.shape[0]
    nt = D_MODEL
    return pl.pallas_call(
        _mod_kernel,
        grid=(n_layers, n // nt),
        in_specs=[
            pl.BlockSpec((rows, d), lambda l, j: (0, 0)),
            pl.BlockSpec((1, d, nt), lambda l, j: (l, 0, j)),
            pl.BlockSpec((1, 1, nt), lambda l, j: (l, 0, j)),
        ],
        out_specs=pl.BlockSpec((1, rows, nt), lambda l, j: (l, 0, j)),
        out_shape=jax.ShapeDtypeStruct((n_layers, rows, n), F32),
        compiler_params=pltpu.CompilerParams(
            dimension_semantics=("arbitrary", "arbitrary"), vmem_limit_bytes=VMEM_LIMIT_BYTES),
        name="adaln_modulation",
    )(c_pad, w, b)


def _norm_modulate(x_ref, g_ref, shift_ref, scale_ref, h_ref, rows, start=0):
    gain = g_ref[...] * (1.0 + scale_ref[0])
    shift = shift_ref[0]
    for r in range(start, start + rows, NORM_ROWS):
        xv = x_ref[0, r:r + NORM_ROWS, :]
        ms = jnp.mean(xv * xv, axis=-1, keepdims=True)
        xn = xv * lax.rsqrt(ms + EPS)
        h_ref[r:r + NORM_ROWS, :] = (xn * gain + shift).astype(BF16)


def _split_bf16(a):
    hi = a.astype(BF16)
    lo = (a - hi.astype(F32)).astype(BF16)
    return hi, lo


def _fuse_kernel(win_ref, wg_ref, asc_ref, o_ref):
    a_hi, a_lo = _split_bf16(win_ref[0])
    b_hi, b_lo = _split_bf16(wg_ref[0, 0])
    acc = jnp.dot(a_hi, b_hi, preferred_element_type=F32)
    acc = acc + jnp.dot(a_hi, b_lo, preferred_element_type=F32)
    acc = acc + jnp.dot(a_lo, b_hi, preferred_element_type=F32)
    o_ref[0] = (acc * asc_ref[0]).astype(BF16)


def _fuse_group_weights(a_w_in, a_w_group, a_scale):
    n_a, d, _ = a_w_in.shape
    return pl.pallas_call(
        _fuse_kernel,
        grid=(n_a, N_GROUPS),
        in_specs=[
            pl.BlockSpec((1, d, GROUP_WIDTH), lambda l, g: (l, 0, g)),
            pl.BlockSpec((1, 1, GROUP_WIDTH, GROUP_WIDTH), lambda l, g: (l, g, 0, 0)),
            pl.BlockSpec((1, 1, GROUP_WIDTH), lambda l, g: (l, 0, g)),
        ],
        out_specs=pl.BlockSpec((1, d, GROUP_WIDTH), lambda l, g: (l, 0, g)),
        out_shape=jax.ShapeDtypeStruct((n_a, d, A_WIDTH), BF16),
        compiler_params=pltpu.CompilerParams(
            dimension_semantics=("arbitrary", "arbitrary"), vmem_limit_bytes=VMEM_LIMIT_BYTES),
        name="fuse_group_weights",
    )(a_w_in, a_w_group, a_scale.reshape(n_a, 1, A_WIDTH))


def _pool_kernel(x_ref, shift_ref, scale_ref, gate_ref, g_ref, wv_ref, wz_ref, wout_ref, *rest, emit_kv):
    if emit_kv:
        kshift_ref, kscale_ref, kg_ref, wk_ref, wvt_ref, o_ref, k_ref, vt_ref = rest[:8]
    else:
        o_ref = rest[0]
    h_buf, v_ext, z_buf, gated_buf, carry = rest[-5:]
    t = pl.program_id(1)
    ts = SEQ_TILE

    @pl.when(t == 0)
    def _():
        carry[...] = jnp.zeros_like(carry)

    _norm_modulate(x_ref, g_ref, shift_ref, scale_ref, h_buf, ts)
    h = h_buf[...]

    for g, w in enumerate(POOL_WINDOWS):
        cols = slice(g * GROUP_WIDTH, (g + 1) * GROUP_WIDTH)
        v_ext[g, 0:MAX_HALO, :] = carry[g]
        v_ext[g, MAX_HALO:, :] = jnp.dot(h, wv_ref[:, cols], preferred_element_type=F32)
        carry[g] = v_ext[g, ts:ts + MAX_HALO, :]
        z_buf[g] = jnp.dot(h, wz_ref[:, cols], preferred_element_type=F32)

        halo = MAX_HALO if w > SUBLANES else SUBLANES
        for r in range(0, ts, POOL_ROWS):
            ev = v_ext[g, MAX_HALO + r - halo:MAX_HALO + r + POOL_ROWS, :]
            s = ev
            k = 1
            while k < w:
                s = s + pltpu.roll(s, k, 0)
                k *= 2
            s = s[halo:]
            if r == 0:
                pos = t * ts + lax.broadcasted_iota(jnp.int32, (POOL_ROWS, GROUP_WIDTH), 0)
                mean = s * (1.0 / jnp.minimum(pos + 1, w).astype(F32))
            else:
                mean = s * (1.0 / w)
            z = z_buf[g, r:r + POOL_ROWS, :]
            gated_buf[r:r + POOL_ROWS, cols] = ((mean - ev[halo:]) * (z * _sigmoid(z))).astype(BF16)

    y = jnp.dot(gated_buf[...], wout_ref[...], preferred_element_type=F32)
    o_ref[0] = x_ref[0] + gate_ref[0] * y

    if emit_kv:
        _norm_modulate(o_ref, kg_ref, kshift_ref, kscale_ref, h_buf, ts)
        hk = h_buf[...]
        k_ref[0] = jnp.dot(hk, wk_ref[...], preferred_element_type=F32).astype(BF16)
        vt_ref[0] = lax.dot_general(wvt_ref[...], hk, _NT_DIMS, preferred_element_type=F32).astype(BF16)


def _pool_layer(x, shift, scale, gate, norm_g, w_v, w_z, w_out, kv=None):
    b, s, d = x.shape
    ts = SEQ_TILE
    vec_spec = pl.BlockSpec((1, 1, d), lambda i, j: (i, 0, 0))
    x_spec = pl.BlockSpec((1, ts, d), lambda i, j: (i, j, 0))
    in_specs = [
        x_spec,
        vec_spec, vec_spec, vec_spec,
        _const_spec((1, d)),
        _const_spec((d, A_WIDTH)),
        _const_spec((d, A_WIDTH)),
        _const_spec((A_WIDTH, d)),
    ]
    out_specs = x_spec
    out_shape = jax.ShapeDtypeStruct((b, s, d), F32)
    args = (x, shift, scale, gate, norm_g, w_v, w_z, w_out)
    if kv is not None:
        in_specs += [vec_spec, vec_spec, _const_spec((1, d)), _const_spec((d, KV_WIDTH)),
                     _const_spec((KV_WIDTH, d))]
        out_specs = [x_spec,
                     pl.BlockSpec((1, ts, KV_WIDTH), lambda i, j: (i, j, 0)),
                     pl.BlockSpec((1, KV_WIDTH, ts), lambda i, j: (i, 0, j))]
        out_shape = [out_shape,
                     jax.ShapeDtypeStruct((b, s, KV_WIDTH), BF16),
                     jax.ShapeDtypeStruct((b, KV_WIDTH, s), BF16)]
        args += tuple(kv)
    return pl.pallas_call(
        functools.partial(_pool_kernel, emit_kv=kv is not None),
        grid=(b, s // ts),
        in_specs=in_specs,
        out_specs=out_specs,
        out_shape=out_shape,
        scratch_shapes=[
            pltpu.VMEM((ts, d), BF16),
            pltpu.VMEM((N_GROUPS, ts + MAX_HALO, GROUP_WIDTH), F32),
            pltpu.VMEM((N_GROUPS, ts, GROUP_WIDTH), F32),
            pltpu.VMEM((ts, A_WIDTH), BF16),
            pltpu.VMEM((N_GROUPS, MAX_HALO, GROUP_WIDTH), F32),
        ],
        compiler_params=pltpu.CompilerParams(
            dimension_semantics=("arbitrary", "arbitrary"), vmem_limit_bytes=VMEM_LIMIT_BYTES),
        name="pool_layer",
    )(*args)


def _t5_causal_buckets():
    i = np.arange(BLOCK)[:, None]
    j = np.arange(2 * BLOCK)[None, :]
    n = np.maximum(i + BLOCK - j, 0)
    max_exact = N_BUCKETS // 2
    large = max_exact + (np.log(np.maximum(n, 1) / max_exact) / math.log(MAX_DISTANCE / max_exact)
                         * (N_BUCKETS - max_exact)).astype(np.int32)
    large = np.minimum(large, N_BUCKETS - 1)
    return np.where(n < max_exact, n, large).astype(np.int32)


def _band_codes_t():
    i = np.arange(BLOCK)[None, :]
    j = np.arange(2 * BLOCK)[:, None]
    rel = i + BLOCK - j
    band = (rel >= 0) & (rel < BLOCK)
    return np.where(band, np.where(j < BLOCK, 1.0, 0.0), 2.0).astype(np.float32)


def _bias_kernel(bucket_ref, rel_ref, o_ref):
    bucket = bucket_ref[...]
    for head in range(N_HEADS):
        acc = jnp.zeros((2 * BLOCK, BLOCK), F32)
        for b in range(N_BUCKETS):
            acc = jnp.where(bucket == b, rel_ref[b, head], acc)
        kv_head, g = divmod(head, GQA_GROUPS)
        o_ref[kv_head, g] = acc * LOG2E


def _position_bias_t(rel_bias):
    buckets_t = jnp.asarray(np.ascontiguousarray(_t5_causal_buckets().T))
    return pl.pallas_call(
        _bias_kernel,
        in_specs=[
            pl.BlockSpec(memory_space=pltpu.VMEM),
            pl.BlockSpec(memory_space=pltpu.SMEM),
        ],
        out_specs=pl.BlockSpec(memory_space=pltpu.VMEM),
        out_shape=jax.ShapeDtypeStruct((N_KV_HEADS, GQA_GROUPS, 2 * BLOCK, BLOCK), F32),
        name="position_bias",
    )(buckets_t, rel_bias)


def _attn_kernel(x_ref, shift_ref, scale_ref, gate_ref, g_ref, wqt_ref, wz_ref, kp_ref, kc_ref,
                 vtp_ref, vtc_ref, bias_ref, code_ref, sink_ref, wout_ref, fg_ref, o_ref,
                 h_buf, qt_buf, z_buf, og_buf, s_buf, p_buf, r_buf, y_buf, *, final_norm):
    t = pl.program_id(1)
    n_chunks = ATTN_TILE // ATTN_CHUNK
    pieces = B_WIDTH // PROJ_PIECE
    steps_per_chunk = (ATTN_CHUNK // BLOCK) * N_KV_HEADS
    n_steps = n_chunks * steps_per_chunk
    norm_rows = ATTN_CHUNK // steps_per_chunk

    def norm_slice(c, j):
        _norm_modulate(x_ref, g_ref, shift_ref, scale_ref, h_buf, norm_rows,
                       start=c * ATTN_CHUNK + j * norm_rows)

    def project_piece(c, n):
        chunk = slice(c * ATTN_CHUNK, (c + 1) * ATTN_CHUNK)
        h = h_buf[chunk, :]
        if n < pieces:
            sl = slice(n * PROJ_PIECE, (n + 1) * PROJ_PIECE)
            qt = lax.dot_general(wqt_ref[sl, :], h, _NT_DIMS, preferred_element_type=F32)
            qt = (qt * (HEAD_DIM ** -0.5 * LOG2E)).astype(BF16)
            for blk in range(ATTN_CHUNK // BLOCK):
                qt_buf[c * (ATTN_CHUNK // BLOCK) + blk, sl, :] = qt[:, blk * BLOCK:(blk + 1) * BLOCK]
        else:
            sl = slice((n - pieces) * PROJ_PIECE, (n - pieces + 1) * PROJ_PIECE)
            z = jnp.dot(h, wz_ref[:, sl], preferred_element_type=F32)
            for sub in range(PROJ_PIECE // LANES):
                z_buf[(n - pieces) * (PROJ_PIECE // LANES) + sub, chunk, :] = z[:, sub * LANES:(sub + 1) * LANES]

    def out_piece(c, n):
        chunk = slice(c * ATTN_CHUNK, (c + 1) * ATTN_CHUNK)
        sl = slice(n * PROJ_PIECE, (n + 1) * PROJ_PIECE)
        og = jnp.concatenate([og_buf[sb, chunk, :] for sb in range(B_WIDTH // LANES)], axis=1)
        y_buf[c % 2, n] = jnp.dot(og, wout_ref[:, sl], preferred_element_type=F32)
        if n == pieces - 1:
            for r in range(0, ATTN_CHUNK, NORM_ROWS):
                rows = slice(c * ATTN_CHUNK + r, c * ATTN_CHUNK + r + NORM_ROWS)
                y = jnp.concatenate([y_buf[c % 2, m, r:r + NORM_ROWS, :] for m in range(pieces)], axis=1)
                out = x_ref[0, rows, :] + gate_ref[0] * y
                if final_norm:
                    ms = jnp.mean(out * out, axis=-1, keepdims=True)
                    out = (out * lax.rsqrt(ms + EPS)) * fg_ref[...]
                o_ref[0, rows, :] = out

    def stage_scores(step):
        i, kvh = divmod(step, N_KV_HEADS)
        _scores(i, kvh, qt_buf, kp_ref, kc_ref, s_buf.at[step % 2])

    def stage_softmax(step):
        i, kvh = divmod(step, N_KV_HEADS)
        _softmax(i, kvh, t, s_buf.at[step % 2], code_ref, bias_ref, sink_ref,
                 p_buf.at[step % 2], r_buf.at[step % 2])

    def stage_pv(step):
        i, kvh = divmod(step, N_KV_HEADS)
        _pv_gate(i, kvh, p_buf.at[step % 2], r_buf.at[step % 2], vtp_ref, vtc_ref, z_buf, og_buf)

    norm_slices = [(c, j) for c in range(n_chunks) for j in range(steps_per_chunk)]
    for c, j in norm_slices[:2 * steps_per_chunk]:
        norm_slice(c, j)
    for n in range(2 * pieces):
        project_piece(0, n)

    def out_slices(ko):
        co, no = divmod(ko, steps_per_chunk)
        for n in range(no * pieces // steps_per_chunk, (no + 1) * pieces // steps_per_chunk):
            out_piece(co, n)

    out_lag = steps_per_chunk + 2
    for k in range(n_steps + 2):
        c, j = divmod(k, steps_per_chunk)
        if k < n_steps:
            stage_scores(k)
        if 1 <= k <= n_steps:
            stage_softmax(k - 1)
        if 2 <= k <= n_steps + 1:
            stage_pv(k - 2)
        if c + 2 < n_chunks:
            norm_slice(c + 2, j)
        if c + 1 < n_chunks:
            for n in range(j * 2 * pieces // steps_per_chunk, (j + 1) * 2 * pieces // steps_per_chunk):
                project_piece(c + 1, n)
        if k >= out_lag:
            out_slices(k - out_lag)
    for ko in range(n_steps + 2 - out_lag, n_steps):
        out_slices(ko)


def _scores(i, kvh, qt_buf, kp_ref, kc_ref, s_ref):
    rows = slice(i * BLOCK, (i + 1) * BLOCK)
    k_prev = kp_ref[0] if i == 0 else kc_ref[0, (i - 1) * BLOCK:i * BLOCK, :]
    k2 = jnp.concatenate([k_prev, kc_ref[0, rows, :]], axis=0)
    q_cat = jnp.concatenate(
        [qt_buf[i, (kvh * GQA_GROUPS + g) * HEAD_DIM:(kvh * GQA_GROUPS + g + 1) * HEAD_DIM, :]
         for g in range(GQA_GROUPS)], axis=1)
    q_zeros = jnp.zeros((HEAD_DIM, GQA_GROUPS * BLOCK), BF16)
    q_sel = jnp.concatenate([q_cat, q_zeros] if kvh == 0 else [q_zeros, q_cat], axis=0)
    s = jnp.dot(k2, q_sel, preferred_element_type=F32)
    for g in range(GQA_GROUPS):
        s_ref[g] = s[:, g * BLOCK:(g + 1) * BLOCK]


def _softmax(i, kvh, t, s_ref, code_ref, bias_ref, sink_ref, p_ref, r_ref):
    limit = jnp.where(t == 0, 1.0, 2.0) if i == 0 else 2.0
    for g in range(GQA_GROUPS):
        sink = sink_ref[kvh, :, g * BLOCK:(g + 1) * BLOCK]
        sg = jnp.where(code_ref[...] < limit, s_ref[g] + bias_ref[kvh, g], NEG_INF)
        m = jnp.maximum(jnp.max(sg, axis=0, keepdims=True), sink)
        e = jnp.exp2(sg - m)
        den = jnp.sum(e, axis=0, keepdims=True) + jnp.exp2(sink - m)
        p_ref[g] = e.astype(BF16)
        r_ref[g] = 1.0 / den


def _pv_gate(i, kvh, p_ref, r_ref, vtp_ref, vtc_ref, z_buf, og_buf):
    rows = slice(i * BLOCK, (i + 1) * BLOCK)
    vt_prev = vtp_ref[0] if i == 0 else vtc_ref[0, :, (i - 1) * BLOCK:i * BLOCK]
    vt2 = jnp.concatenate([vt_prev, vtc_ref[0, :, rows]], axis=1)
    vt = vt2[kvh * HEAD_DIM:(kvh + 1) * HEAD_DIM, :]
    pt = jnp.concatenate([p_ref[g] for g in range(GQA_GROUPS)], axis=1)
    ot = jnp.dot(vt, pt, preferred_element_type=F32)
    for p in range(GQA_GROUPS // 2):
        pair_t = jnp.concatenate(
            [ot[:, g * BLOCK:(g + 1) * BLOCK] * r_ref[g] for g in (2 * p, 2 * p + 1)], axis=0)
        slab = kvh * (GQA_GROUPS // 2) + p
        z = z_buf[slab, rows, :]
        og_buf[slab, rows, :] = (pair_t.T * (z * _sigmoid(z))).astype(BF16)


def _attn_layer(x, shift, scale, gate, norm_g, w_q_t, w_z, k, v_t, bias, codes, sinks, w_out, final_g,
                final_norm):
    b, s, d = x.shape
    ts = ATTN_TILE
    blocks_per_tile = ts // BLOCK
    vec_spec = pl.BlockSpec((1, 1, d), lambda i, j: (i, 0, 0))
    score_shape = (2, GQA_GROUPS, 2 * BLOCK, BLOCK)

    def prev_block(j):
        return jnp.maximum(j * blocks_per_tile - 1, 0)

    return pl.pallas_call(
        functools.partial(_attn_kernel, final_norm=final_norm),
        grid=(b, s // ts),
        in_specs=[
            pl.BlockSpec((1, ts, d), lambda i, j: (i, j, 0)),
            vec_spec, vec_spec, vec_spec,
            _const_spec((1, d)),
            _const_spec((B_WIDTH, d)),
            _const_spec((d, B_WIDTH)),
            pl.BlockSpec((1, BLOCK, KV_WIDTH), lambda i, j: (i, prev_block(j), 0)),
            pl.BlockSpec((1, ts, KV_WIDTH), lambda i, j: (i, j, 0)),
            pl.BlockSpec((1, KV_WIDTH, BLOCK), lambda i, j: (i, 0, prev_block(j))),
            pl.BlockSpec((1, KV_WIDTH, ts), lambda i, j: (i, 0, j)),
            _const_spec((N_KV_HEADS, GQA_GROUPS, 2 * BLOCK, BLOCK)),
            _const_spec((2 * BLOCK, BLOCK)),
            _const_spec((N_KV_HEADS, 1, GQA_GROUPS * BLOCK)),
            _const_spec((B_WIDTH, d)),
            _const_spec((1, d)),
        ],
        out_specs=pl.BlockSpec((1, ts, d), lambda i, j: (i, j, 0)),
        out_shape=jax.ShapeDtypeStruct((b, s, d), F32),
        scratch_shapes=[
            pltpu.VMEM((ts, d), BF16),
            pltpu.VMEM((blocks_per_tile, B_WIDTH, BLOCK), BF16),
            pltpu.VMEM((B_WIDTH // LANES, ts, LANES), F32),
            pltpu.VMEM((B_WIDTH // LANES, ts, LANES), BF16),
            pltpu.VMEM(score_shape, F32),
            pltpu.VMEM(score_shape, BF16),
            pltpu.VMEM((2, GQA_GROUPS, 1, BLOCK), F32),
            pltpu.VMEM((2, d // PROJ_PIECE, ATTN_CHUNK, PROJ_PIECE), F32),
        ],
        compiler_params=pltpu.CompilerParams(
            dimension_semantics=("arbitrary", "arbitrary"), vmem_limit_bytes=VMEM_LIMIT_BYTES),
        name="attn_layer",
    )(x, shift, scale, gate, norm_g, w_q_t, w_z, k, k, v_t, v_t, bias, codes, sinks, w_out, final_g)


def kernel(x, c, norm_g, ada_w, ada_b, a_w_in, a_w_group, a_scale, a_w_out, kv_norm_g, kv_ada_w,
           kv_ada_b, w_kv, b_w_in, b_sinks, b_w_out, rel_bias, final_g):
    b, s, d = x.shape
    depth = norm_g.shape[0]
    n_a = a_w_in.shape[0]
    assert (d, s % SEQ_TILE, s % ATTN_TILE) == (D_MODEL, 0, 0) and b <= SUBLANES

    c_pad = jnp.pad(c, ((0, SUBLANES - b), (0, 0)))
    mod = _modulation(c_pad, ada_w, ada_b.reshape(depth, 1, 3 * d))[:, :b]
    mod_kv = _modulation(c_pad, kv_ada_w[None], kv_ada_b.reshape(1, 1, 2 * d))[0, :b]

    def vec(m, k):
        return m[:, k * d:(k + 1) * d].reshape(b, 1, d)

    w_v = _fuse_group_weights(a_w_in, a_w_group, a_scale)
    kv_args = (vec(mod_kv, 0), vec(mod_kv, 1), kv_norm_g.reshape(1, d),
               w_kv[:, :KV_WIDTH].astype(BF16), w_kv[:, KV_WIDTH:].T.astype(BF16))
    for l in range(n_a):
        res = _pool_layer(x, vec(mod[l], 0), vec(mod[l], 1), vec(mod[l], 2), norm_g[l].reshape(1, d),
                          w_v[l], a_w_in[l, :, A_WIDTH:].astype(BF16), a_w_out[l].astype(BF16),
                          kv=kv_args if l == n_a - 1 else None)
        x = res if l < n_a - 1 else res[0]
    k, v_t = res[1], res[2]
    bias = _position_bias_t(rel_bias)
    codes = jnp.asarray(_band_codes_t())
    for l in range(n_a, depth):
        j = l - n_a
        sinks = jnp.repeat(b_sinks[j] * LOG2E, BLOCK).reshape(N_KV_HEADS, 1, GQA_GROUPS * BLOCK)
        x = _attn_layer(x, vec(mod[l], 0), vec(mod[l], 1), vec(mod[l], 2), norm_g[l].reshape(1, d),
                        b_w_in[j, :, :B_WIDTH].T.astype(BF16), b_w_in[j, :, B_WIDTH:].astype(BF16),
                        k, v_t, bias, codes, sinks, b_w_out[j].astype(BF16),
                        final_g.reshape(1, d), final_norm=(l == depth - 1))
    return x
```

```python
import functools
import math

import numpy as np
import jax
import jax.numpy as jnp
from jax import lax
from jax.experimental import pallas as pl
from jax.experimental.pallas import tpu as pltpu

F32 = jnp.float32
BF16 = jnp.bfloat16

D_MODEL = 1024
A_WIDTH = 2048
POOL_WINDOWS = (2, 4, 8, 16)
N_GROUPS = len(POOL_WINDOWS)
GROUP_WIDTH = A_WIDTH // N_GROUPS
HEAD_DIM = 64
N_HEADS = 16
N_KV_HEADS = 2
GQA_GROUPS = N_HEADS // N_KV_HEADS
KV_WIDTH = N_KV_HEADS * HEAD_DIM
B_WIDTH = N_HEADS * HEAD_DIM
BLOCK = 128
N_BUCKETS = 32
MAX_DISTANCE = 128
EPS = 1e-6
NEG_INF = -1e30
LOG2E = math.log2(math.e)

LANES = 128
SUBLANES = 8
VMEM_LIMIT_BYTES = 56 * 1024 * 1024

MAX_HALO = 16
SEQ_TILE = 1024
ATTN_TILE = 1024
ATTN_CHUNK = 256
PROJ_PIECE = 256
KV_CHUNK = 256
NORM_ROWS = 32
POOL_ROWS = 64

_NT_DIMS = (((1,), (1,)), ((), ()))


def _sigmoid(v):
    return 1.0 / (1.0 + jnp.exp(-v))


def _const_spec(shape):
    zeros = (0,) * len(shape)
    return pl.BlockSpec(shape, lambda *_: zeros, pipeline_mode=pl.Buffered(1))


def _mod_kernel(c_ref, w_ref, b_ref, o_ref):
    c = c_ref[...]
    c_act = c * _sigmoid(c)
    o_ref[0] = jnp.dot(c_act, w_ref[0], preferred_element_type=F32) + b_ref[0]


def _modulation(c_pad, w, b):
    n_layers, d, n = w.shape
    rows = c_pad.shape[0]
    nt = D_MODEL
    return pl.pallas_call(
        _mod_kernel,
        grid=(n_layers, n // nt),
        in_specs=[
            pl.BlockSpec((rows, d), lambda l, j: (0, 0)),
            pl.BlockSpec((1, d, nt), lambda l, j: (l, 0, j)),
            pl.BlockSpec((1, 1, nt), lambda l, j: (l, 0, j)),
        ],
        out_specs=pl.BlockSpec((1, rows, nt), lambda l, j: (l, 0, j)),
        out_shape=jax.ShapeDtypeStruct((n_layers, rows, n), F32),
        compiler_params=pltpu.CompilerParams(
            dimension_semantics=("arbitrary", "arbitrary"), vmem_limit_bytes=VMEM_LIMIT_BYTES),
        name="adaln_modulation",
    )(c_pad, w, b)


def _norm_modulate(x_ref, g_ref, shift_ref, scale_ref, h_ref, rows, start=0):
    gain = g_ref[...] * (1.0 + scale_ref[0])
    shift = shift_ref[0]
    for r in range(start, start + rows, NORM_ROWS):
        xv = x_ref[0, r:r + NORM_ROWS, :]
        ms = jnp.mean(xv * xv, axis=-1, keepdims=True)
        xn = xv * lax.rsqrt(ms + EPS)
        h_ref[r:r + NORM_ROWS, :] = (xn * gain + shift).astype(BF16)


def _split_bf16(a):
    hi = a.astype(BF16)
    lo = (a - hi.astype(F32)).astype(BF16)
    return hi, lo


def _fuse_kernel(wu_ref, wzin_ref, wg_ref, asc_ref, wv_ref, wz_ref):
    a_hi, a_lo = _split_bf16(wu_ref[0])
    b_hi, b_lo = _split_bf16(wg_ref[0, 0])
    acc = jnp.dot(a_hi, b_hi, preferred_element_type=F32)
    acc = acc + jnp.dot(a_hi, b_lo, preferred_element_type=F32)
    acc = acc + jnp.dot(a_lo, b_hi, preferred_element_type=F32)
    wv_ref[0] = (acc * asc_ref[0]).astype(BF16)
    wz_ref[0] = wzin_ref[0].astype(BF16)


def _fuse_group_weights(a_w_in, a_w_group, a_scale):
    n_a, d, _ = a_w_in.shape
    col_block = pl.BlockSpec((1, d, GROUP_WIDTH), lambda l, g: (l, 0, g))
    return pl.pallas_call(
        _fuse_kernel,
        grid=(n_a, N_GROUPS),
        in_specs=[
            col_block,
            pl.BlockSpec((1, d, GROUP_WIDTH), lambda l, g: (l, 0, N_GROUPS + g)),
            pl.BlockSpec((1, 1, GROUP_WIDTH, GROUP_WIDTH), lambda l, g: (l, g, 0, 0)),
            pl.BlockSpec((1, 1, GROUP_WIDTH), lambda l, g: (l, 0, g)),
        ],
        out_specs=[col_block, col_block],
        out_shape=[jax.ShapeDtypeStruct((n_a, d, A_WIDTH), BF16)] * 2,
        compiler_params=pltpu.CompilerParams(
            dimension_semantics=("arbitrary", "arbitrary"), vmem_limit_bytes=VMEM_LIMIT_BYTES),
        name="fuse_group_weights",
    )(a_w_in, a_w_in, a_w_group, a_scale.reshape(n_a, 1, A_WIDTH))


def _pool_kernel(x_ref, shift_ref, scale_ref, gate_ref, g_ref, wv_ref, wz_ref, wout_ref, *rest, emit_kv):
    if emit_kv:
        kshift_ref, kscale_ref, kg_ref, wk_ref, wvt_ref, o_ref, k_ref, vt_ref = rest[:8]
        hk_buf = rest[-1]
        rest = rest[:-1]
    else:
        o_ref = rest[0]
    h_buf, v_ext, z_buf, gated_buf, carry = rest[-5:]
    t = pl.program_id(1)
    ts = SEQ_TILE

    @pl.when(t == 0)
    def _():
        carry[...] = jnp.zeros_like(carry)

    _norm_modulate(x_ref, g_ref, shift_ref, scale_ref, h_buf, ts)
    h = h_buf[...]

    for g, w in enumerate(POOL_WINDOWS):
        cols = slice(g * GROUP_WIDTH, (g + 1) * GROUP_WIDTH)
        v_ext[g, 0:MAX_HALO, :] = carry[g]
        v_ext[g, MAX_HALO:, :] = jnp.dot(h, wv_ref[0, :, cols], preferred_element_type=F32)
        carry[g] = v_ext[g, ts:ts + MAX_HALO, :]
        z_buf[g] = jnp.dot(h, wz_ref[0, :, cols], preferred_element_type=F32)

        halo = MAX_HALO if w > SUBLANES else SUBLANES
        for r in range(0, ts, POOL_ROWS):
            ev = v_ext[g, MAX_HALO + r - halo:MAX_HALO + r + POOL_ROWS, :]
            s = ev
            k = 1
            while k < w:
                s = s + pltpu.roll(s, k, 0)
                k *= 2
            s = s[halo:]
            if r == 0:
                pos = t * ts + lax.broadcasted_iota(jnp.int32, (POOL_ROWS, GROUP_WIDTH), 0)
                mean = s * (1.0 / jnp.minimum(pos + 1, w).astype(F32))
            else:
                mean = s * (1.0 / w)
            z = z_buf[g, r:r + POOL_ROWS, :]
            gated_buf[r:r + POOL_ROWS, cols] = ((mean - ev[halo:]) * (z * _sigmoid(z))).astype(BF16)

    if not emit_kv:
        y = jnp.dot(gated_buf[...], wout_ref[...], preferred_element_type=F32)
        o_ref[0] = x_ref[0] + gate_ref[0] * y
        return

    def out_chunk(r):
        rows = slice(r, r + KV_CHUNK)
        y = jnp.dot(gated_buf[rows, :], wout_ref[...], preferred_element_type=F32)
        o_ref[0, rows, :] = x_ref[0, rows, :] + gate_ref[0] * y

    def kv_chunk(r):
        rows = slice(r, r + KV_CHUNK)
        _norm_modulate(o_ref, kg_ref, kshift_ref, kscale_ref, hk_buf, KV_CHUNK, start=r)
        hk = hk_buf[rows, :]
        k_ref[0, rows, :] = jnp.dot(hk, wk_ref[...], preferred_element_type=F32).astype(BF16)
        vt_ref[0, :, rows] = lax.dot_general(wvt_ref[...], hk, _NT_DIMS,
                                             preferred_element_type=F32).astype(BF16)

    starts = list(range(0, ts, KV_CHUNK))
    out_chunk(starts[0])
    for prev, cur in zip(starts[:-1], starts[1:]):
        out_chunk(cur)
        kv_chunk(prev)
    kv_chunk(starts[-1])


def _pool_layer(layer, x, shift, scale, gate, norm_g, w_v, w_z, w_out, kv=None):
    b, s, d = x.shape
    ts = SEQ_TILE
    vec_spec = pl.BlockSpec((1, 1, d), lambda i, j: (i, 0, 0))
    x_spec = pl.BlockSpec((1, ts, d), lambda i, j: (i, j, 0))
    layer_weights = pl.BlockSpec((1, d, A_WIDTH), lambda i, j: (layer, 0, 0), pipeline_mode=pl.Buffered(1))
    in_specs = [
        x_spec,
        vec_spec, vec_spec, vec_spec,
        _const_spec((1, d)),
        layer_weights,
        layer_weights,
        _const_spec((A_WIDTH, d)),
    ]
    out_specs = x_spec
    out_shape = jax.ShapeDtypeStruct((b, s, d), F32)
    args = (x, shift, scale, gate, norm_g, w_v, w_z, w_out)
    if kv is not None:
        in_specs += [vec_spec, vec_spec, _const_spec((1, d)), _const_spec((d, KV_WIDTH)),
                     _const_spec((KV_WIDTH, d))]
        out_specs = [x_spec,
                     pl.BlockSpec((1, ts, KV_WIDTH), lambda i, j: (i, j, 0)),
                     pl.BlockSpec((1, KV_WIDTH, ts), lambda i, j: (i, 0, j))]
        out_shape = [out_shape,
                     jax.ShapeDtypeStruct((b, s, KV_WIDTH), BF16),
                     jax.ShapeDtypeStruct((b, KV_WIDTH, s), BF16)]
        args += tuple(kv)
    scratch_shapes = [
        pltpu.VMEM((ts, d), BF16),
        pltpu.VMEM((N_GROUPS, ts + MAX_HALO, GROUP_WIDTH), F32),
        pltpu.VMEM((N_GROUPS, ts, GROUP_WIDTH), F32),
        pltpu.VMEM((ts, A_WIDTH), BF16),
        pltpu.VMEM((N_GROUPS, MAX_HALO, GROUP_WIDTH), F32),
    ]
    if kv is not None:
        scratch_shapes.append(pltpu.VMEM((ts, d), BF16))
    return pl.pallas_call(
        functools.partial(_pool_kernel, emit_kv=kv is not None),
        grid=(b, s // ts),
        in_specs=in_specs,
        out_specs=out_specs,
        out_shape=out_shape,
        scratch_shapes=scratch_shapes,
        compiler_params=pltpu.CompilerParams(
            dimension_semantics=("arbitrary", "arbitrary"), vmem_limit_bytes=VMEM_LIMIT_BYTES),
        name="pool_layer",
    )(*args)


def _t5_causal_buckets():
    i = np.arange(BLOCK)[:, None]
    j = np.arange(2 * BLOCK)[None, :]
    n = np.maximum(i + BLOCK - j, 0)
    max_exact = N_BUCKETS // 2
    large = max_exact + (np.log(np.maximum(n, 1) / max_exact) / math.log(MAX_DISTANCE / max_exact)
                         * (N_BUCKETS - max_exact)).astype(np.int32)
    large = np.minimum(large, N_BUCKETS - 1)
    return np.where(n < max_exact, n, large).astype(np.int32)


def _band_codes_t():
    i = np.arange(BLOCK)[None, :]
    j = np.arange(2 * BLOCK)[:, None]
    rel = i + BLOCK - j
    band = (rel >= 0) & (rel < BLOCK)
    return np.where(band, np.where(j < BLOCK, 1.0, 0.0), 2.0).astype(np.float32)


def _bias_kernel(bucket_ref, rel_ref, o_ref):
    bucket = bucket_ref[...]
    for head in range(N_HEADS):
        acc = jnp.zeros((2 * BLOCK, BLOCK), F32)
        for b in range(N_BUCKETS):
            acc = jnp.where(bucket == b, rel_ref[b, head], acc)
        kv_head, g = divmod(head, GQA_GROUPS)
        o_ref[kv_head, g] = acc * LOG2E


def _position_bias_t(rel_bias):
    buckets_t = jnp.asarray(np.ascontiguousarray(_t5_causal_buckets().T))
    return pl.pallas_call(
        _bias_kernel,
        in_specs=[
            pl.BlockSpec(memory_space=pltpu.VMEM),
            pl.BlockSpec(memory_space=pltpu.SMEM),
        ],
        out_specs=pl.BlockSpec(memory_space=pltpu.VMEM),
        out_shape=jax.ShapeDtypeStruct((N_KV_HEADS, GQA_GROUPS, 2 * BLOCK, BLOCK), F32),
        name="position_bias",
    )(buckets_t, rel_bias)


def _attn_kernel(x_ref, shift_ref, scale_ref, gate_ref, g_ref, wqt_ref, wz_ref, kp_ref, kc_ref,
                 vtp_ref, vtc_ref, bias_ref, code_ref, sink_ref, wout_ref, fg_ref, o_ref,
                 h_buf, qt_buf, z_buf, og_buf, s_buf, p_buf, r_buf, y_buf, *, final_norm):
    t = pl.program_id(1)
    n_chunks = ATTN_TILE // ATTN_CHUNK
    pieces = B_WIDTH // PROJ_PIECE
    steps_per_chunk = (ATTN_CHUNK // BLOCK) * N_KV_HEADS
    n_steps = n_chunks * steps_per_chunk
    norm_rows = ATTN_CHUNK // steps_per_chunk

    def norm_slice(c, j):
        _norm_modulate(x_ref, g_ref, shift_ref, scale_ref, h_buf, norm_rows,
                       start=c * ATTN_CHUNK + j * norm_rows)

    def project_piece(c, n):
        chunk = slice(c * ATTN_CHUNK, (c + 1) * ATTN_CHUNK)
        h = h_buf[chunk, :]
        if n < pieces:
            sl = slice(n * PROJ_PIECE, (n + 1) * PROJ_PIECE)
            qt = lax.dot_general(wqt_ref[0, sl, :], h, _NT_DIMS, preferred_element_type=F32)
            qt = (qt * (HEAD_DIM ** -0.5 * LOG2E)).astype(BF16)
            for blk in range(ATTN_CHUNK // BLOCK):
                qt_buf[c * (ATTN_CHUNK // BLOCK) + blk, sl, :] = qt[:, blk * BLOCK:(blk + 1) * BLOCK]
        else:
            sl = slice((n - pieces) * PROJ_PIECE, (n - pieces + 1) * PROJ_PIECE)
            z = jnp.dot(h, wz_ref[0, :, sl], preferred_element_type=F32)
            for sub in range(PROJ_PIECE // LANES):
                z_buf[(n - pieces) * (PROJ_PIECE // LANES) + sub, chunk, :] = z[:, sub * LANES:(sub + 1) * LANES]

    def out_piece(c, n):
        chunk = slice(c * ATTN_CHUNK, (c + 1) * ATTN_CHUNK)
        sl = slice(n * PROJ_PIECE, (n + 1) * PROJ_PIECE)
        og = jnp.concatenate([og_buf[sb, chunk, :] for sb in range(B_WIDTH // LANES)], axis=1)
        y_buf[c % 2, n] = jnp.dot(og, wout_ref[:, sl], preferred_element_type=F32)
        if n == pieces - 1:
            for r in range(0, ATTN_CHUNK, NORM_ROWS):
                rows = slice(c * ATTN_CHUNK + r, c * ATTN_CHUNK + r + NORM_ROWS)
                y = jnp.concatenate([y_buf[c % 2, m, r:r + NORM_ROWS, :] for m in range(pieces)], axis=1)
                out = x_ref[0, rows, :] + gate_ref[0] * y
                if final_norm:
                    ms = jnp.mean(out * out, axis=-1, keepdims=True)
                    out = (out * lax.rsqrt(ms + EPS)) * fg_ref[...]
                o_ref[0, rows, :] = out

    def stage_scores(step):
        i, kvh = divmod(step, N_KV_HEADS)
        _scores(i, kvh, qt_buf, kp_ref, kc_ref, s_buf.at[step % 2])

    def stage_softmax(step):
        i, kvh = divmod(step, N_KV_HEADS)
        _softmax(i, kvh, t, s_buf.at[step % 2], code_ref, bias_ref, sink_ref,
                 p_buf.at[step % 2], r_buf.at[step % 2])

    def stage_pv(step):
        i, kvh = divmod(step, N_KV_HEADS)
        _pv_gate(i, kvh, p_buf.at[step % 2], r_buf.at[step % 2], vtp_ref, vtc_ref, z_buf, og_buf)

    norm_slices = [(c, j) for c in range(n_chunks) for j in range(steps_per_chunk)]
    for c, j in norm_slices[:2 * steps_per_chunk]:
        norm_slice(c, j)
    for n in range(2 * pieces):
        project_piece(0, n)

    def out_slices(ko):
        co, no = divmod(ko, steps_per_chunk)
        for n in range(no * pieces // steps_per_chunk, (no + 1) * pieces // steps_per_chunk):
            out_piece(co, n)

    out_lag = steps_per_chunk + 2
    for k in range(n_steps + 2):
        c, j = divmod(k, steps_per_chunk)
        if k < n_steps:
            stage_scores(k)
        if 1 <= k <= n_steps:
            stage_softmax(k - 1)
        if 2 <= k <= n_steps + 1:
            stage_pv(k - 2)
        if c + 2 < n_chunks:
            norm_slice(c + 2, j)
        if c + 1 < n_chunks:
            for n in range(j * 2 * pieces // steps_per_chunk, (j + 1) * 2 * pieces // steps_per_chunk):
                project_piece(c + 1, n)
        if k >= out_lag:
            out_slices(k - out_lag)
    for ko in range(n_steps + 2 - out_lag, n_steps):
        out_slices(ko)


def _scores(i, kvh, qt_buf, kp_ref, kc_ref, s_ref):
    rows = slice(i * BLOCK, (i + 1) * BLOCK)
    k_prev = kp_ref[0] if i == 0 else kc_ref[0, (i - 1) * BLOCK:i * BLOCK, :]
    k2 = jnp.concatenate([k_prev, kc_ref[0, rows, :]], axis=0)
    q_cat = jnp.concatenate(
        [qt_buf[i, (kvh * GQA_GROUPS + g) * HEAD_DIM:(kvh * GQA_GROUPS + g + 1) * HEAD_DIM, :]
         for g in range(GQA_GROUPS)], axis=1)
    q_zeros = jnp.zeros((HEAD_DIM, GQA_GROUPS * BLOCK), BF16)
    q_sel = jnp.concatenate([q_cat, q_zeros] if kvh == 0 else [q_zeros, q_cat], axis=0)
    s = jnp.dot(k2, q_sel, preferred_element_type=F32)
    for g in range(GQA_GROUPS):
        s_ref[g] = s[:, g * BLOCK:(g + 1) * BLOCK]


def _softmax(i, kvh, t, s_ref, code_ref, bias_ref, sink_ref, p_ref, r_ref):
    limit = jnp.where(t == 0, 1.0, 2.0) if i == 0 else 2.0
    for g in range(GQA_GROUPS):
        sink = sink_ref[kvh, :, g * BLOCK:(g + 1) * BLOCK]
        sg = jnp.where(code_ref[...] < limit, s_ref[g] + bias_ref[kvh, g], NEG_INF)
        m = jnp.maximum(jnp.max(sg, axis=0, keepdims=True), sink)
        e = jnp.exp2(sg - m)
        den = jnp.sum(e, axis=0, keepdims=True) + jnp.exp2(sink - m)
        p_ref[g] = e.astype(BF16)
        r_ref[g] = 1.0 / den


def _pv_gate(i, kvh, p_ref, r_ref, vtp_ref, vtc_ref, z_buf, og_buf):
    rows = slice(i * BLOCK, (i + 1) * BLOCK)
    vt_prev = vtp_ref[0] if i == 0 else vtc_ref[0, :, (i - 1) * BLOCK:i * BLOCK]
    vt2 = jnp.concatenate([vt_prev, vtc_ref[0, :, rows]], axis=1)
    vt = vt2[kvh * HEAD_DIM:(kvh + 1) * HEAD_DIM, :]
    pt = jnp.concatenate([p_ref[g] for g in range(GQA_GROUPS)], axis=1)
    ot = jnp.dot(vt, pt, preferred_element_type=F32)
    for p in range(GQA_GROUPS // 2):
        pair_t = jnp.concatenate(
            [ot[:, g * BLOCK:(g + 1) * BLOCK] * r_ref[g] for g in (2 * p, 2 * p + 1)], axis=0)
        slab = kvh * (GQA_GROUPS // 2) + p
        z = z_buf[slab, rows, :]
        og_buf[slab, rows, :] = (pair_t.T * (z * _sigmoid(z))).astype(BF16)


def _attn_weights_kernel(wq_ref, wzin_ref, wqt_ref, wz_ref):
    wqt_ref[0] = wq_ref[0].T.astype(BF16)
    wz_ref[0] = wzin_ref[0].astype(BF16)


def _attn_weights(b_w_in):
    n_b, d, _ = b_w_in.shape
    blocks = B_WIDTH // PROJ_PIECE
    return pl.pallas_call(
        _attn_weights_kernel,
        grid=(n_b, blocks),
        in_specs=[
            pl.BlockSpec((1, d, PROJ_PIECE), lambda l, j: (l, 0, j)),
            pl.BlockSpec((1, d, PROJ_PIECE), lambda l, j: (l, 0, blocks + j)),
        ],
        out_specs=[
            pl.BlockSpec((1, PROJ_PIECE, d), lambda l, j: (l, j, 0)),
            pl.BlockSpec((1, d, PROJ_PIECE), lambda l, j: (l, 0, j)),
        ],
        out_shape=[jax.ShapeDtypeStruct((n_b, B_WIDTH, d), BF16),
                   jax.ShapeDtypeStruct((n_b, d, B_WIDTH), BF16)],
        compiler_params=pltpu.CompilerParams(
            dimension_semantics=("arbitrary", "arbitrary"), vmem_limit_bytes=VMEM_LIMIT_BYTES),
        name="attn_weights",
    )(b_w_in, b_w_in)


def _attn_layer(layer, x, shift, scale, gate, norm_g, w_q_t, w_z, k, v_t, bias, codes, sinks, w_out,
                final_g, final_norm):
    b, s, d = x.shape
    ts = ATTN_TILE
    blocks_per_tile = ts // BLOCK
    vec_spec = pl.BlockSpec((1, 1, d), lambda i, j: (i, 0, 0))
    score_shape = (2, GQA_GROUPS, 2 * BLOCK, BLOCK)

    def prev_block(j):
        return jnp.maximum(j * blocks_per_tile - 1, 0)

    return pl.pallas_call(
        functools.partial(_attn_kernel, final_norm=final_norm),
        grid=(b, s // ts),
        in_specs=[
            pl.BlockSpec((1, ts, d), lambda i, j: (i, j, 0)),
            vec_spec, vec_spec, vec_spec,
            _const_spec((1, d)),
            pl.BlockSpec((1, B_WIDTH, d), lambda i, j: (layer, 0, 0), pipeline_mode=pl.Buffered(1)),
            pl.BlockSpec((1, d, B_WIDTH), lambda i, j: (layer, 0, 0), pipeline_mode=pl.Buffered(1)),
            pl.BlockSpec((1, BLOCK, KV_WIDTH), lambda i, j: (i, prev_block(j), 0)),
            pl.BlockSpec((1, ts, KV_WIDTH), lambda i, j: (i, j, 0)),
            pl.BlockSpec((1, KV_WIDTH, BLOCK), lambda i, j: (i, 0, prev_block(j))),
            pl.BlockSpec((1, KV_WIDTH, ts), lambda i, j: (i, 0, j)),
            _const_spec((N_KV_HEADS, GQA_GROUPS, 2 * BLOCK, BLOCK)),
            _const_spec((2 * BLOCK, BLOCK)),
            _const_spec((N_KV_HEADS, 1, GQA_GROUPS * BLOCK)),
            _const_spec((B_WIDTH, d)),
            _const_spec((1, d)),
        ],
        out_specs=pl.BlockSpec((1, ts, d), lambda i, j: (i, j, 0)),
        out_shape=jax.ShapeDtypeStruct((b, s, d), F32),
        scratch_shapes=[
            pltpu.VMEM((ts, d), BF16),
            pltpu.VMEM((blocks_per_tile, B_WIDTH, BLOCK), BF16),
            pltpu.VMEM((B_WIDTH // LANES, ts, LANES), F32),
            pltpu.VMEM((B_WIDTH // LANES, ts, LANES), BF16),
            pltpu.VMEM(score_shape, F32),
            pltpu.VMEM(score_shape, BF16),
            pltpu.VMEM((2, GQA_GROUPS, 1, BLOCK), F32),
            pltpu.VMEM((2, d // PROJ_PIECE, ATTN_CHUNK, PROJ_PIECE), F32),
        ],
        compiler_params=pltpu.CompilerParams(
            dimension_semantics=("arbitrary", "arbitrary"), vmem_limit_bytes=VMEM_LIMIT_BYTES),
        name="attn_layer",
    )(x, shift, scale, gate, norm_g, w_q_t, w_z, k, k, v_t, v_t, bias, codes, sinks, w_out, final_g)


def kernel(x, c, norm_g, ada_w, ada_b, a_w_in, a_w_group, a_scale, a_w_out, kv_norm_g, kv_ada_w,
           kv_ada_b, w_kv, b_w_in, b_sinks, b_w_out, rel_bias, final_g):
    b, s, d = x.shape
    depth = norm_g.shape[0]
    n_a = a_w_in.shape[0]
    assert (d, s % SEQ_TILE, s % ATTN_TILE) == (D_MODEL, 0, 0) and b <= SUBLANES

    c_pad = jnp.pad(c, ((0, SUBLANES - b), (0, 0)))
    mod = _modulation(c_pad, ada_w, ada_b.reshape(depth, 1, 3 * d))[:, :b]
    mod_kv = _modulation(c_pad, kv_ada_w[None], kv_ada_b.reshape(1, 1, 2 * d))[0, :b]

    def vec(m, k):
        return m[:, k * d:(k + 1) * d].reshape(b, 1, d)

    w_v, w_z = _fuse_group_weights(a_w_in, a_w_group, a_scale)
    kv_args = (vec(mod_kv, 0), vec(mod_kv, 1), kv_norm_g.reshape(1, d),
               w_kv[:, :KV_WIDTH].astype(BF16), w_kv[:, KV_WIDTH:].T.astype(BF16))
    for l in range(n_a):
        res = _pool_layer(l, x, vec(mod[l], 0), vec(mod[l], 1), vec(mod[l], 2), norm_g[l].reshape(1, d),
                          w_v, w_z, a_w_out[l].astype(BF16),
                          kv=kv_args if l == n_a - 1 else None)
        x = res if l < n_a - 1 else res[0]
    k, v_t = res[1], res[2]
    bias = _position_bias_t(rel_bias)
    codes = jnp.asarray(_band_codes_t())
    w_q_t, w_qz = _attn_weights(b_w_in)
    for l in range(n_a, depth):
        j = l - n_a
        sinks = jnp.repeat(b_sinks[j] * LOG2E, BLOCK).reshape(N_KV_HEADS, 1, GQA_GROUPS * BLOCK)
        x = _attn_layer(j, x, vec(mod[l], 0), vec(mod[l], 1), vec(mod[l], 2), norm_g[l].reshape(1, d),
                        w_q_t, w_qz, k, v_t, bias, codes, sinks, b_w_out[j].astype(BF16),
                        final_g.reshape(1, d), final_norm=(l == depth - 1))
    return x
```

```python
import functools
import math

import numpy as np
import jax
import jax.numpy as jnp
from jax import lax
from jax.experimental import pallas as pl
from jax.experimental.pallas import tpu as pltpu

F32 = jnp.float32
BF16 = jnp.bfloat16

D_MODEL = 1024
A_WIDTH = 2048
POOL_WINDOWS = (2, 4, 8, 16)
N_GROUPS = len(POOL_WINDOWS)
GROUP_WIDTH = A_WIDTH // N_GROUPS
HEAD_DIM = 64
N_HEADS = 16
N_KV_HEADS = 2
GQA_GROUPS = N_HEADS // N_KV_HEADS
KV_WIDTH = N_KV_HEADS * HEAD_DIM
B_WIDTH = N_HEADS * HEAD_DIM
BLOCK = 128
N_BUCKETS = 32
MAX_DISTANCE = 128
EPS = 1e-6
NEG_INF = -1e30
LOG2E = math.log2(math.e)

LANES = 128
SUBLANES = 8
VMEM_LIMIT_BYTES = 56 * 1024 * 1024

MAX_HALO = 16
SEQ_TILE = 1024
ATTN_TILE = 1024
ATTN_CHUNK = 256
PROJ_PIECE = 256
SUM_ROWS = 16
KV_CHUNK = 256
NORM_ROWS = 32
POOL_ROWS = 64

_NT_DIMS = (((1,), (1,)), ((), ()))


def _sigmoid(v):
    return 1.0 / (1.0 + jnp.exp(-v))


def _const_spec(shape):
    zeros = (0,) * len(shape)
    return pl.BlockSpec(shape, lambda *_: zeros, pipeline_mode=pl.Buffered(1))


def _mod_kernel(c_ref, w_ref, b_ref, o_ref):
    c = c_ref[...]
    c_act = c * _sigmoid(c)
    o_ref[0] = jnp.dot(c_act, w_ref[0], preferred_element_type=F32) + b_ref[0]


def _modulation(c_pad, w, b):
    n_layers, d, n = w.shape
    rows = c_pad.shape[0]
    nt = D_MODEL
    return pl.pallas_call(
        _mod_kernel,
        grid=(n_layers, n // nt),
        in_specs=[
            pl.BlockSpec((rows, d), lambda l, j: (0, 0)),
            pl.BlockSpec((1, d, nt), lambda l, j: (l, 0, j)),
            pl.BlockSpec((1, 1, nt), lambda l, j: (l, 0, j)),
        ],
        out_specs=pl.BlockSpec((1, rows, nt), lambda l, j: (l, 0, j)),
        out_shape=jax.ShapeDtypeStruct((n_layers, rows, n), F32),
        compiler_params=pltpu.CompilerParams(
            dimension_semantics=("arbitrary", "arbitrary"), vmem_limit_bytes=VMEM_LIMIT_BYTES),
        name="adaln_modulation",
    )(c_pad, w, b)


def _norm_modulate(x_ref, g_ref, shift_ref, scale_ref, h_ref, rows, start=0):
    gain = g_ref[...] * (1.0 + scale_ref[0])
    shift = shift_ref[0]
    for r in range(start, start + rows, NORM_ROWS):
        xv = x_ref[0, r:r + NORM_ROWS, :]
        ms = jnp.mean(xv * xv, axis=-1, keepdims=True)
        xn = xv * lax.rsqrt(ms + EPS)
        h_ref[r:r + NORM_ROWS, :] = (xn * gain + shift).astype(BF16)


def _split_bf16(a):
    hi = a.astype(BF16)
    lo = (a - hi.astype(F32)).astype(BF16)
    return hi, lo


def _fuse_kernel(wu_ref, wzin_ref, wg_ref, asc_ref, wv_ref, wz_ref):
    a_hi, a_lo = _split_bf16(wu_ref[0])
    b_hi, b_lo = _split_bf16(wg_ref[0, 0])
    acc = jnp.dot(a_hi, b_hi, preferred_element_type=F32)
    acc = acc + jnp.dot(a_hi, b_lo, preferred_element_type=F32)
    acc = acc + jnp.dot(a_lo, b_hi, preferred_element_type=F32)
    wv_ref[0] = (acc * asc_ref[0]).astype(BF16)
    wz_ref[0] = wzin_ref[0].astype(BF16)


def _fuse_group_weights(a_w_in, a_w_group, a_scale):
    n_a, d, _ = a_w_in.shape
    col_block = pl.BlockSpec((1, d, GROUP_WIDTH), lambda l, g: (l, 0, g))
    return pl.pallas_call(
        _fuse_kernel,
        grid=(n_a, N_GROUPS),
        in_specs=[
            col_block,
            pl.BlockSpec((1, d, GROUP_WIDTH), lambda l, g: (l, 0, N_GROUPS + g)),
            pl.BlockSpec((1, 1, GROUP_WIDTH, GROUP_WIDTH), lambda l, g: (l, g, 0, 0)),
            pl.BlockSpec((1, 1, GROUP_WIDTH), lambda l, g: (l, 0, g)),
        ],
        out_specs=[col_block, col_block],
        out_shape=[jax.ShapeDtypeStruct((n_a, d, A_WIDTH), BF16)] * 2,
        compiler_params=pltpu.CompilerParams(
            dimension_semantics=("arbitrary", "arbitrary"), vmem_limit_bytes=VMEM_LIMIT_BYTES),
        name="fuse_group_weights",
    )(a_w_in, a_w_in, a_w_group, a_scale.reshape(n_a, 1, A_WIDTH))


def _pool_kernel(x_ref, shift_ref, scale_ref, gate_ref, g_ref, wv_ref, wz_ref, wout_ref, *rest, emit_kv):
    if emit_kv:
        kshift_ref, kscale_ref, kg_ref, wk_ref, wvt_ref, o_ref, k_ref, vt_ref = rest[:8]
        hk_buf = rest[-1]
        rest = rest[:-1]
    else:
        o_ref = rest[0]
    h_buf, v_ext, z_buf, gated_buf, carry = rest[-5:]
    t = pl.program_id(1)
    ts = SEQ_TILE

    @pl.when(t == 0)
    def _():
        carry[...] = jnp.zeros_like(carry)

    _norm_modulate(x_ref, g_ref, shift_ref, scale_ref, h_buf, ts)
    h = h_buf[...]

    for g, w in enumerate(POOL_WINDOWS):
        cols = slice(g * GROUP_WIDTH, (g + 1) * GROUP_WIDTH)
        v_ext[g, 0:MAX_HALO, :] = carry[g]
        v_ext[g, MAX_HALO:, :] = jnp.dot(h, wv_ref[0, :, cols], preferred_element_type=F32)
        carry[g] = v_ext[g, ts:ts + MAX_HALO, :]
        z_buf[g] = jnp.dot(h, wz_ref[0, :, cols], preferred_element_type=F32)

        halo = MAX_HALO if w > SUBLANES else SUBLANES
        for r in range(0, ts, POOL_ROWS):
            ev = v_ext[g, MAX_HALO + r - halo:MAX_HALO + r + POOL_ROWS, :]
            s = ev
            k = 1
            while k < w:
                s = s + pltpu.roll(s, k, 0)
                k *= 2
            s = s[halo:]
            if r == 0:
                pos = t * ts + lax.broadcasted_iota(jnp.int32, (POOL_ROWS, GROUP_WIDTH), 0)
                mean = s * (1.0 / jnp.minimum(pos + 1, w).astype(F32))
            else:
                mean = s * (1.0 / w)
            z = z_buf[g, r:r + POOL_ROWS, :]
            gated_buf[r:r + POOL_ROWS, cols] = ((mean - ev[halo:]) * (z * _sigmoid(z))).astype(BF16)

    if not emit_kv:
        y = jnp.dot(gated_buf[...], wout_ref[...], preferred_element_type=F32)
        o_ref[0] = x_ref[0] + gate_ref[0] * y
        return

    def out_chunk(r):
        rows = slice(r, r + KV_CHUNK)
        y = jnp.dot(gated_buf[rows, :], wout_ref[...], preferred_element_type=F32)
        o_ref[0, rows, :] = x_ref[0, rows, :] + gate_ref[0] * y

    def kv_chunk(r):
        rows = slice(r, r + KV_CHUNK)
        _norm_modulate(o_ref, kg_ref, kshift_ref, kscale_ref, hk_buf, KV_CHUNK, start=r)
        hk = hk_buf[rows, :]
        k_ref[0, rows, :] = jnp.dot(hk, wk_ref[...], preferred_element_type=F32).astype(BF16)
        vt_ref[0, :, rows] = lax.dot_general(wvt_ref[...], hk, _NT_DIMS,
                                             preferred_element_type=F32).astype(BF16)

    starts = list(range(0, ts, KV_CHUNK))
    out_chunk(starts[0])
    for prev, cur in zip(starts[:-1], starts[1:]):
        out_chunk(cur)
        kv_chunk(prev)
    kv_chunk(starts[-1])


def _pool_layer(layer, x, shift, scale, gate, norm_g, w_v, w_z, w_out, kv=None):
    b, s, d = x.shape
    ts = SEQ_TILE
    vec_spec = pl.BlockSpec((1, 1, d), lambda i, j: (i, 0, 0))
    x_spec = pl.BlockSpec((1, ts, d), lambda i, j: (i, j, 0))
    layer_weights = pl.BlockSpec((1, d, A_WIDTH), lambda i, j: (layer, 0, 0), pipeline_mode=pl.Buffered(1))
    in_specs = [
        x_spec,
        vec_spec, vec_spec, vec_spec,
        _const_spec((1, d)),
        layer_weights,
        layer_weights,
        _const_spec((A_WIDTH, d)),
    ]
    out_specs = x_spec
    out_shape = jax.ShapeDtypeStruct((b, s, d), F32)
    args = (x, shift, scale, gate, norm_g, w_v, w_z, w_out)
    if kv is not None:
        in_specs += [vec_spec, vec_spec, _const_spec((1, d)), _const_spec((d, KV_WIDTH)),
                     _const_spec((KV_WIDTH, d))]
        out_specs = [x_spec,
                     pl.BlockSpec((1, ts, KV_WIDTH), lambda i, j: (i, j, 0)),
                     pl.BlockSpec((1, KV_WIDTH, ts), lambda i, j: (i, 0, j))]
        out_shape = [out_shape,
                     jax.ShapeDtypeStruct((b, s, KV_WIDTH), BF16),
                     jax.ShapeDtypeStruct((b, KV_WIDTH, s), BF16)]
        args += tuple(kv)
    scratch_shapes = [
        pltpu.VMEM((ts, d), BF16),
        pltpu.VMEM((N_GROUPS, ts + MAX_HALO, GROUP_WIDTH), F32),
        pltpu.VMEM((N_GROUPS, ts, GROUP_WIDTH), F32),
        pltpu.VMEM((ts, A_WIDTH), BF16),
        pltpu.VMEM((N_GROUPS, MAX_HALO, GROUP_WIDTH), F32),
    ]
    if kv is not None:
        scratch_shapes.append(pltpu.VMEM((ts, d), BF16))
    return pl.pallas_call(
        functools.partial(_pool_kernel, emit_kv=kv is not None),
        grid=(b, s // ts),
        in_specs=in_specs,
        out_specs=out_specs,
        out_shape=out_shape,
        scratch_shapes=scratch_shapes,
        compiler_params=pltpu.CompilerParams(
            dimension_semantics=("arbitrary", "arbitrary"), vmem_limit_bytes=VMEM_LIMIT_BYTES),
        name="pool_layer",
    )(*args)


def _t5_causal_buckets():
    i = np.arange(BLOCK)[:, None]
    j = np.arange(2 * BLOCK)[None, :]
    n = np.maximum(i + BLOCK - j, 0)
    max_exact = N_BUCKETS // 2
    large = max_exact + (np.log(np.maximum(n, 1) / max_exact) / math.log(MAX_DISTANCE / max_exact)
                         * (N_BUCKETS - max_exact)).astype(np.int32)
    large = np.minimum(large, N_BUCKETS - 1)
    return np.where(n < max_exact, n, large).astype(np.int32)


def _band_codes_t():
    i = np.arange(BLOCK)[None, :]
    j = np.arange(2 * BLOCK)[:, None]
    rel = i + BLOCK - j
    band = (rel >= 0) & (rel < BLOCK)
    return np.where(band, np.where(j < BLOCK, 1.0, 0.0), 2.0).astype(np.float32)


def _bias_kernel(bucket_ref, rel_ref, o_ref):
    bucket = bucket_ref[...]
    for head in range(N_HEADS):
        acc = jnp.zeros((2 * BLOCK, BLOCK), F32)
        for b in range(N_BUCKETS):
            acc = jnp.where(bucket == b, rel_ref[b, head], acc)
        kv_head, g = divmod(head, GQA_GROUPS)
        o_ref[kv_head, g] = acc * LOG2E


def _position_bias_t(rel_bias):
    buckets_t = jnp.asarray(np.ascontiguousarray(_t5_causal_buckets().T))
    return pl.pallas_call(
        _bias_kernel,
        in_specs=[
            pl.BlockSpec(memory_space=pltpu.VMEM),
            pl.BlockSpec(memory_space=pltpu.SMEM),
        ],
        out_specs=pl.BlockSpec(memory_space=pltpu.VMEM),
        out_shape=jax.ShapeDtypeStruct((N_KV_HEADS, GQA_GROUPS, 2 * BLOCK, BLOCK), F32),
        name="position_bias",
    )(buckets_t, rel_bias)


def _attn_kernel(x_ref, shift_ref, scale_ref, gate_ref, g_ref, wqt_ref, wz_ref, kp_ref, kc_ref,
                 vtp_ref, vtc_ref, bias_ref, code_ref, sink_ref, wout_ref, fg_ref, o_ref,
                 h_buf, qt_buf, z_buf, og_buf, s_buf, p_buf, r_buf, y_buf, *, final_norm):
    t = pl.program_id(1)
    n_chunks = ATTN_TILE // ATTN_CHUNK
    pieces = B_WIDTH // PROJ_PIECE
    steps_per_chunk = (ATTN_CHUNK // BLOCK) * N_KV_HEADS
    n_steps = n_chunks * steps_per_chunk
    norm_rows = ATTN_CHUNK // steps_per_chunk

    def norm_slice(c, j):
        _norm_modulate(x_ref, g_ref, shift_ref, scale_ref, h_buf, norm_rows,
                       start=c * ATTN_CHUNK + j * norm_rows)

    h_chunks = {}

    def project_piece(c, n):
        chunk = slice(c * ATTN_CHUNK, (c + 1) * ATTN_CHUNK)
        if c not in h_chunks:
            h_chunks[c] = h_buf[chunk, :]
        h = h_chunks[c]
        if n < pieces:
            sl = slice(n * PROJ_PIECE, (n + 1) * PROJ_PIECE)
            qt = lax.dot_general(wqt_ref[0, sl, :], h, _NT_DIMS, preferred_element_type=F32)
            qt = qt.astype(BF16)
            for blk in range(ATTN_CHUNK // BLOCK):
                qt_buf[c * (ATTN_CHUNK // BLOCK) + blk, sl, :] = qt[:, blk * BLOCK:(blk + 1) * BLOCK]
        else:
            sl = slice((n - pieces) * PROJ_PIECE, (n - pieces + 1) * PROJ_PIECE)
            z = jnp.dot(h, wz_ref[0, :, sl], preferred_element_type=F32)
            for sub in range(PROJ_PIECE // LANES):
                z_buf[(n - pieces) * (PROJ_PIECE // LANES) + sub, chunk, :] = z[:, sub * LANES:(sub + 1) * LANES]

    def out_piece(c, n):
        chunk = slice(c * ATTN_CHUNK, (c + 1) * ATTN_CHUNK)
        sl = slice(n * PROJ_PIECE, (n + 1) * PROJ_PIECE)
        og = jnp.concatenate([og_buf[sb, chunk, :] for sb in range(B_WIDTH // LANES)], axis=1)
        y_buf[c % 2, n] = jnp.dot(og, wout_ref[:, sl], preferred_element_type=F32)
        if n == pieces - 1:
            for r in range(0, ATTN_CHUNK, NORM_ROWS):
                rows = slice(c * ATTN_CHUNK + r, c * ATTN_CHUNK + r + NORM_ROWS)
                y = jnp.concatenate([y_buf[c % 2, m, r:r + NORM_ROWS, :] for m in range(pieces)], axis=1)
                out = x_ref[0, rows, :] + gate_ref[0] * y
                if final_norm:
                    ms = jnp.mean(out * out, axis=-1, keepdims=True)
                    out = (out * lax.rsqrt(ms + EPS)) * fg_ref[...]
                o_ref[0, rows, :] = out

    def stage_scores(step):
        i, kvh = divmod(step, N_KV_HEADS)
        _scores(i, kvh, qt_buf, kp_ref, kc_ref, s_buf.at[step % 2])

    def stage_softmax(step):
        i, kvh = divmod(step, N_KV_HEADS)
        _softmax(i, kvh, t, s_buf.at[step % 2], code_ref, bias_ref, sink_ref,
                 p_buf.at[step % 2], r_buf.at[step % 2])

    def stage_pv(step):
        i, kvh = divmod(step, N_KV_HEADS)
        _pv_gate(i, kvh, p_buf.at[step % 2], r_buf.at[step % 2], vtp_ref, vtc_ref, z_buf, og_buf)

    norm_slices = [(c, j) for c in range(n_chunks) for j in range(steps_per_chunk)]
    for c, j in norm_slices[:2 * steps_per_chunk]:
        norm_slice(c, j)
    for n in range(2 * pieces):
        project_piece(0, n)

    def out_slices(ko):
        co, no = divmod(ko, steps_per_chunk)
        for n in range(no * pieces // steps_per_chunk, (no + 1) * pieces // steps_per_chunk):
            out_piece(co, n)

    out_lag = steps_per_chunk + 2
    for k in range(n_steps + 2):
        c, j = divmod(k, steps_per_chunk)
        if k < n_steps:
            stage_scores(k)
        if 1 <= k <= n_steps:
            stage_softmax(k - 1)
        if 2 <= k <= n_steps + 1:
            stage_pv(k - 2)
        if c + 2 < n_chunks:
            norm_slice(c + 2, j)
        if c + 1 < n_chunks:
            for n in range(j * 2 * pieces // steps_per_chunk, (j + 1) * 2 * pieces // steps_per_chunk):
                project_piece(c + 1, n)
        if k >= out_lag:
            out_slices(k - out_lag)
    for ko in range(n_steps + 2 - out_lag, n_steps):
        out_slices(ko)


def _scores(i, kvh, qt_buf, kp_ref, kc_ref, s_ref):
    rows = slice(i * BLOCK, (i + 1) * BLOCK)
    k_prev = kp_ref[0] if i == 0 else kc_ref[0, (i - 1) * BLOCK:i * BLOCK, :]
    k2 = jnp.concatenate([k_prev, kc_ref[0, rows, :]], axis=0)
    q_cat = jnp.concatenate(
        [qt_buf[i, (kvh * GQA_GROUPS + g) * HEAD_DIM:(kvh * GQA_GROUPS + g + 1) * HEAD_DIM, :]
         for g in range(GQA_GROUPS)], axis=1)
    q_zeros = jnp.zeros((HEAD_DIM, GQA_GROUPS * BLOCK), BF16)
    q_sel = jnp.concatenate([q_cat, q_zeros] if kvh == 0 else [q_zeros, q_cat], axis=0)
    s = jnp.dot(k2, q_sel, preferred_element_type=F32)
    for g in range(GQA_GROUPS):
        s_ref[g] = s[:, g * BLOCK:(g + 1) * BLOCK]


def _softmax(i, kvh, t, s_ref, code_ref, bias_ref, sink_ref, p_ref, r_ref):
    limit = jnp.where(t == 0, 1.0, 2.0) if i == 0 else 2.0
    for g in range(GQA_GROUPS):
        sink = sink_ref[kvh, :, g * BLOCK:(g + 1) * BLOCK]
        sg = jnp.where(code_ref[...] < limit, s_ref[g] + bias_ref[kvh, g], NEG_INF)
        m = jnp.maximum(jnp.max(sg, axis=0, keepdims=True), sink)
        p_ref[g] = jnp.exp2(sg - m).astype(BF16)
        r_ref[g] = jnp.exp2(sink - m)


def _pv_gate(i, kvh, p_ref, r_ref, vtp_ref, vtc_ref, z_buf, og_buf):
    rows = slice(i * BLOCK, (i + 1) * BLOCK)
    vt_prev = vtp_ref[0] if i == 0 else vtc_ref[0, :, (i - 1) * BLOCK:i * BLOCK]
    vt2 = jnp.concatenate([vt_prev, vtc_ref[0, :, rows]], axis=1)
    ones_rows = (lax.broadcasted_iota(jnp.int32, (SUM_ROWS, 2 * BLOCK), 0) == 0).astype(BF16)
    vt = jnp.concatenate([vt2[kvh * HEAD_DIM:(kvh + 1) * HEAD_DIM, :], ones_rows], axis=0)
    pt = jnp.concatenate([p_ref[g] for g in range(GQA_GROUPS)], axis=1)
    ot = jnp.dot(vt, pt, preferred_element_type=F32)
    inv_den = [1.0 / (ot[HEAD_DIM:HEAD_DIM + 1, g * BLOCK:(g + 1) * BLOCK] + r_ref[g])
               for g in range(GQA_GROUPS)]
    for p in range(GQA_GROUPS // 2):
        pair_t = jnp.concatenate(
            [ot[:HEAD_DIM, g * BLOCK:(g + 1) * BLOCK] * inv_den[g] for g in (2 * p, 2 * p + 1)], axis=0)
        slab = kvh * (GQA_GROUPS // 2) + p
        z = z_buf[slab, rows, :]
        og_buf[slab, rows, :] = (pair_t.T * (z * _sigmoid(z))).astype(BF16)


def _attn_weights_kernel(wq_ref, wzin_ref, wqt_ref, wz_ref):
    wqt_ref[0] = (wq_ref[0].T * (HEAD_DIM ** -0.5 * LOG2E)).astype(BF16)
    wz_ref[0] = wzin_ref[0].astype(BF16)


def _attn_weights(b_w_in):
    n_b, d, _ = b_w_in.shape
    blocks = B_WIDTH // PROJ_PIECE
    return pl.pallas_call(
        _attn_weights_kernel,
        grid=(n_b, blocks),
        in_specs=[
            pl.BlockSpec((1, d, PROJ_PIECE), lambda l, j: (l, 0, j)),
            pl.BlockSpec((1, d, PROJ_PIECE), lambda l, j: (l, 0, blocks + j)),
        ],
        out_specs=[
            pl.BlockSpec((1, PROJ_PIECE, d), lambda l, j: (l, j, 0)),
            pl.BlockSpec((1, d, PROJ_PIECE), lambda l, j: (l, 0, j)),
        ],
        out_shape=[jax.ShapeDtypeStruct((n_b, B_WIDTH, d), BF16),
                   jax.ShapeDtypeStruct((n_b, d, B_WIDTH), BF16)],
        compiler_params=pltpu.CompilerParams(
            dimension_semantics=("arbitrary", "arbitrary"), vmem_limit_bytes=VMEM_LIMIT_BYTES),
        name="attn_weights",
    )(b_w_in, b_w_in)


def _attn_layer(layer, x, shift, scale, gate, norm_g, w_q_t, w_z, k, v_t, bias, codes, sinks, w_out,
                final_g, final_norm):
    b, s, d = x.shape
    ts = ATTN_TILE
    blocks_per_tile = ts // BLOCK
    vec_spec = pl.BlockSpec((1, 1, d), lambda i, j: (i, 0, 0))
    score_shape = (2, GQA_GROUPS, 2 * BLOCK, BLOCK)

    def prev_block(j):
        return jnp.maximum(j * blocks_per_tile - 1, 0)

    return pl.pallas_call(
        functools.partial(_attn_kernel, final_norm=final_norm),
        grid=(b, s // ts),
        in_specs=[
            pl.BlockSpec((1, ts, d), lambda i, j: (i, j, 0)),
            vec_spec, vec_spec, vec_spec,
            _const_spec((1, d)),
            pl.BlockSpec((1, B_WIDTH, d), lambda i, j: (layer, 0, 0), pipeline_mode=pl.Buffered(1)),
            pl.BlockSpec((1, d, B_WIDTH), lambda i, j: (layer, 0, 0), pipeline_mode=pl.Buffered(1)),
            pl.BlockSpec((1, BLOCK, KV_WIDTH), lambda i, j: (i, prev_block(j), 0)),
            pl.BlockSpec((1, ts, KV_WIDTH), lambda i, j: (i, j, 0)),
            pl.BlockSpec((1, KV_WIDTH, BLOCK), lambda i, j: (i, 0, prev_block(j))),
            pl.BlockSpec((1, KV_WIDTH, ts), lambda i, j: (i, 0, j)),
            _const_spec((N_KV_HEADS, GQA_GROUPS, 2 * BLOCK, BLOCK)),
            _const_spec((2 * BLOCK, BLOCK)),
            _const_spec((N_KV_HEADS, 1, GQA_GROUPS * BLOCK)),
            _const_spec((B_WIDTH, d)),
            _const_spec((1, d)),
        ],
        out_specs=pl.BlockSpec((1, ts, d), lambda i, j: (i, j, 0)),
        out_shape=jax.ShapeDtypeStruct((b, s, d), F32),
        scratch_shapes=[
            pltpu.VMEM((ts, d), BF16),
            pltpu.VMEM((blocks_per_tile, B_WIDTH, BLOCK), BF16),
            pltpu.VMEM((B_WIDTH // LANES, ts, LANES), F32),
            pltpu.VMEM((B_WIDTH // LANES, ts, LANES), BF16),
            pltpu.VMEM(score_shape, F32),
            pltpu.VMEM(score_shape, BF16),
            pltpu.VMEM((2, GQA_GROUPS, 1, BLOCK), F32),
            pltpu.VMEM((2, d // PROJ_PIECE, ATTN_CHUNK, PROJ_PIECE), F32),
        ],
        compiler_params=pltpu.CompilerParams(
            dimension_semantics=("arbitrary", "arbitrary"), vmem_limit_bytes=VMEM_LIMIT_BYTES),
        name="attn_layer",
    )(x, shift, scale, gate, norm_g, w_q_t, w_z, k, k, v_t, v_t, bias, codes, sinks, w_out, final_g)


def kernel(x, c, norm_g, ada_w, ada_b, a_w_in, a_w_group, a_scale, a_w_out, kv_norm_g, kv_ada_w,
           kv_ada_b, w_kv, b_w_in, b_sinks, b_w_out, rel_bias, final_g):
    b, s, d = x.shape
    depth = norm_g.shape[0]
    n_a = a_w_in.shape[0]
    assert (d, s % SEQ_TILE, s % ATTN_TILE) == (D_MODEL, 0, 0) and b <= SUBLANES

    c_pad = jnp.pad(c, ((0, SUBLANES - b), (0, 0)))
    mod = _modulation(c_pad, ada_w, ada_b.reshape(depth, 1, 3 * d))[:, :b]
    mod_kv = _modulation(c_pad, kv_ada_w[None], kv_ada_b.reshape(1, 1, 2 * d))[0, :b]

    def vec(m, k):
        return m[:, k * d:(k + 1) * d].reshape(b, 1, d)

    w_v, w_z = _fuse_group_weights(a_w_in, a_w_group, a_scale)
    kv_args = (vec(mod_kv, 0), vec(mod_kv, 1), kv_norm_g.reshape(1, d),
               w_kv[:, :KV_WIDTH].astype(BF16), w_kv[:, KV_WIDTH:].T.astype(BF16))
    for l in range(n_a):
        res = _pool_layer(l, x, vec(mod[l], 0), vec(mod[l], 1), vec(mod[l], 2), norm_g[l].reshape(1, d),
                          w_v, w_z, a_w_out[l].astype(BF16),
                          kv=kv_args if l == n_a - 1 else None)
        x = res if l < n_a - 1 else res[0]
    k, v_t = res[1], res[2]
    bias = _position_bias_t(rel_bias)
    codes = jnp.asarray(_band_codes_t())
    w_q_t, w_qz = _attn_weights(b_w_in)
    for l in range(n_a, depth):
        j = l - n_a
        sinks = jnp.repeat(b_sinks[j] * LOG2E, BLOCK).reshape(N_KV_HEADS, 1, GQA_GROUPS * BLOCK)
        x = _attn_layer(j, x, vec(mod[l], 0), vec(mod[l], 1), vec(mod[l], 2), norm_g[l].reshape(1, d),
                        w_q_t, w_qz, k, v_t, bias, codes, sinks, b_w_out[j].astype(BF16),
                        final_g.reshape(1, d), final_norm=(l == depth - 1))
    return x
```

```python
import functools
import math

import numpy as np
import jax
import jax.numpy as jnp
from jax import lax
from jax.experimental import pallas as pl
from jax.experimental.pallas import tpu as pltpu

F32 = jnp.float32
BF16 = jnp.bfloat16

D_MODEL = 1024
A_WIDTH = 2048
POOL_WINDOWS = (2, 4, 8, 16)
N_GROUPS = len(POOL_WINDOWS)
GROUP_WIDTH = A_WIDTH // N_GROUPS
HEAD_DIM = 64
N_HEADS = 16
N_KV_HEADS = 2
GQA_GROUPS = N_HEADS // N_KV_HEADS
KV_WIDTH = N_KV_HEADS * HEAD_DIM
B_WIDTH = N_HEADS * HEAD_DIM
BLOCK = 128
N_BUCKETS = 32
MAX_DISTANCE = 128
EPS = 1e-6
NEG_INF = -1e30
LOG2E = math.log2(math.e)

LANES = 128
SUBLANES = 8
VMEM_LIMIT_BYTES = 56 * 1024 * 1024

MAX_HALO = 16
SEQ_TILE = 1024
ATTN_TILE = 1024
ATTN_CHUNK = 256
PROJ_PIECE = 256
SUM_ROWS = 16
KV_CHUNK = 256
NORM_ROWS = 32
POOL_ROWS = 64

_NT_DIMS = (((1,), (1,)), ((), ()))


def _sigmoid(v):
    return 1.0 / (1.0 + jnp.exp(-v))


def _const_spec(shape):
    zeros = (0,) * len(shape)
    return pl.BlockSpec(shape, lambda *_: zeros, pipeline_mode=pl.Buffered(1))


def _mod_kernel(c_ref, w_ref, b_ref, o_ref):
    c = c_ref[...]
    c_act = c * _sigmoid(c)
    o_ref[0] = jnp.dot(c_act, w_ref[0], preferred_element_type=F32) + b_ref[0]


def _modulation(c_pad, w, b):
    n_layers, d, n = w.shape
    rows = c_pad.shape[0]
    nt = D_MODEL
    return pl.pallas_call(
        _mod_kernel,
        grid=(n_layers, n // nt),
        in_specs=[
            pl.BlockSpec((rows, d), lambda l, j: (0, 0)),
            pl.BlockSpec((1, d, nt), lambda l, j: (l, 0, j)),
            pl.BlockSpec((1, 1, nt), lambda l, j: (l, 0, j)),
        ],
        out_specs=pl.BlockSpec((1, rows, nt), lambda l, j: (l, 0, j)),
        out_shape=jax.ShapeDtypeStruct((n_layers, rows, n), F32),
        compiler_params=pltpu.CompilerParams(
            dimension_semantics=("arbitrary", "arbitrary"), vmem_limit_bytes=VMEM_LIMIT_BYTES),
        name="adaln_modulation",
    )(c_pad, w, b)


def _norm_modulate(x_ref, g_ref, shift_ref, scale_ref, h_ref, rows, start=0):
    gain = g_ref[...] * (1.0 + scale_ref[0])
    shift = shift_ref[0]
    for r in range(start, start + rows, NORM_ROWS):
        xv = x_ref[0, r:r + NORM_ROWS, :]
        ms = jnp.mean(xv * xv, axis=-1, keepdims=True)
        xn = xv * lax.rsqrt(ms + EPS)
        h_ref[r:r + NORM_ROWS, :] = (xn * gain + shift).astype(BF16)


def _split_bf16(a):
    hi = a.astype(BF16)
    lo = (a - hi.astype(F32)).astype(BF16)
    return hi, lo


def _fuse_kernel(wu_ref, wzin_ref, wg_ref, asc_ref, wv_ref, wz_ref):
    a_hi, a_lo = _split_bf16(wu_ref[0])
    b_hi, b_lo = _split_bf16(wg_ref[0, 0])
    acc = jnp.dot(a_hi, b_hi, preferred_element_type=F32)
    acc = acc + jnp.dot(a_hi, b_lo, preferred_element_type=F32)
    acc = acc + jnp.dot(a_lo, b_hi, preferred_element_type=F32)
    wv_ref[0] = (acc * asc_ref[0]).astype(BF16)
    wz_ref[0] = wzin_ref[0].astype(BF16)


def _fuse_group_weights(a_w_in, a_w_group, a_scale):
    n_a, d, _ = a_w_in.shape
    col_block = pl.BlockSpec((1, d, GROUP_WIDTH), lambda l, g: (l, 0, g))
    return pl.pallas_call(
        _fuse_kernel,
        grid=(n_a, N_GROUPS),
        in_specs=[
            col_block,
            pl.BlockSpec((1, d, GROUP_WIDTH), lambda l, g: (l, 0, N_GROUPS + g)),
            pl.BlockSpec((1, 1, GROUP_WIDTH, GROUP_WIDTH), lambda l, g: (l, g, 0, 0)),
            pl.BlockSpec((1, 1, GROUP_WIDTH), lambda l, g: (l, 0, g)),
        ],
        out_specs=[col_block, col_block],
        out_shape=[jax.ShapeDtypeStruct((n_a, d, A_WIDTH), BF16)] * 2,
        compiler_params=pltpu.CompilerParams(
            dimension_semantics=("arbitrary", "arbitrary"), vmem_limit_bytes=VMEM_LIMIT_BYTES),
        name="fuse_group_weights",
    )(a_w_in, a_w_in, a_w_group, a_scale.reshape(n_a, 1, A_WIDTH))


def _pool_kernel(x_ref, shift_ref, scale_ref, gate_ref, g_ref, wv_ref, wz_ref, wout_ref, *rest, emit_kv):
    if emit_kv:
        kshift_ref, kscale_ref, kg_ref, wk_ref, wvt_ref, o_ref, k_ref, vt_ref = rest[:8]
        hk_buf = rest[-1]
        rest = rest[:-1]
    else:
        o_ref = rest[0]
    h_buf, v_ext, z_buf, gated_buf, carry = rest[-5:]
    t = pl.program_id(1)
    ts = SEQ_TILE

    @pl.when(t == 0)
    def _():
        carry[...] = jnp.zeros_like(carry)

    _norm_modulate(x_ref, g_ref, shift_ref, scale_ref, h_buf, ts)
    h = h_buf[...]

    for g, w in enumerate(POOL_WINDOWS):
        cols = slice(g * GROUP_WIDTH, (g + 1) * GROUP_WIDTH)
        z_buf[g] = jnp.dot(h, wz_ref[0, :, cols], preferred_element_type=F32)
        v_ext[g, 0:MAX_HALO, :] = carry[g]
        v_ext[g, MAX_HALO:, :] = jnp.dot(h, wv_ref[0, :, cols], preferred_element_type=F32)
        carry[g] = v_ext[g, ts:ts + MAX_HALO, :]

        halo = MAX_HALO if w > SUBLANES else SUBLANES
        for r in range(0, ts, POOL_ROWS):
            ev = v_ext[g, MAX_HALO + r - halo:MAX_HALO + r + POOL_ROWS, :]
            s = ev
            k = 1
            while k < w:
                s = s + pltpu.roll(s, k, 0)
                k *= 2
            s = s[halo:]
            if r == 0:
                pos = t * ts + lax.broadcasted_iota(jnp.int32, (POOL_ROWS, GROUP_WIDTH), 0)
                mean = s * (1.0 / jnp.minimum(pos + 1, w).astype(F32))
            else:
                mean = s * (1.0 / w)
            z = z_buf[g, r:r + POOL_ROWS, :]
            gated_buf[r:r + POOL_ROWS, cols] = ((mean - ev[halo:]) * (z * _sigmoid(z))).astype(BF16)

    if not emit_kv:
        y = jnp.dot(gated_buf[...], wout_ref[...], preferred_element_type=F32)
        o_ref[0] = x_ref[0] + gate_ref[0] * y
        return

    def out_chunk(r):
        rows = slice(r, r + KV_CHUNK)
        y = jnp.dot(gated_buf[rows, :], wout_ref[...], preferred_element_type=F32)
        o_ref[0, rows, :] = x_ref[0, rows, :] + gate_ref[0] * y

    def kv_chunk(r):
        rows = slice(r, r + KV_CHUNK)
        _norm_modulate(o_ref, kg_ref, kshift_ref, kscale_ref, hk_buf, KV_CHUNK, start=r)
        hk = hk_buf[rows, :]
        k_ref[0, rows, :] = jnp.dot(hk, wk_ref[...], preferred_element_type=F32).astype(BF16)
        vt_ref[0, :, rows] = lax.dot_general(wvt_ref[...], hk, _NT_DIMS,
                                             preferred_element_type=F32).astype(BF16)

    starts = list(range(0, ts, KV_CHUNK))
    out_chunk(starts[0])
    for prev, cur in zip(starts[:-1], starts[1:]):
        out_chunk(cur)
        kv_chunk(prev)
    kv_chunk(starts[-1])


def _pool_layer(layer, x, shift, scale, gate, norm_g, w_v, w_z, w_out, kv=None):
    b, s, d = x.shape
    ts = SEQ_TILE
    vec_spec = pl.BlockSpec((1, 1, d), lambda i, j: (i, 0, 0))
    x_spec = pl.BlockSpec((1, ts, d), lambda i, j: (i, j, 0))
    layer_weights = pl.BlockSpec((1, d, A_WIDTH), lambda i, j: (layer, 0, 0), pipeline_mode=pl.Buffered(1))
    in_specs = [
        x_spec,
        vec_spec, vec_spec, vec_spec,
        _const_spec((1, d)),
        layer_weights,
        layer_weights,
        _const_spec((A_WIDTH, d)),
    ]
    out_specs = x_spec
    out_shape = jax.ShapeDtypeStruct((b, s, d), F32)
    args = (x, shift, scale, gate, norm_g, w_v, w_z, w_out)
    if kv is not None:
        in_specs += [vec_spec, vec_spec, _const_spec((1, d)), _const_spec((d, KV_WIDTH)),
                     _const_spec((KV_WIDTH, d))]
        out_specs = [x_spec,
                     pl.BlockSpec((1, ts, KV_WIDTH), lambda i, j: (i, j, 0)),
                     pl.BlockSpec((1, KV_WIDTH, ts), lambda i, j: (i, 0, j))]
        out_shape = [out_shape,
                     jax.ShapeDtypeStruct((b, s, KV_WIDTH), BF16),
                     jax.ShapeDtypeStruct((b, KV_WIDTH, s), BF16)]
        args += tuple(kv)
    scratch_shapes = [
        pltpu.VMEM((ts, d), BF16),
        pltpu.VMEM((N_GROUPS, ts + MAX_HALO, GROUP_WIDTH), F32),
        pltpu.VMEM((N_GROUPS, ts, GROUP_WIDTH), F32),
        pltpu.VMEM((ts, A_WIDTH), BF16),
        pltpu.VMEM((N_GROUPS, MAX_HALO, GROUP_WIDTH), F32),
    ]
    if kv is not None:
        scratch_shapes.append(pltpu.VMEM((ts, d), BF16))
    return pl.pallas_call(
        functools.partial(_pool_kernel, emit_kv=kv is not None),
        grid=(b, s // ts),
        in_specs=in_specs,
        out_specs=out_specs,
        out_shape=out_shape,
        scratch_shapes=scratch_shapes,
        compiler_params=pltpu.CompilerParams(
            dimension_semantics=("arbitrary", "arbitrary"), vmem_limit_bytes=VMEM_LIMIT_BYTES),
        name="pool_layer",
    )(*args)


def _t5_causal_buckets():
    i = np.arange(BLOCK)[:, None]
    j = np.arange(2 * BLOCK)[None, :]
    n = np.maximum(i + BLOCK - j, 0)
    max_exact = N_BUCKETS // 2
    large = max_exact + (np.log(np.maximum(n, 1) / max_exact) / math.log(MAX_DISTANCE / max_exact)
                         * (N_BUCKETS - max_exact)).astype(np.int32)
    large = np.minimum(large, N_BUCKETS - 1)
    return np.where(n < max_exact, n, large).astype(np.int32)


def _band_codes_t():
    i = np.arange(BLOCK)[None, :]
    j = np.arange(2 * BLOCK)[:, None]
    rel = i + BLOCK - j
    band = (rel >= 0) & (rel < BLOCK)
    return np.where(band, np.where(j < BLOCK, 1.0, 0.0), 2.0).astype(np.float32)


def _bias_kernel(bucket_ref, rel_ref, o_ref):
    bucket = bucket_ref[...]
    for head in range(N_HEADS):
        acc = jnp.zeros((2 * BLOCK, BLOCK), F32)
        for b in range(N_BUCKETS):
            acc = jnp.where(bucket == b, rel_ref[b, head], acc)
        kv_head, g = divmod(head, GQA_GROUPS)
        o_ref[kv_head, g] = acc * LOG2E


def _position_bias_t(rel_bias):
    buckets_t = jnp.asarray(np.ascontiguousarray(_t5_causal_buckets().T))
    return pl.pallas_call(
        _bias_kernel,
        in_specs=[
            pl.BlockSpec(memory_space=pltpu.VMEM),
            pl.BlockSpec(memory_space=pltpu.SMEM),
        ],
        out_specs=pl.BlockSpec(memory_space=pltpu.VMEM),
        out_shape=jax.ShapeDtypeStruct((N_KV_HEADS, GQA_GROUPS, 2 * BLOCK, BLOCK), F32),
        name="position_bias",
    )(buckets_t, rel_bias)


def _attn_kernel(x_ref, shift_ref, scale_ref, gate_ref, g_ref, wqt_ref, wz_ref, kp_ref, kc_ref,
                 vtp_ref, vtc_ref, bias_ref, code_ref, sink_ref, wout_ref, fg_ref, o_ref,
                 h_buf, qt_buf, z_buf, og_buf, s_buf, p_buf, r_buf, y_buf, *, final_norm):
    t = pl.program_id(1)
    n_chunks = ATTN_TILE // ATTN_CHUNK
    pieces = B_WIDTH // PROJ_PIECE
    steps_per_chunk = (ATTN_CHUNK // BLOCK) * N_KV_HEADS
    n_steps = n_chunks * steps_per_chunk
    norm_rows = ATTN_CHUNK // steps_per_chunk

    def norm_slice(c, j):
        _norm_modulate(x_ref, g_ref, shift_ref, scale_ref, h_buf, norm_rows,
                       start=c * ATTN_CHUNK + j * norm_rows)

    h_chunks = {}

    def project_piece(c, n):
        chunk = slice(c * ATTN_CHUNK, (c + 1) * ATTN_CHUNK)
        if c not in h_chunks:
            h_chunks[c] = h_buf[chunk, :]
        h = h_chunks[c]
        if n < pieces:
            sl = slice(n * PROJ_PIECE, (n + 1) * PROJ_PIECE)
            qt = lax.dot_general(wqt_ref[0, sl, :], h, _NT_DIMS, preferred_element_type=F32)
            qt = qt.astype(BF16)
            for blk in range(ATTN_CHUNK // BLOCK):
                qt_buf[c * (ATTN_CHUNK // BLOCK) + blk, sl, :] = qt[:, blk * BLOCK:(blk + 1) * BLOCK]
        else:
            sl = slice((n - pieces) * PROJ_PIECE, (n - pieces + 1) * PROJ_PIECE)
            z = jnp.dot(h, wz_ref[0, :, sl], preferred_element_type=F32)
            for sub in range(PROJ_PIECE // LANES):
                z_buf[(n - pieces) * (PROJ_PIECE // LANES) + sub, chunk, :] = z[:, sub * LANES:(sub + 1) * LANES]

    def out_piece(c, n):
        chunk = slice(c * ATTN_CHUNK, (c + 1) * ATTN_CHUNK)
        sl = slice(n * PROJ_PIECE, (n + 1) * PROJ_PIECE)
        og = jnp.concatenate([og_buf[sb, chunk, :] for sb in range(B_WIDTH // LANES)], axis=1)
        y_buf[c % 2, n] = jnp.dot(og, wout_ref[:, sl], preferred_element_type=F32)
        if n == pieces - 1:
            for r in range(0, ATTN_CHUNK, NORM_ROWS):
                rows = slice(c * ATTN_CHUNK + r, c * ATTN_CHUNK + r + NORM_ROWS)
                y = jnp.concatenate([y_buf[c % 2, m, r:r + NORM_ROWS, :] for m in range(pieces)], axis=1)
                out = x_ref[0, rows, :] + gate_ref[0] * y
                if final_norm:
                    ms = jnp.mean(out * out, axis=-1, keepdims=True)
                    out = (out * lax.rsqrt(ms + EPS)) * fg_ref[...]
                o_ref[0, rows, :] = out

    def stage_scores(step):
        i, kvh = divmod(step, N_KV_HEADS)
        _scores(i, kvh, qt_buf, kp_ref, kc_ref, s_buf.at[step % 2])

    def stage_softmax(step):
        i, kvh = divmod(step, N_KV_HEADS)
        _softmax(i, kvh, t, s_buf.at[step % 2], code_ref, bias_ref, sink_ref,
                 p_buf.at[step % 2], r_buf.at[step % 2])

    def stage_pv(step):
        i, kvh = divmod(step, N_KV_HEADS)
        _pv_gate(i, kvh, p_buf.at[step % 2], r_buf.at[step % 2], vtp_ref, vtc_ref, z_buf, og_buf)

    norm_slices = [(c, j) for c in range(n_chunks) for j in range(steps_per_chunk)]
    for c, j in norm_slices[:2 * steps_per_chunk]:
        norm_slice(c, j)
    for n in range(2 * pieces):
        project_piece(0, n)

    def out_slices(ko):
        co, no = divmod(ko, steps_per_chunk)
        for n in range(no * pieces // steps_per_chunk, (no + 1) * pieces // steps_per_chunk):
            out_piece(co, n)

    out_lag = steps_per_chunk + 2
    for k in range(n_steps + 2):
        c, j = divmod(k, steps_per_chunk)
        if k < n_steps:
            stage_scores(k)
        if 1 <= k <= n_steps:
            stage_softmax(k - 1)
        if 2 <= k <= n_steps + 1:
            stage_pv(k - 2)
        if c + 2 < n_chunks:
            norm_slice(c + 2, j)
        if c + 1 < n_chunks:
            for n in range(j * 2 * pieces // steps_per_chunk, (j + 1) * 2 * pieces // steps_per_chunk):
                project_piece(c + 1, n)
        if k >= out_lag:
            out_slices(k - out_lag)
    for ko in range(n_steps + 2 - out_lag, n_steps):
        out_slices(ko)


def _scores(i, kvh, qt_buf, kp_ref, kc_ref, s_ref):
    rows = slice(i * BLOCK, (i + 1) * BLOCK)
    k_prev = kp_ref[0] if i == 0 else kc_ref[0, (i - 1) * BLOCK:i * BLOCK, :]
    k2 = jnp.concatenate([k_prev, kc_ref[0, rows, :]], axis=0)
    q_cat = jnp.concatenate(
        [qt_buf[i, (kvh * GQA_GROUPS + g) * HEAD_DIM:(kvh * GQA_GROUPS + g + 1) * HEAD_DIM, :]
         for g in range(GQA_GROUPS)], axis=1)
    q_zeros = jnp.zeros((HEAD_DIM, GQA_GROUPS * BLOCK), BF16)
    q_sel = jnp.concatenate([q_cat, q_zeros] if kvh == 0 else [q_zeros, q_cat], axis=0)
    s = jnp.dot(k2, q_sel, preferred_element_type=F32)
    for g in range(GQA_GROUPS):
        s_ref[g] = s[:, g * BLOCK:(g + 1) * BLOCK]


def _softmax(i, kvh, t, s_ref, code_ref, bias_ref, sink_ref, p_ref, r_ref):
    limit = jnp.where(t == 0, 1.0, 2.0) if i == 0 else 2.0
    for g in range(GQA_GROUPS):
        sink = sink_ref[kvh, :, g * BLOCK:(g + 1) * BLOCK]
        sg = jnp.where(code_ref[...] < limit, s_ref[g] + bias_ref[kvh, g], NEG_INF)
        m = jnp.maximum(jnp.max(sg, axis=0, keepdims=True), sink)
        p_ref[g] = jnp.exp2(sg - m).astype(BF16)
        r_ref[g] = jnp.exp2(sink - m)


def _pv_gate(i, kvh, p_ref, r_ref, vtp_ref, vtc_ref, z_buf, og_buf):
    rows = slice(i * BLOCK, (i + 1) * BLOCK)
    vt_prev = vtp_ref[0] if i == 0 else vtc_ref[0, :, (i - 1) * BLOCK:i * BLOCK]
    vt2 = jnp.concatenate([vt_prev, vtc_ref[0, :, rows]], axis=1)
    ones_rows = (lax.broadcasted_iota(jnp.int32, (SUM_ROWS, 2 * BLOCK), 0) == 0).astype(BF16)
    vt = jnp.concatenate([vt2[kvh * HEAD_DIM:(kvh + 1) * HEAD_DIM, :], ones_rows], axis=0)
    pt = jnp.concatenate([p_ref[g] for g in range(GQA_GROUPS)], axis=1)
    ot = jnp.dot(vt, pt, preferred_element_type=F32)
    inv_den = [1.0 / (ot[HEAD_DIM:HEAD_DIM + 1, g * BLOCK:(g + 1) * BLOCK] + r_ref[g])
               for g in range(GQA_GROUPS)]
    for p in range(GQA_GROUPS // 2):
        pair_t = jnp.concatenate(
            [ot[:HEAD_DIM, g * BLOCK:(g + 1) * BLOCK] * inv_den[g] for g in (2 * p, 2 * p + 1)], axis=0)
        slab = kvh * (GQA_GROUPS // 2) + p
        z = z_buf[slab, rows, :]
        og_buf[slab, rows, :] = (pair_t.T * (z * _sigmoid(z))).astype(BF16)


def _attn_weights_kernel(wq_ref, wzin_ref, wqt_ref, wz_ref):
    wqt_ref[0] = (wq_ref[0].T * (HEAD_DIM ** -0.5 * LOG2E)).astype(BF16)
    wz_ref[0] = wzin_ref[0].astype(BF16)


def _attn_weights(b_w_in):
    n_b, d, _ = b_w_in.shape
    blocks = B_WIDTH // PROJ_PIECE
    return pl.pallas_call(
        _attn_weights_kernel,
        grid=(n_b, blocks),
        in_specs=[
            pl.BlockSpec((1, d, PROJ_PIECE), lambda l, j: (l, 0, j)),
            pl.BlockSpec((1, d, PROJ_PIECE), lambda l, j: (l, 0, blocks + j)),
        ],
        out_specs=[
            pl.BlockSpec((1, PROJ_PIECE, d), lambda l, j: (l, j, 0)),
            pl.BlockSpec((1, d, PROJ_PIECE), lambda l, j: (l, 0, j)),
        ],
        out_shape=[jax.ShapeDtypeStruct((n_b, B_WIDTH, d), BF16),
                   jax.ShapeDtypeStruct((n_b, d, B_WIDTH), BF16)],
        compiler_params=pltpu.CompilerParams(
            dimension_semantics=("arbitrary", "arbitrary"), vmem_limit_bytes=VMEM_LIMIT_BYTES),
        name="attn_weights",
    )(b_w_in, b_w_in)


def _attn_layer(layer, x, shift, scale, gate, norm_g, w_q_t, w_z, k, v_t, bias, codes, sinks, w_out,
                final_g, final_norm):
    b, s, d = x.shape
    ts = ATTN_TILE
    blocks_per_tile = ts // BLOCK
    vec_spec = pl.BlockSpec((1, 1, d), lambda i, j: (i, 0, 0))
    score_shape = (2, GQA_GROUPS, 2 * BLOCK, BLOCK)

    def prev_block(j):
        return jnp.maximum(j * blocks_per_tile - 1, 0)

    return pl.pallas_call(
        functools.partial(_attn_kernel, final_norm=final_norm),
        grid=(b, s // ts),
        in_specs=[
            pl.BlockSpec((1, ts, d), lambda i, j: (i, j, 0)),
            vec_spec, vec_spec, vec_spec,
            _const_spec((1, d)),
            pl.BlockSpec((1, B_WIDTH, d), lambda i, j: (layer, 0, 0), pipeline_mode=pl.Buffered(1)),
            pl.BlockSpec((1, d, B_WIDTH), lambda i, j: (layer, 0, 0), pipeline_mode=pl.Buffered(1)),
            pl.BlockSpec((1, BLOCK, KV_WIDTH), lambda i, j: (i, prev_block(j), 0)),
            pl.BlockSpec((1, ts, KV_WIDTH), lambda i, j: (i, j, 0)),
            pl.BlockSpec((1, KV_WIDTH, BLOCK), lambda i, j: (i, 0, prev_block(j))),
            pl.BlockSpec((1, KV_WIDTH, ts), lambda i, j: (i, 0, j)),
            _const_spec((N_KV_HEADS, GQA_GROUPS, 2 * BLOCK, BLOCK)),
            _const_spec((2 * BLOCK, BLOCK)),
            _const_spec((N_KV_HEADS, 1, GQA_GROUPS * BLOCK)),
            _const_spec((B_WIDTH, d)),
            _const_spec((1, d)),
        ],
        out_specs=pl.BlockSpec((1, ts, d), lambda i, j: (i, j, 0)),
        out_shape=jax.ShapeDtypeStruct((b, s, d), F32),
        scratch_shapes=[
            pltpu.VMEM((ts, d), BF16),
            pltpu.VMEM((blocks_per_tile, B_WIDTH, BLOCK), BF16),
            pltpu.VMEM((B_WIDTH // LANES, ts, LANES), F32),
            pltpu.VMEM((B_WIDTH // LANES, ts, LANES), BF16),
            pltpu.VMEM(score_shape, F32),
            pltpu.VMEM(score_shape, BF16),
            pltpu.VMEM((2, GQA_GROUPS, 1, BLOCK), F32),
            pltpu.VMEM((2, d // PROJ_PIECE, ATTN_CHUNK, PROJ_PIECE), F32),
        ],
        compiler_params=pltpu.CompilerParams(
            dimension_semantics=("arbitrary", "arbitrary"), vmem_limit_bytes=VMEM_LIMIT_BYTES),
        name="attn_layer",
    )(x, shift, scale, gate, norm_g, w_q_t, w_z, k, k, v_t, v_t, bias, codes, sinks, w_out, final_g)


def kernel(x, c, norm_g, ada_w, ada_b, a_w_in, a_w_group, a_scale, a_w_out, kv_norm_g, kv_ada_w,
           kv_ada_b, w_kv, b_w_in, b_sinks, b_w_out, rel_bias, final_g):
    b, s, d = x.shape
    depth = norm_g.shape[0]
    n_a = a_w_in.shape[0]
    assert (d, s % SEQ_TILE, s % ATTN_TILE) == (D_MODEL, 0, 0) and b <= SUBLANES

    c_pad = jnp.pad(c, ((0, SUBLANES - b), (0, 0)))
    mod = _modulation(c_pad, ada_w, ada_b.reshape(depth, 1, 3 * d))[:, :b]
    mod_kv = _modulation(c_pad, kv_ada_w[None], kv_ada_b.reshape(1, 1, 2 * d))[0, :b]

    def vec(m, k):
        return m[:, k * d:(k + 1) * d].reshape(b, 1, d)

    w_v, w_z = _fuse_group_weights(a_w_in, a_w_group, a_scale)
    kv_args = (vec(mod_kv, 0), vec(mod_kv, 1), kv_norm_g.reshape(1, d),
               w_kv[:, :KV_WIDTH].astype(BF16), w_kv[:, KV_WIDTH:].T.astype(BF16))
    for l in range(n_a):
        res = _pool_layer(l, x, vec(mod[l], 0), vec(mod[l], 1), vec(mod[l], 2), norm_g[l].reshape(1, d),
                          w_v, w_z, a_w_out[l].astype(BF16),
                          kv=kv_args if l == n_a - 1 else None)
        x = res if l < n_a - 1 else res[0]
    k, v_t = res[1], res[2]
    bias = _position_bias_t(rel_bias)
    codes = jnp.asarray(_band_codes_t())
    w_q_t, w_qz = _attn_weights(b_w_in)
    for l in range(n_a, depth):
        j = l - n_a
        sinks = jnp.repeat(b_sinks[j] * LOG2E, BLOCK).reshape(N_KV_HEADS, 1, GQA_GROUPS * BLOCK)
        x = _attn_layer(j, x, vec(mod[l], 0), vec(mod[l], 1), vec(mod[l], 2), norm_g[l].reshape(1, d),
                        w_q_t, w_qz, k, v_t, bias, codes, sinks, b_w_out[j].astype(BF16),
                        final_g.reshape(1, d), final_norm=(l == depth - 1))
    return x
```

```python
import functools
import math

import numpy as np
import jax
import jax.numpy as jnp
from jax import lax
from jax.experimental import pallas as pl
from jax.experimental.pallas import tpu as pltpu

F32 = jnp.float32
BF16 = jnp.bfloat16

D_MODEL = 1024
A_WIDTH = 2048
POOL_WINDOWS = (2, 4, 8, 16)
N_GROUPS = len(POOL_WINDOWS)
GROUP_WIDTH = A_WIDTH // N_GROUPS
HEAD_DIM = 64
N_HEADS = 16
N_KV_HEADS = 2
GQA_GROUPS = N_HEADS // N_KV_HEADS
KV_WIDTH = N_KV_HEADS * HEAD_DIM
B_WIDTH = N_HEADS * HEAD_DIM
BLOCK = 128
N_BUCKETS = 32
MAX_DISTANCE = 128
EPS = 1e-6
NEG_INF = -1e30
LOG2E = math.log2(math.e)

LANES = 128
SUBLANES = 8
VMEM_LIMIT_BYTES = 56 * 1024 * 1024

MAX_HALO = 16
SEQ_TILE = 1024
ATTN_TILE = 1024
ATTN_CHUNK = 256
PROJ_PIECE = 256
SUM_ROWS = 16
KV_CHUNK = 256
NORM_ROWS = 32
POOL_ROWS = 64

_NT_DIMS = (((1,), (1,)), ((), ()))


def _sigmoid(v):
    return 1.0 / (1.0 + jnp.exp(-v))


def _const_spec(shape):
    zeros = (0,) * len(shape)
    return pl.BlockSpec(shape, lambda *_: zeros, pipeline_mode=pl.Buffered(1))


def _mod_kernel(c_ref, w_ref, b_ref, o_ref):
    c = c_ref[...]
    c_act = c * _sigmoid(c)
    o_ref[0] = jnp.dot(c_act, w_ref[0], preferred_element_type=F32) + b_ref[0]


def _modulation(c_pad, w, b):
    n_layers, d, n = w.shape
    rows = c_pad.shape[0]
    nt = D_MODEL
    return pl.pallas_call(
        _mod_kernel,
        grid=(n_layers, n // nt),
        in_specs=[
            pl.BlockSpec((rows, d), lambda l, j: (0, 0)),
            pl.BlockSpec((1, d, nt), lambda l, j: (l, 0, j)),
            pl.BlockSpec((1, 1, nt), lambda l, j: (l, 0, j)),
        ],
        out_specs=pl.BlockSpec((1, rows, nt), lambda l, j: (l, 0, j)),
        out_shape=jax.ShapeDtypeStruct((n_layers, rows, n), F32),
        compiler_params=pltpu.CompilerParams(
            dimension_semantics=("arbitrary", "arbitrary"), vmem_limit_bytes=VMEM_LIMIT_BYTES),
        name="adaln_modulation",
    )(c_pad, w, b)


def _norm_modulate(x_ref, g_ref, shift_ref, scale_ref, h_ref, rows, start=0):
    gain = g_ref[...] * (1.0 + scale_ref[0])
    shift = shift_ref[0]
    for r in range(start, start + rows, NORM_ROWS):
        xv = x_ref[0, r:r + NORM_ROWS, :]
        ms = jnp.mean(xv * xv, axis=-1, keepdims=True)
        xn = xv * lax.rsqrt(ms + EPS)
        h_ref[r:r + NORM_ROWS, :] = (xn * gain + shift).astype(BF16)


def _split_bf16(a):
    hi = a.astype(BF16)
    lo = (a - hi.astype(F32)).astype(BF16)
    return hi, lo


def _fuse_kernel(wu_ref, wzin_ref, wg_ref, asc_ref, wv_ref, wz_ref):
    a_hi, a_lo = _split_bf16(wu_ref[0])
    b_hi, b_lo = _split_bf16(wg_ref[0, 0])
    acc = jnp.dot(a_hi, b_hi, preferred_element_type=F32)
    acc = acc + jnp.dot(a_hi, b_lo, preferred_element_type=F32)
    acc = acc + jnp.dot(a_lo, b_hi, preferred_element_type=F32)
    wv_ref[0] = (acc * asc_ref[0]).astype(BF16)
    wz_ref[0] = wzin_ref[0].astype(BF16)


def _fuse_group_weights(a_w_in, a_w_group, a_scale):
    n_a, d, _ = a_w_in.shape
    col_block = pl.BlockSpec((1, d, GROUP_WIDTH), lambda l, g: (l, 0, g))
    return pl.pallas_call(
        _fuse_kernel,
        grid=(n_a, N_GROUPS),
        in_specs=[
            col_block,
            pl.BlockSpec((1, d, GROUP_WIDTH), lambda l, g: (l, 0, N_GROUPS + g)),
            pl.BlockSpec((1, 1, GROUP_WIDTH, GROUP_WIDTH), lambda l, g: (l, g, 0, 0)),
            pl.BlockSpec((1, 1, GROUP_WIDTH), lambda l, g: (l, 0, g)),
        ],
        out_specs=[col_block, col_block],
        out_shape=[jax.ShapeDtypeStruct((n_a, d, A_WIDTH), BF16)] * 2,
        compiler_params=pltpu.CompilerParams(
            dimension_semantics=("arbitrary", "arbitrary"), vmem_limit_bytes=VMEM_LIMIT_BYTES),
        name="fuse_group_weights",
    )(a_w_in, a_w_in, a_w_group, a_scale.reshape(n_a, 1, A_WIDTH))


def _pool_kernel(x_ref, shift_ref, scale_ref, gate_ref, g_ref, wv_ref, wz_ref, wout_ref, *rest, emit_kv):
    if emit_kv:
        kshift_ref, kscale_ref, kg_ref, wk_ref, wvt_ref, o_ref, k_ref, vt_ref = rest[:8]
        hk_buf = rest[-1]
        rest = rest[:-1]
    else:
        o_ref = rest[0]
    h_buf, v_ext, z_buf, gated_buf, carry = rest[-5:]
    t = pl.program_id(1)
    ts = SEQ_TILE

    @pl.when(t == 0)
    def _():
        carry[...] = jnp.zeros_like(carry)

    _norm_modulate(x_ref, g_ref, shift_ref, scale_ref, h_buf, ts)
    h = h_buf[...]

    for g, w in enumerate(POOL_WINDOWS):
        cols = slice(g * GROUP_WIDTH, (g + 1) * GROUP_WIDTH)
        z_buf[g] = jnp.dot(h, wz_ref[0, :, cols], preferred_element_type=F32)
        v_ext[g, 0:MAX_HALO, :] = carry[g]
        v_ext[g, MAX_HALO:, :] = jnp.dot(h, wv_ref[0, :, cols], preferred_element_type=F32)
        carry[g] = v_ext[g, ts:ts + MAX_HALO, :]

        halo = MAX_HALO if w > SUBLANES else SUBLANES
        for r in range(0, ts, POOL_ROWS):
            ev = v_ext[g, MAX_HALO + r - halo:MAX_HALO + r + POOL_ROWS, :]
            s = ev
            k = 1
            while k < w:
                s = s + pltpu.roll(s, k, 0)
                k *= 2
            s = s[halo:]
            if r == 0:
                pos = t * ts + lax.broadcasted_iota(jnp.int32, (POOL_ROWS, GROUP_WIDTH), 0)
                mean = s * (1.0 / jnp.minimum(pos + 1, w).astype(F32))
            else:
                mean = s * (1.0 / w)
            z = z_buf[g, r:r + POOL_ROWS, :]
            gated_buf[r:r + POOL_ROWS, cols] = ((mean - ev[halo:]) * (z * _sigmoid(z))).astype(BF16)

    if not emit_kv:
        y = jnp.dot(gated_buf[...], wout_ref[...], preferred_element_type=F32)
        o_ref[0] = x_ref[0] + gate_ref[0] * y
        return

    def out_chunk(r):
        rows = slice(r, r + KV_CHUNK)
        y = jnp.dot(gated_buf[rows, :], wout_ref[...], preferred_element_type=F32)
        o_ref[0, rows, :] = x_ref[0, rows, :] + gate_ref[0] * y

    def kv_chunk(r):
        rows = slice(r, r + KV_CHUNK)
        _norm_modulate(o_ref, kg_ref, kshift_ref, kscale_ref, hk_buf, KV_CHUNK, start=r)
        hk = hk_buf[rows, :]
        k_ref[0, rows, :] = jnp.dot(hk, wk_ref[...], preferred_element_type=F32).astype(BF16)
        vt_ref[0, :, rows] = lax.dot_general(wvt_ref[...], hk, _NT_DIMS,
                                             preferred_element_type=F32).astype(BF16)

    starts = list(range(0, ts, KV_CHUNK))
    out_chunk(starts[0])
    for prev, cur in zip(starts[:-1], starts[1:]):
        out_chunk(cur)
        kv_chunk(prev)
    kv_chunk(starts[-1])


def _pool_layer(layer, x, shift, scale, gate, norm_g, w_v, w_z, w_out, kv=None):
    b, s, d = x.shape
    ts = SEQ_TILE
    vec_spec = pl.BlockSpec((1, 1, d), lambda i, j: (i, 0, 0))
    x_spec = pl.BlockSpec((1, ts, d), lambda i, j: (i, j, 0))
    layer_weights = pl.BlockSpec((1, d, A_WIDTH), lambda i, j: (layer, 0, 0), pipeline_mode=pl.Buffered(1))
    in_specs = [
        x_spec,
        vec_spec, vec_spec, vec_spec,
        _const_spec((1, d)),
        layer_weights,
        layer_weights,
        _const_spec((A_WIDTH, d)),
    ]
    out_specs = x_spec
    out_shape = jax.ShapeDtypeStruct((b, s, d), F32)
    args = (x, shift, scale, gate, norm_g, w_v, w_z, w_out)
    if kv is not None:
        in_specs += [vec_spec, vec_spec, _const_spec((1, d)), _const_spec((d, KV_WIDTH)),
                     _const_spec((KV_WIDTH, d))]
        out_specs = [x_spec,
                     pl.BlockSpec((1, ts, KV_WIDTH), lambda i, j: (i, j, 0)),
                     pl.BlockSpec((1, KV_WIDTH, ts), lambda i, j: (i, 0, j))]
        out_shape = [out_shape,
                     jax.ShapeDtypeStruct((b, s, KV_WIDTH), BF16),
                     jax.ShapeDtypeStruct((b, KV_WIDTH, s), BF16)]
        args += tuple(kv)
    scratch_shapes = [
        pltpu.VMEM((ts, d), BF16),
        pltpu.VMEM((N_GROUPS, ts + MAX_HALO, GROUP_WIDTH), F32),
        pltpu.VMEM((N_GROUPS, ts, GROUP_WIDTH), F32),
        pltpu.VMEM((ts, A_WIDTH), BF16),
        pltpu.VMEM((N_GROUPS, MAX_HALO, GROUP_WIDTH), F32),
    ]
    if kv is not None:
        scratch_shapes.append(pltpu.VMEM((ts, d), BF16))
    return pl.pallas_call(
        functools.partial(_pool_kernel, emit_kv=kv is not None),
        grid=(b, s // ts),
        in_specs=in_specs,
        out_specs=out_specs,
        out_shape=out_shape,
        scratch_shapes=scratch_shapes,
        compiler_params=pltpu.CompilerParams(
            dimension_semantics=("arbitrary", "arbitrary"), vmem_limit_bytes=VMEM_LIMIT_BYTES),
        name="pool_layer",
    )(*args)


def _t5_causal_buckets():
    i = np.arange(BLOCK)[:, None]
    j = np.arange(2 * BLOCK)[None, :]
    n = np.maximum(i + BLOCK - j, 0)
    max_exact = N_BUCKETS // 2
    large = max_exact + (np.log(np.maximum(n, 1) / max_exact) / math.log(MAX_DISTANCE / max_exact)
                         * (N_BUCKETS - max_exact)).astype(np.int32)
    large = np.minimum(large, N_BUCKETS - 1)
    return np.where(n < max_exact, n, large).astype(np.int32)


def _band_codes_t():
    i = np.arange(BLOCK)[None, :]
    j = np.arange(2 * BLOCK)[:, None]
    rel = i + BLOCK - j
    band = (rel >= 0) & (rel < BLOCK)
    return np.where(band, np.where(j < BLOCK, 1.0, 0.0), 2.0).astype(np.float32)


def _bias_kernel(bucket_ref, rel_ref, o_ref):
    bucket = bucket_ref[...]
    for head in range(N_HEADS):
        acc = jnp.zeros((2 * BLOCK, BLOCK), F32)
        for b in range(N_BUCKETS):
            acc = jnp.where(bucket == b, rel_ref[b, head], acc)
        kv_head, g = divmod(head, GQA_GROUPS)
        o_ref[kv_head, g] = acc * LOG2E


def _position_bias_t(rel_bias):
    buckets_t = jnp.asarray(np.ascontiguousarray(_t5_causal_buckets().T))
    return pl.pallas_call(
        _bias_kernel,
        in_specs=[
            pl.BlockSpec(memory_space=pltpu.VMEM),
            pl.BlockSpec(memory_space=pltpu.SMEM),
        ],
        out_specs=pl.BlockSpec(memory_space=pltpu.VMEM),
        out_shape=jax.ShapeDtypeStruct((N_KV_HEADS, GQA_GROUPS, 2 * BLOCK, BLOCK), F32),
        name="position_bias",
    )(buckets_t, rel_bias)


def _attn_kernel(x_ref, shift_ref, scale_ref, gate_ref, g_ref, wqt_ref, wz_ref, kp_ref, kc_ref,
                 vtp_ref, vtc_ref, bias_ref, code_ref, sink_ref, wout_ref, fg_ref, o_ref,
                 h_buf, qt_buf, z_buf, og_buf, s_buf, p_buf, r_buf, y_buf, *, final_norm):
    t = pl.program_id(1)
    n_chunks = ATTN_TILE // ATTN_CHUNK
    pieces = B_WIDTH // PROJ_PIECE
    steps_per_chunk = (ATTN_CHUNK // BLOCK) * N_KV_HEADS
    n_steps = n_chunks * steps_per_chunk
    norm_rows = ATTN_CHUNK // steps_per_chunk

    def norm_slice(c, j):
        _norm_modulate(x_ref, g_ref, shift_ref, scale_ref, h_buf, norm_rows,
                       start=c * ATTN_CHUNK + j * norm_rows)

    h_chunks = {}

    def project_piece(c, n):
        chunk = slice(c * ATTN_CHUNK, (c + 1) * ATTN_CHUNK)
        if c not in h_chunks:
            h_chunks[c] = h_buf[chunk, :]
        h = h_chunks[c]
        if n < pieces:
            sl = slice(n * PROJ_PIECE, (n + 1) * PROJ_PIECE)
            qt = lax.dot_general(wqt_ref[0, sl, :], h, _NT_DIMS, preferred_element_type=F32)
            qt = qt.astype(BF16)
            for blk in range(ATTN_CHUNK // BLOCK):
                qt_buf[c * (ATTN_CHUNK // BLOCK) + blk, sl, :] = qt[:, blk * BLOCK:(blk + 1) * BLOCK]
        else:
            z = jnp.dot(h, wz_ref[0, n - pieces], preferred_element_type=F32)
            for sub in range(PROJ_PIECE // LANES):
                z_buf[(n - pieces) * (PROJ_PIECE // LANES) + sub, chunk, :] = z[:, sub * LANES:(sub + 1) * LANES]

    def out_piece(c, n):
        chunk = slice(c * ATTN_CHUNK, (c + 1) * ATTN_CHUNK)
        og = jnp.concatenate([og_buf[sb, chunk, :] for sb in range(B_WIDTH // LANES)], axis=1)
        y_buf[c % 2, n] = jnp.dot(og, wout_ref[0, n], preferred_element_type=F32)
        if n == pieces - 1:
            for r in range(0, ATTN_CHUNK, NORM_ROWS):
                rows = slice(c * ATTN_CHUNK + r, c * ATTN_CHUNK + r + NORM_ROWS)
                y = jnp.concatenate([y_buf[c % 2, m, r:r + NORM_ROWS, :] for m in range(pieces)], axis=1)
                out = x_ref[0, rows, :] + gate_ref[0] * y
                if final_norm:
                    ms = jnp.mean(out * out, axis=-1, keepdims=True)
                    out = (out * lax.rsqrt(ms + EPS)) * fg_ref[...]
                o_ref[0, rows, :] = out

    def stage_scores(step):
        i, kvh = divmod(step, N_KV_HEADS)
        _scores(i, kvh, qt_buf, kp_ref, kc_ref, s_buf.at[step % 2])

    def stage_softmax(step):
        i, kvh = divmod(step, N_KV_HEADS)
        _softmax(i, kvh, t, s_buf.at[step % 2], code_ref, bias_ref, sink_ref,
                 p_buf.at[step % 2], r_buf.at[step % 2])

    def stage_pv(step):
        i, kvh = divmod(step, N_KV_HEADS)
        _pv_gate(i, kvh, p_buf.at[step % 2], r_buf.at[step % 2], vtp_ref, vtc_ref, z_buf, og_buf)

    norm_slices = [(c, j) for c in range(n_chunks) for j in range(steps_per_chunk)]
    for c, j in norm_slices[:2 * steps_per_chunk]:
        norm_slice(c, j)
    for n in range(2 * pieces):
        project_piece(0, n)

    def out_slices(ko):
        co, no = divmod(ko, steps_per_chunk)
        for n in range(no * pieces // steps_per_chunk, (no + 1) * pieces // steps_per_chunk):
            out_piece(co, n)

    out_lag = steps_per_chunk + 2
    for k in range(n_steps + 2):
        c, j = divmod(k, steps_per_chunk)
        if k < n_steps:
            stage_scores(k)
        if 1 <= k <= n_steps:
            stage_softmax(k - 1)
        if 2 <= k <= n_steps + 1:
            stage_pv(k - 2)
        if c + 2 < n_chunks:
            norm_slice(c + 2, j)
        if c + 1 < n_chunks:
            for n in range(j * 2 * pieces // steps_per_chunk, (j + 1) * 2 * pieces // steps_per_chunk):
                project_piece(c + 1, n)
        if k >= out_lag:
            out_slices(k - out_lag)
    for ko in range(n_steps + 2 - out_lag, n_steps):
        out_slices(ko)


def _scores(i, kvh, qt_buf, kp_ref, kc_ref, s_ref):
    rows = slice(i * BLOCK, (i + 1) * BLOCK)
    k_prev = kp_ref[0] if i == 0 else kc_ref[0, (i - 1) * BLOCK:i * BLOCK, :]
    k2 = jnp.concatenate([k_prev, kc_ref[0, rows, :]], axis=0)
    q_cat = jnp.concatenate(
        [qt_buf[i, (kvh * GQA_GROUPS + g) * HEAD_DIM:(kvh * GQA_GROUPS + g + 1) * HEAD_DIM, :]
         for g in range(GQA_GROUPS)], axis=1)
    q_zeros = jnp.zeros((HEAD_DIM, GQA_GROUPS * BLOCK), BF16)
    q_sel = jnp.concatenate([q_cat, q_zeros] if kvh == 0 else [q_zeros, q_cat], axis=0)
    s = jnp.dot(k2, q_sel, preferred_element_type=F32)
    for g in range(GQA_GROUPS):
        s_ref[g] = s[:, g * BLOCK:(g + 1) * BLOCK]


def _softmax(i, kvh, t, s_ref, code_ref, bias_ref, sink_ref, p_ref, r_ref):
    limit = jnp.where(t == 0, 1.0, 2.0) if i == 0 else 2.0
    for g in range(GQA_GROUPS):
        sink = sink_ref[kvh, :, g * BLOCK:(g + 1) * BLOCK]
        sg = jnp.where(code_ref[...] < limit, s_ref[g] + bias_ref[kvh, g], NEG_INF)
        m = jnp.maximum(jnp.max(sg, axis=0, keepdims=True), sink)
        p_ref[g] = jnp.exp2(sg - m).astype(BF16)
        r_ref[g] = jnp.exp2(sink - m)


def _pv_gate(i, kvh, p_ref, r_ref, vtp_ref, vtc_ref, z_buf, og_buf):
    rows = slice(i * BLOCK, (i + 1) * BLOCK)
    vt_prev = vtp_ref[0] if i == 0 else vtc_ref[0, :, (i - 1) * BLOCK:i * BLOCK]
    vt2 = jnp.concatenate([vt_prev, vtc_ref[0, :, rows]], axis=1)
    ones_rows = (lax.broadcasted_iota(jnp.int32, (SUM_ROWS, 2 * BLOCK), 0) == 0).astype(BF16)
    vt = jnp.concatenate([vt2[kvh * HEAD_DIM:(kvh + 1) * HEAD_DIM, :], ones_rows], axis=0)
    pt = jnp.concatenate([p_ref[g] for g in range(GQA_GROUPS)], axis=1)
    ot = jnp.dot(vt, pt, preferred_element_type=F32)
    inv_den = [1.0 / (ot[HEAD_DIM:HEAD_DIM + 1, g * BLOCK:(g + 1) * BLOCK] + r_ref[g])
               for g in range(GQA_GROUPS)]
    for p in range(GQA_GROUPS // 2):
        pair_t = jnp.concatenate(
            [ot[:HEAD_DIM, g * BLOCK:(g + 1) * BLOCK] * inv_den[g] for g in (2 * p, 2 * p + 1)], axis=0)
        slab = kvh * (GQA_GROUPS // 2) + p
        z = z_buf[slab, rows, :]
        og_buf[slab, rows, :] = (pair_t.T * (z * _sigmoid(z))).astype(BF16)


def _attn_weights_kernel(wq_ref, wzin_ref, woutin_ref, wqt_ref, wz_ref, wout_ref):
    wqt_ref[0] = (wq_ref[0].T * (HEAD_DIM ** -0.5 * LOG2E)).astype(BF16)
    wz_ref[0, 0] = wzin_ref[0].astype(BF16)
    wout_ref[0, 0] = woutin_ref[0].astype(BF16)


def _attn_weights(b_w_in, b_w_out):
    n_b, d, _ = b_w_in.shape
    blocks = B_WIDTH // PROJ_PIECE
    slab_spec = pl.BlockSpec((1, 1, d, PROJ_PIECE), lambda l, j: (l, j, 0, 0))
    slab_shape = jax.ShapeDtypeStruct((n_b, blocks, d, PROJ_PIECE), BF16)
    return pl.pallas_call(
        _attn_weights_kernel,
        grid=(n_b, blocks),
        in_specs=[
            pl.BlockSpec((1, d, PROJ_PIECE), lambda l, j: (l, 0, j)),
            pl.BlockSpec((1, d, PROJ_PIECE), lambda l, j: (l, 0, blocks + j)),
            pl.BlockSpec((1, B_WIDTH, PROJ_PIECE), lambda l, j: (l, 0, j)),
        ],
        out_specs=[pl.BlockSpec((1, PROJ_PIECE, d), lambda l, j: (l, j, 0)), slab_spec, slab_spec],
        out_shape=[jax.ShapeDtypeStruct((n_b, B_WIDTH, d), BF16), slab_shape, slab_shape],
        compiler_params=pltpu.CompilerParams(
            dimension_semantics=("arbitrary", "arbitrary"), vmem_limit_bytes=VMEM_LIMIT_BYTES),
        name="attn_weights",
    )(b_w_in, b_w_in, b_w_out)


def _attn_layer(layer, x, shift, scale, gate, norm_g, w_q_t, w_z, k, v_t, bias, codes, sinks, w_out,
                final_g, final_norm):
    b, s, d = x.shape
    ts = ATTN_TILE
    blocks_per_tile = ts // BLOCK
    vec_spec = pl.BlockSpec((1, 1, d), lambda i, j: (i, 0, 0))
    score_shape = (2, GQA_GROUPS, 2 * BLOCK, BLOCK)
    weight_slabs = pl.BlockSpec((1, B_WIDTH // PROJ_PIECE, d, PROJ_PIECE), lambda i, j: (layer, 0, 0, 0),
                                pipeline_mode=pl.Buffered(1))

    def prev_block(j):
        return jnp.maximum(j * blocks_per_tile - 1, 0)

    return pl.pallas_call(
        functools.partial(_attn_kernel, final_norm=final_norm),
        grid=(b, s // ts),
        in_specs=[
            pl.BlockSpec((1, ts, d), lambda i, j: (i, j, 0)),
            vec_spec, vec_spec, vec_spec,
            _const_spec((1, d)),
            pl.BlockSpec((1, B_WIDTH, d), lambda i, j: (layer, 0, 0), pipeline_mode=pl.Buffered(1)),
            weight_slabs,
            pl.BlockSpec((1, BLOCK, KV_WIDTH), lambda i, j: (i, prev_block(j), 0)),
            pl.BlockSpec((1, ts, KV_WIDTH), lambda i, j: (i, j, 0)),
            pl.BlockSpec((1, KV_WIDTH, BLOCK), lambda i, j: (i, 0, prev_block(j))),
            pl.BlockSpec((1, KV_WIDTH, ts), lambda i, j: (i, 0, j)),
            _const_spec((N_KV_HEADS, GQA_GROUPS, 2 * BLOCK, BLOCK)),
            _const_spec((2 * BLOCK, BLOCK)),
            _const_spec((N_KV_HEADS, 1, GQA_GROUPS * BLOCK)),
            weight_slabs,
            _const_spec((1, d)),
        ],
        out_specs=pl.BlockSpec((1, ts, d), lambda i, j: (i, j, 0)),
        out_shape=jax.ShapeDtypeStruct((b, s, d), F32),
        scratch_shapes=[
            pltpu.VMEM((ts, d), BF16),
            pltpu.VMEM((blocks_per_tile, B_WIDTH, BLOCK), BF16),
            pltpu.VMEM((B_WIDTH // LANES, ts, LANES), F32),
            pltpu.VMEM((B_WIDTH // LANES, ts, LANES), BF16),
            pltpu.VMEM(score_shape, F32),
            pltpu.VMEM(score_shape, BF16),
            pltpu.VMEM((2, GQA_GROUPS, 1, BLOCK), F32),
            pltpu.VMEM((2, d // PROJ_PIECE, ATTN_CHUNK, PROJ_PIECE), F32),
        ],
        compiler_params=pltpu.CompilerParams(
            dimension_semantics=("arbitrary", "arbitrary"), vmem_limit_bytes=VMEM_LIMIT_BYTES),
        name="attn_layer",
    )(x, shift, scale, gate, norm_g, w_q_t, w_z, k, k, v_t, v_t, bias, codes, sinks, w_out, final_g)


def kernel(x, c, norm_g, ada_w, ada_b, a_w_in, a_w_group, a_scale, a_w_out, kv_norm_g, kv_ada_w,
           kv_ada_b, w_kv, b_w_in, b_sinks, b_w_out, rel_bias, final_g):
    b, s, d = x.shape
    depth = norm_g.shape[0]
    n_a = a_w_in.shape[0]
    assert (d, s % SEQ_TILE, s % ATTN_TILE) == (D_MODEL, 0, 0) and b <= SUBLANES

    c_pad = jnp.pad(c, ((0, SUBLANES - b), (0, 0)))
    mod = _modulation(c_pad, ada_w, ada_b.reshape(depth, 1, 3 * d))[:, :b]
    mod_kv = _modulation(c_pad, kv_ada_w[None], kv_ada_b.reshape(1, 1, 2 * d))[0, :b]

    def vec(m, k):
        return m[:, k * d:(k + 1) * d].reshape(b, 1, d)

    w_v, w_z = _fuse_group_weights(a_w_in, a_w_group, a_scale)
    kv_args = (vec(mod_kv, 0), vec(mod_kv, 1), kv_norm_g.reshape(1, d),
               w_kv[:, :KV_WIDTH].astype(BF16), w_kv[:, KV_WIDTH:].T.astype(BF16))
    for l in range(n_a):
        res = _pool_layer(l, x, vec(mod[l], 0), vec(mod[l], 1), vec(mod[l], 2), norm_g[l].reshape(1, d),
                          w_v, w_z, a_w_out[l].astype(BF16),
                          kv=kv_args if l == n_a - 1 else None)
        x = res if l < n_a - 1 else res[0]
    k, v_t = res[1], res[2]
    bias = _position_bias_t(rel_bias)
    codes = jnp.asarray(_band_codes_t())
    w_q_t, w_qz, w_o = _attn_weights(b_w_in, b_w_out)
    for l in range(n_a, depth):
        j = l - n_a
        sinks = jnp.repeat(b_sinks[j] * LOG2E, BLOCK).reshape(N_KV_HEADS, 1, GQA_GROUPS * BLOCK)
        x = _attn_layer(j, x, vec(mod[l], 0), vec(mod[l], 1), vec(mod[l], 2), norm_g[l].reshape(1, d),
                        w_q_t, w_qz, k, v_t, bias, codes, sinks, w_o,
                        final_g.reshape(1, d), final_norm=(l == depth - 1))
    return x
```

```python
import functools
import math

import numpy as np
import jax
import jax.numpy as jnp
from jax import lax
from jax.experimental import pallas as pl
from jax.experimental.pallas import tpu as pltpu

F32 = jnp.float32
BF16 = jnp.bfloat16

D_MODEL = 1024
A_WIDTH = 2048
POOL_WINDOWS = (2, 4, 8, 16)
N_GROUPS = len(POOL_WINDOWS)
GROUP_WIDTH = A_WIDTH // N_GROUPS
HEAD_DIM = 64
N_HEADS = 16
N_KV_HEADS = 2
GQA_GROUPS = N_HEADS // N_KV_HEADS
KV_WIDTH = N_KV_HEADS * HEAD_DIM
B_WIDTH = N_HEADS * HEAD_DIM
BLOCK = 128
N_BUCKETS = 32
MAX_DISTANCE = 128
EPS = 1e-6
NEG_INF = -1e30
LOG2E = math.log2(math.e)

LANES = 128
SUBLANES = 8
VMEM_LIMIT_BYTES = 56 * 1024 * 1024

MAX_HALO = 16
SEQ_TILE = 1024
ATTN_TILE = 1024
ATTN_CHUNK = 256
PROJ_PIECE = 256
SUM_ROWS = 16
KV_CHUNK = 256
NORM_ROWS = 32
K_SLAB = 256
POOL_ROWS = 256

_NT_DIMS = (((1,), (1,)), ((), ()))


def _sigmoid(v):
    return 1.0 / (1.0 + jnp.exp(-v))


def _const_spec(shape):
    zeros = (0,) * len(shape)
    return pl.BlockSpec(shape, lambda *_: zeros, pipeline_mode=pl.Buffered(1))


def _mod_kernel(c_ref, w_ref, b_ref, o_ref):
    c = c_ref[...]
    c_act = c * _sigmoid(c)
    o_ref[0] = jnp.dot(c_act, w_ref[0], preferred_element_type=F32) + b_ref[0]


def _modulation(c_pad, w, b):
    n_layers, d, n = w.shape
    rows = c_pad.shape[0]
    nt = D_MODEL
    return pl.pallas_call(
        _mod_kernel,
        grid=(n_layers, n // nt),
        in_specs=[
            pl.BlockSpec((rows, d), lambda l, j: (0, 0)),
            pl.BlockSpec((1, d, nt), lambda l, j: (l, 0, j)),
            pl.BlockSpec((1, 1, nt), lambda l, j: (l, 0, j)),
        ],
        out_specs=pl.BlockSpec((1, rows, nt), lambda l, j: (l, 0, j)),
        out_shape=jax.ShapeDtypeStruct((n_layers, rows, n), F32),
        compiler_params=pltpu.CompilerParams(
            dimension_semantics=("arbitrary", "arbitrary"), vmem_limit_bytes=VMEM_LIMIT_BYTES),
        name="adaln_modulation",
    )(c_pad, w, b)


def _norm_modulate(x_ref, g_ref, shift_ref, scale_ref, h_ref, rows, start=0):
    gain = g_ref[...] * (1.0 + scale_ref[0])
    shift = shift_ref[0]
    for r in range(start, start + rows, NORM_ROWS):
        xv = x_ref[0, r:r + NORM_ROWS, :]
        ms = jnp.mean(xv * xv, axis=-1, keepdims=True)
        xn = xv * lax.rsqrt(ms + EPS)
        _store_slabs(h_ref, slice(r, r + NORM_ROWS), (xn * gain + shift).astype(BF16))


def _store_slabs(ref, rows, value):
    width = ref.shape[-1]
    for sb in range(ref.shape[0]):
        ref[sb, rows, :] = value[:, sb * width:(sb + 1) * width]


def _load_slabs(ref, rows=slice(None)):
    return jnp.concatenate([ref[sb, rows, :] for sb in range(ref.shape[0])], axis=1)


def _split_bf16(a):
    hi = a.astype(BF16)
    lo = (a - hi.astype(F32)).astype(BF16)
    return hi, lo


def _fuse_kernel(wu_ref, wzin_ref, wg_ref, asc_ref, woutin_ref, wv_ref, wz_ref, wout_ref):
    a_hi, a_lo = _split_bf16(wu_ref[0])
    b_hi, b_lo = _split_bf16(wg_ref[0, 0])
    acc = jnp.dot(a_hi, b_hi, preferred_element_type=F32)
    acc = acc + jnp.dot(a_hi, b_lo, preferred_element_type=F32)
    acc = acc + jnp.dot(a_lo, b_hi, preferred_element_type=F32)
    _store_slabs(wv_ref.at[0], slice(None), (acc * asc_ref[0]).astype(BF16))
    _store_slabs(wz_ref.at[0], slice(None), wzin_ref[0].astype(BF16))
    wout_ref[0, 0] = woutin_ref[0].astype(BF16)


def _fuse_group_weights(a_w_in, a_w_group, a_scale, a_w_out):
    n_a, d, _ = a_w_in.shape
    assert d // K_SLAB == N_GROUPS
    group_slabs = GROUP_WIDTH // K_SLAB
    slab_block = pl.BlockSpec((1, group_slabs, d, K_SLAB), lambda l, g: (l, g, 0, 0))
    slab_shape = jax.ShapeDtypeStruct((n_a, A_WIDTH // K_SLAB, d, K_SLAB), BF16)
    return pl.pallas_call(
        _fuse_kernel,
        grid=(n_a, N_GROUPS),
        in_specs=[
            pl.BlockSpec((1, d, GROUP_WIDTH), lambda l, g: (l, 0, g)),
            pl.BlockSpec((1, d, GROUP_WIDTH), lambda l, g: (l, 0, N_GROUPS + g)),
            pl.BlockSpec((1, 1, GROUP_WIDTH, GROUP_WIDTH), lambda l, g: (l, g, 0, 0)),
            pl.BlockSpec((1, 1, GROUP_WIDTH), lambda l, g: (l, 0, g)),
            pl.BlockSpec((1, A_WIDTH, K_SLAB), lambda l, g: (l, 0, g)),
        ],
        out_specs=[slab_block, slab_block,
                   pl.BlockSpec((1, 1, A_WIDTH, K_SLAB), lambda l, g: (l, g, 0, 0))],
        out_shape=[slab_shape, slab_shape,
                   jax.ShapeDtypeStruct((n_a, d // K_SLAB, A_WIDTH, K_SLAB), BF16)],
        compiler_params=pltpu.CompilerParams(
            dimension_semantics=("arbitrary", "arbitrary"), vmem_limit_bytes=VMEM_LIMIT_BYTES),
        name="fuse_group_weights",
    )(a_w_in, a_w_in, a_w_group, a_scale.reshape(n_a, 1, A_WIDTH), a_w_out)


def _pool_kernel(x_ref, shift_ref, scale_ref, gate_ref, g_ref, wv_ref, wz_ref, wout_ref, *rest, emit_kv):
    if emit_kv:
        kshift_ref, kscale_ref, kg_ref, wk_ref, wvt_ref, o_ref, k_ref, vt_ref = rest[:8]
        hk_buf = rest[-1]
        rest = rest[:-1]
    else:
        o_ref = rest[0]
    h_buf, v_ext, z_buf, gated_buf, carry = rest[-5:]
    t = pl.program_id(1)
    ts = SEQ_TILE

    @pl.when(t == 0)
    def _():
        carry[...] = jnp.zeros_like(carry)

    _norm_modulate(x_ref, g_ref, shift_ref, scale_ref, h_buf, ts)
    h = _load_slabs(h_buf)
    group_slabs = GROUP_WIDTH // K_SLAB
    lane_slabs = GROUP_WIDTH // LANES

    for g, w in enumerate(POOL_WINDOWS):
        w_slabs = slice(g * group_slabs, (g + 1) * group_slabs)
        z = jnp.dot(h, _load_slabs(wz_ref.at[0, w_slabs]), preferred_element_type=F32)
        v = jnp.dot(h, _load_slabs(wv_ref.at[0, w_slabs]), preferred_element_type=F32)
        halo = MAX_HALO if w > SUBLANES else SUBLANES
        for sl in range(lane_slabs):
            lanes = slice(sl * LANES, (sl + 1) * LANES)
            z_buf[g, sl] = z[:, lanes]
            v_ext[g, sl, 0:MAX_HALO, :] = carry[g, sl]
            v_ext[g, sl, MAX_HALO:, :] = v[:, lanes]
            carry[g, sl] = v_ext[g, sl, ts:ts + MAX_HALO, :]
            for r in range(0, ts, POOL_ROWS):
                ev = v_ext[g, sl, MAX_HALO + r - halo:MAX_HALO + r + POOL_ROWS, :]
                s = ev
                k = 1
                while k < w:
                    s = s + pltpu.roll(s, k, 0)
                    k *= 2
                s = s[halo:]
                if r == 0:
                    pos = t * ts + lax.broadcasted_iota(jnp.int32, (POOL_ROWS, LANES), 0)
                    mean = s * (1.0 / jnp.minimum(pos + 1, w).astype(F32))
                else:
                    mean = s * (1.0 / w)
                zc = z_buf[g, sl, r:r + POOL_ROWS, :]
                col = g * GROUP_WIDTH + sl * LANES
                gated_buf[col // K_SLAB, r:r + POOL_ROWS, col % K_SLAB:col % K_SLAB + LANES] = (
                    (mean - ev[halo:]) * (zc * _sigmoid(zc))).astype(BF16)

    w_out = _load_slabs(wout_ref.at[0])
    if not emit_kv:
        y = jnp.dot(_load_slabs(gated_buf), w_out, preferred_element_type=F32)
        o_ref[0] = x_ref[0] + gate_ref[0] * y
        return

    def out_chunk(r):
        rows = slice(r, r + KV_CHUNK)
        y = jnp.dot(_load_slabs(gated_buf, rows), w_out, preferred_element_type=F32)
        o_ref[0, rows, :] = x_ref[0, rows, :] + gate_ref[0] * y

    def kv_chunk(r):
        rows = slice(r, r + KV_CHUNK)
        _norm_modulate(o_ref, kg_ref, kshift_ref, kscale_ref, hk_buf, KV_CHUNK, start=r)
        hk = _load_slabs(hk_buf, rows)
        k_ref[0, rows, :] = jnp.dot(hk, wk_ref[...], preferred_element_type=F32).astype(BF16)
        vt_ref[0, :, rows] = lax.dot_general(wvt_ref[...], hk, _NT_DIMS,
                                             preferred_element_type=F32).astype(BF16)

    starts = list(range(0, ts, KV_CHUNK))
    out_chunk(starts[0])
    for prev, cur in zip(starts[:-1], starts[1:]):
        out_chunk(cur)
        kv_chunk(prev)
    kv_chunk(starts[-1])


def _pool_layer(layer, x, shift, scale, gate, norm_g, w_v, w_z, w_out, kv=None):
    b, s, d = x.shape
    ts = SEQ_TILE
    vec_spec = pl.BlockSpec((1, 1, d), lambda i, j: (i, 0, 0))
    x_spec = pl.BlockSpec((1, ts, d), lambda i, j: (i, j, 0))
    def layer_slabs(rows, cols):
        return pl.BlockSpec((1, cols // K_SLAB, rows, K_SLAB), lambda i, j: (layer, 0, 0, 0),
                            pipeline_mode=pl.Buffered(1))

    in_specs = [
        x_spec,
        vec_spec, vec_spec, vec_spec,
        _const_spec((1, d)),
        layer_slabs(d, A_WIDTH),
        layer_slabs(d, A_WIDTH),
        layer_slabs(A_WIDTH, d),
    ]
    out_specs = x_spec
    out_shape = jax.ShapeDtypeStruct((b, s, d), F32)
    args = (x, shift, scale, gate, norm_g, w_v, w_z, w_out)
    if kv is not None:
        in_specs += [vec_spec, vec_spec, _const_spec((1, d)), _const_spec((d, KV_WIDTH)),
                     _const_spec((KV_WIDTH, d))]
        out_specs = [x_spec,
                     pl.BlockSpec((1, ts, KV_WIDTH), lambda i, j: (i, j, 0)),
                     pl.BlockSpec((1, KV_WIDTH, ts), lambda i, j: (i, 0, j))]
        out_shape = [out_shape,
                     jax.ShapeDtypeStruct((b, s, KV_WIDTH), BF16),
                     jax.ShapeDtypeStruct((b, KV_WIDTH, s), BF16)]
        args += tuple(kv)
    lane_slabs = GROUP_WIDTH // LANES
    scratch_shapes = [
        pltpu.VMEM((d // K_SLAB, ts, K_SLAB), BF16),
        pltpu.VMEM((N_GROUPS, lane_slabs, ts + MAX_HALO, LANES), F32),
        pltpu.VMEM((N_GROUPS, lane_slabs, ts, LANES), F32),
        pltpu.VMEM((A_WIDTH // K_SLAB, ts, K_SLAB), BF16),
        pltpu.VMEM((N_GROUPS, lane_slabs, MAX_HALO, LANES), F32),
    ]
    if kv is not None:
        scratch_shapes.append(pltpu.VMEM((d // K_SLAB, ts, K_SLAB), BF16))
    return pl.pallas_call(
        functools.partial(_pool_kernel, emit_kv=kv is not None),
        grid=(b, s // ts),
        in_specs=in_specs,
        out_specs=out_specs,
        out_shape=out_shape,
        scratch_shapes=scratch_shapes,
        compiler_params=pltpu.CompilerParams(
            dimension_semantics=("arbitrary", "arbitrary"), vmem_limit_bytes=VMEM_LIMIT_BYTES),
        name="pool_layer",
    )(*args)


def _t5_causal_buckets():
    i = np.arange(BLOCK)[:, None]
    j = np.arange(2 * BLOCK)[None, :]
    n = np.maximum(i + BLOCK - j, 0)
    max_exact = N_BUCKETS // 2
    large = max_exact + (np.log(np.maximum(n, 1) / max_exact) / math.log(MAX_DISTANCE / max_exact)
                         * (N_BUCKETS - max_exact)).astype(np.int32)
    large = np.minimum(large, N_BUCKETS - 1)
    return np.where(n < max_exact, n, large).astype(np.int32)


def _band_codes_t():
    i = np.arange(BLOCK)[None, :]
    j = np.arange(2 * BLOCK)[:, None]
    rel = i + BLOCK - j
    band = (rel >= 0) & (rel < BLOCK)
    return np.where(band, np.where(j < BLOCK, 1.0, 0.0), 2.0).astype(np.float32)


def _bias_kernel(bucket_ref, rel_ref, o_ref):
    bucket = bucket_ref[...]
    for head in range(N_HEADS):
        acc = jnp.zeros((2 * BLOCK, BLOCK), F32)
        for b in range(N_BUCKETS):
            acc = jnp.where(bucket == b, rel_ref[b, head], acc)
        kv_head, g = divmod(head, GQA_GROUPS)
        o_ref[kv_head, g] = acc * LOG2E


def _position_bias_t(rel_bias):
    buckets_t = jnp.asarray(np.ascontiguousarray(_t5_causal_buckets().T))
    return pl.pallas_call(
        _bias_kernel,
        in_specs=[
            pl.BlockSpec(memory_space=pltpu.VMEM),
            pl.BlockSpec(memory_space=pltpu.SMEM),
        ],
        out_specs=pl.BlockSpec(memory_space=pltpu.VMEM),
        out_shape=jax.ShapeDtypeStruct((N_KV_HEADS, GQA_GROUPS, 2 * BLOCK, BLOCK), F32),
        name="position_bias",
    )(buckets_t, rel_bias)


def _attn_kernel(x_ref, shift_ref, scale_ref, gate_ref, g_ref, wqt_ref, wz_ref, kp_ref, kc_ref,
                 vtp_ref, vtc_ref, bias_ref, code_ref, sink_ref, wout_ref, fg_ref, o_ref,
                 h_buf, qt_buf, z_buf, og_buf, s_buf, p_buf, r_buf, y_buf, *, final_norm):
    t = pl.program_id(1)
    n_chunks = ATTN_TILE // ATTN_CHUNK
    pieces = B_WIDTH // PROJ_PIECE
    steps_per_chunk = (ATTN_CHUNK // BLOCK) * N_KV_HEADS
    n_steps = n_chunks * steps_per_chunk
    norm_rows = ATTN_CHUNK // steps_per_chunk

    def norm_slice(c, j):
        _norm_modulate(x_ref, g_ref, shift_ref, scale_ref, h_buf, norm_rows,
                       start=c * ATTN_CHUNK + j * norm_rows)

    h_chunks = {}

    def project_piece(c, n):
        chunk = slice(c * ATTN_CHUNK, (c + 1) * ATTN_CHUNK)
        if c not in h_chunks:
            h_chunks[c] = _load_slabs(h_buf, chunk)
        h = h_chunks[c]
        if n < pieces:
            sl = slice(n * PROJ_PIECE, (n + 1) * PROJ_PIECE)
            qt = lax.dot_general(wqt_ref[0, sl, :], h, _NT_DIMS, preferred_element_type=F32)
            qt = qt.astype(BF16)
            for blk in range(ATTN_CHUNK // BLOCK):
                qt_buf[c * (ATTN_CHUNK // BLOCK) + blk, sl, :] = qt[:, blk * BLOCK:(blk + 1) * BLOCK]
        else:
            z = jnp.dot(h, wz_ref[0, n - pieces], preferred_element_type=F32)
            for sub in range(PROJ_PIECE // LANES):
                z_buf[(n - pieces) * (PROJ_PIECE // LANES) + sub, chunk, :] = z[:, sub * LANES:(sub + 1) * LANES]

    def out_piece(c, n):
        chunk = slice(c * ATTN_CHUNK, (c + 1) * ATTN_CHUNK)
        og = jnp.concatenate([og_buf[sb, chunk, :] for sb in range(B_WIDTH // LANES)], axis=1)
        y_buf[c % 2, n] = jnp.dot(og, wout_ref[0, n], preferred_element_type=F32)
        if n == pieces - 1:
            for r in range(0, ATTN_CHUNK, NORM_ROWS):
                rows = slice(c * ATTN_CHUNK + r, c * ATTN_CHUNK + r + NORM_ROWS)
                y = jnp.concatenate([y_buf[c % 2, m, r:r + NORM_ROWS, :] for m in range(pieces)], axis=1)
                out = x_ref[0, rows, :] + gate_ref[0] * y
                if final_norm:
                    ms = jnp.mean(out * out, axis=-1, keepdims=True)
                    out = (out * lax.rsqrt(ms + EPS)) * fg_ref[...]
                o_ref[0, rows, :] = out

    def stage_scores(step):
        i, kvh = divmod(step, N_KV_HEADS)
        _scores(i, kvh, qt_buf, kp_ref, kc_ref, s_buf.at[step % 2])

    def stage_softmax(step):
        i, kvh = divmod(step, N_KV_HEADS)
        _softmax(i, kvh, t, s_buf.at[step % 2], code_ref, bias_ref, sink_ref,
                 p_buf.at[step % 2], r_buf.at[step % 2])

    def stage_pv(step):
        i, kvh = divmod(step, N_KV_HEADS)
        _pv_gate(i, kvh, p_buf.at[step % 2], r_buf.at[step % 2], vtp_ref, vtc_ref, z_buf, og_buf)

    norm_slices = [(c, j) for c in range(n_chunks) for j in range(steps_per_chunk)]
    for c, j in norm_slices[:2 * steps_per_chunk]:
        norm_slice(c, j)
    for n in range(2 * pieces):
        project_piece(0, n)

    def out_slices(ko):
        co, no = divmod(ko, steps_per_chunk)
        for n in range(no * pieces // steps_per_chunk, (no + 1) * pieces // steps_per_chunk):
            out_piece(co, n)

    out_lag = steps_per_chunk + 2
    for k in range(n_steps + 2):
        c, j = divmod(k, steps_per_chunk)
        if k < n_steps:
            stage_scores(k)
        if 1 <= k <= n_steps:
            stage_softmax(k - 1)
        if 2 <= k <= n_steps + 1:
            stage_pv(k - 2)
        if c + 2 < n_chunks:
            norm_slice(c + 2, j)
        if c + 1 < n_chunks:
            for n in range(j * 2 * pieces // steps_per_chunk, (j + 1) * 2 * pieces // steps_per_chunk):
                project_piece(c + 1, n)
        if k >= out_lag:
            out_slices(k - out_lag)
    for ko in range(n_steps + 2 - out_lag, n_steps):
        out_slices(ko)


def _scores(i, kvh, qt_buf, kp_ref, kc_ref, s_ref):
    rows = slice(i * BLOCK, (i + 1) * BLOCK)
    k_prev = kp_ref[0] if i == 0 else kc_ref[0, (i - 1) * BLOCK:i * BLOCK, :]
    k2 = jnp.concatenate([k_prev, kc_ref[0, rows, :]], axis=0)
    q_cat = jnp.concatenate(
        [qt_buf[i, (kvh * GQA_GROUPS + g) * HEAD_DIM:(kvh * GQA_GROUPS + g + 1) * HEAD_DIM, :]
         for g in range(GQA_GROUPS)], axis=1)
    q_zeros = jnp.zeros((HEAD_DIM, GQA_GROUPS * BLOCK), BF16)
    q_sel = jnp.concatenate([q_cat, q_zeros] if kvh == 0 else [q_zeros, q_cat], axis=0)
    s = jnp.dot(k2, q_sel, preferred_element_type=F32)
    for g in range(GQA_GROUPS):
        s_ref[g] = s[:, g * BLOCK:(g + 1) * BLOCK]


def _softmax(i, kvh, t, s_ref, code_ref, bias_ref, sink_ref, p_ref, r_ref):
    limit = jnp.where(t == 0, 1.0, 2.0) if i == 0 else 2.0
    for g in range(GQA_GROUPS):
        sink = sink_ref[kvh, :, g * BLOCK:(g + 1) * BLOCK]
        sg = jnp.where(code_ref[...] < limit, s_ref[g] + bias_ref[kvh, g], NEG_INF)
        m = jnp.maximum(jnp.max(sg, axis=0, keepdims=True), sink)
        p_ref[g] = jnp.exp2(sg - m).astype(BF16)
        r_ref[g] = jnp.exp2(sink - m)


def _pv_gate(i, kvh, p_ref, r_ref, vtp_ref, vtc_ref, z_buf, og_buf):
    rows = slice(i * BLOCK, (i + 1) * BLOCK)
    vt_prev = vtp_ref[0] if i == 0 else vtc_ref[0, :, (i - 1) * BLOCK:i * BLOCK]
    vt2 = jnp.concatenate([vt_prev, vtc_ref[0, :, rows]], axis=1)
    ones_rows = (lax.broadcasted_iota(jnp.int32, (SUM_ROWS, 2 * BLOCK), 0) == 0).astype(BF16)
    vt = jnp.concatenate([vt2[kvh * HEAD_DIM:(kvh + 1) * HEAD_DIM, :], ones_rows], axis=0)
    pt = jnp.concatenate([p_ref[g] for g in range(GQA_GROUPS)], axis=1)
    ot = jnp.dot(vt, pt, preferred_element_type=F32)
    inv_den = [1.0 / (ot[HEAD_DIM:HEAD_DIM + 1, g * BLOCK:(g + 1) * BLOCK] + r_ref[g])
               for g in range(GQA_GROUPS)]
    for p in range(GQA_GROUPS // 2):
        pair_t = jnp.concatenate(
            [ot[:HEAD_DIM, g * BLOCK:(g + 1) * BLOCK] * inv_den[g] for g in (2 * p, 2 * p + 1)], axis=0)
        slab = kvh * (GQA_GROUPS // 2) + p
        z = z_buf[slab, rows, :]
        og_buf[slab, rows, :] = (pair_t.T * (z * _sigmoid(z))).astype(BF16)


def _attn_weights_kernel(wq_ref, wzin_ref, woutin_ref, wqt_ref, wz_ref, wout_ref):
    wqt_ref[0] = (wq_ref[0].T * (HEAD_DIM ** -0.5 * LOG2E)).astype(BF16)
    wz_ref[0, 0] = wzin_ref[0].astype(BF16)
    wout_ref[0, 0] = woutin_ref[0].astype(BF16)


def _attn_weights(b_w_in, b_w_out):
    n_b, d, _ = b_w_in.shape
    blocks = B_WIDTH // PROJ_PIECE
    slab_spec = pl.BlockSpec((1, 1, d, PROJ_PIECE), lambda l, j: (l, j, 0, 0))
    slab_shape = jax.ShapeDtypeStruct((n_b, blocks, d, PROJ_PIECE), BF16)
    return pl.pallas_call(
        _attn_weights_kernel,
        grid=(n_b, blocks),
        in_specs=[
            pl.BlockSpec((1, d, PROJ_PIECE), lambda l, j: (l, 0, j)),
            pl.BlockSpec((1, d, PROJ_PIECE), lambda l, j: (l, 0, blocks + j)),
            pl.BlockSpec((1, B_WIDTH, PROJ_PIECE), lambda l, j: (l, 0, j)),
        ],
        out_specs=[pl.BlockSpec((1, PROJ_PIECE, d), lambda l, j: (l, j, 0)), slab_spec, slab_spec],
        out_shape=[jax.ShapeDtypeStruct((n_b, B_WIDTH, d), BF16), slab_shape, slab_shape],
        compiler_params=pltpu.CompilerParams(
            dimension_semantics=("arbitrary", "arbitrary"), vmem_limit_bytes=VMEM_LIMIT_BYTES),
        name="attn_weights",
    )(b_w_in, b_w_in, b_w_out)


def _attn_layer(layer, x, shift, scale, gate, norm_g, w_q_t, w_z, k, v_t, bias, codes, sinks, w_out,
                final_g, final_norm):
    b, s, d = x.shape
    ts = ATTN_TILE
    blocks_per_tile = ts // BLOCK
    vec_spec = pl.BlockSpec((1, 1, d), lambda i, j: (i, 0, 0))
    score_shape = (2, GQA_GROUPS, 2 * BLOCK, BLOCK)
    weight_slabs = pl.BlockSpec((1, B_WIDTH // PROJ_PIECE, d, PROJ_PIECE), lambda i, j: (layer, 0, 0, 0),
                                pipeline_mode=pl.Buffered(1))

    def prev_block(j):
        return jnp.maximum(j * blocks_per_tile - 1, 0)

    return pl.pallas_call(
        functools.partial(_attn_kernel, final_norm=final_norm),
        grid=(b, s // ts),
        in_specs=[
            pl.BlockSpec((1, ts, d), lambda i, j: (i, j, 0)),
            vec_spec, vec_spec, vec_spec,
            _const_spec((1, d)),
            pl.BlockSpec((1, B_WIDTH, d), lambda i, j: (layer, 0, 0), pipeline_mode=pl.Buffered(1)),
            weight_slabs,
            pl.BlockSpec((1, BLOCK, KV_WIDTH), lambda i, j: (i, prev_block(j), 0)),
            pl.BlockSpec((1, ts, KV_WIDTH), lambda i, j: (i, j, 0)),
            pl.BlockSpec((1, KV_WIDTH, BLOCK), lambda i, j: (i, 0, prev_block(j))),
            pl.BlockSpec((1, KV_WIDTH, ts), lambda i, j: (i, 0, j)),
            _const_spec((N_KV_HEADS, GQA_GROUPS, 2 * BLOCK, BLOCK)),
            _const_spec((2 * BLOCK, BLOCK)),
            _const_spec((N_KV_HEADS, 1, GQA_GROUPS * BLOCK)),
            weight_slabs,
            _const_spec((1, d)),
        ],
        out_specs=pl.BlockSpec((1, ts, d), lambda i, j: (i, j, 0)),
        out_shape=jax.ShapeDtypeStruct((b, s, d), F32),
        scratch_shapes=[
            pltpu.VMEM((d // K_SLAB, ts, K_SLAB), BF16),
            pltpu.VMEM((blocks_per_tile, B_WIDTH, BLOCK), BF16),
            pltpu.VMEM((B_WIDTH // LANES, ts, LANES), F32),
            pltpu.VMEM((B_WIDTH // LANES, ts, LANES), BF16),
            pltpu.VMEM(score_shape, F32),
            pltpu.VMEM(score_shape, BF16),
            pltpu.VMEM((2, GQA_GROUPS, 1, BLOCK), F32),
            pltpu.VMEM((2, d // PROJ_PIECE, ATTN_CHUNK, PROJ_PIECE), F32),
        ],
        compiler_params=pltpu.CompilerParams(
            dimension_semantics=("arbitrary", "arbitrary"), vmem_limit_bytes=VMEM_LIMIT_BYTES),
        name="attn_layer",
    )(x, shift, scale, gate, norm_g, w_q_t, w_z, k, k, v_t, v_t, bias, codes, sinks, w_out, final_g)


def kernel(x, c, norm_g, ada_w, ada_b, a_w_in, a_w_group, a_scale, a_w_out, kv_norm_g, kv_ada_w,
           kv_ada_b, w_kv, b_w_in, b_sinks, b_w_out, rel_bias, final_g):
    b, s, d = x.shape
    depth = norm_g.shape[0]
    n_a = a_w_in.shape[0]
    assert (d, s % SEQ_TILE, s % ATTN_TILE) == (D_MODEL, 0, 0) and b <= SUBLANES

    c_pad = jnp.pad(c, ((0, SUBLANES - b), (0, 0)))
    mod = _modulation(c_pad, ada_w, ada_b.reshape(depth, 1, 3 * d))[:, :b]
    mod_kv = _modulation(c_pad, kv_ada_w[None], kv_ada_b.reshape(1, 1, 2 * d))[0, :b]

    def vec(m, k):
        return m[:, k * d:(k + 1) * d].reshape(b, 1, d)

    w_v, w_z, w_ao = _fuse_group_weights(a_w_in, a_w_group, a_scale, a_w_out)
    kv_args = (vec(mod_kv, 0), vec(mod_kv, 1), kv_norm_g.reshape(1, d),
               w_kv[:, :KV_WIDTH].astype(BF16), w_kv[:, KV_WIDTH:].T.astype(BF16))
    for l in range(n_a):
        res = _pool_layer(l, x, vec(mod[l], 0), vec(mod[l], 1), vec(mod[l], 2), norm_g[l].reshape(1, d),
                          w_v, w_z, w_ao,
                          kv=kv_args if l == n_a - 1 else None)
        x = res if l < n_a - 1 else res[0]
    k, v_t = res[1], res[2]
    bias = _position_bias_t(rel_bias)
    codes = jnp.asarray(_band_codes_t())
    w_q_t, w_qz, w_o = _attn_weights(b_w_in, b_w_out)
    for l in range(n_a, depth):
        j = l - n_a
        sinks = jnp.repeat(b_sinks[j] * LOG2E, BLOCK).reshape(N_KV_HEADS, 1, GQA_GROUPS * BLOCK)
        x = _attn_layer(j, x, vec(mod[l], 0), vec(mod[l], 1), vec(mod[l], 2), norm_g[l].reshape(1, d),
                        w_q_t, w_qz, k, v_t, bias, codes, sinks, w_o,
                        final_g.reshape(1, d), final_norm=(l == depth - 1))
    return x
```

```python
import functools
import math

import numpy as np
import jax
import jax.numpy as jnp
from jax import lax
from jax.experimental import pallas as pl
from jax.experimental.pallas import tpu as pltpu

F32 = jnp.float32
BF16 = jnp.bfloat16

D_MODEL = 1024
A_WIDTH = 2048
POOL_WINDOWS = (2, 4, 8, 16)
N_GROUPS = len(POOL_WINDOWS)
GROUP_WIDTH = A_WIDTH // N_GROUPS
HEAD_DIM = 64
N_HEADS = 16
N_KV_HEADS = 2
GQA_GROUPS = N_HEADS // N_KV_HEADS
KV_WIDTH = N_KV_HEADS * HEAD_DIM
B_WIDTH = N_HEADS * HEAD_DIM
BLOCK = 128
N_BUCKETS = 32
MAX_DISTANCE = 128
EPS = 1e-6
NEG_INF = -1e30
LOG2E = math.log2(math.e)

LANES = 128
SUBLANES = 8
VMEM_LIMIT_BYTES = 56 * 1024 * 1024

MAX_HALO = 16
SEQ_TILE = 1024
ATTN_TILE = 1024
ATTN_CHUNK = 256
PROJ_PIECE = 256
SUM_ROWS = 16
KV_CHUNK = 256
NORM_ROWS = 32
K_SLAB = 256
POOL_ROWS = 64

_NT_DIMS = (((1,), (1,)), ((), ()))


def _sigmoid(v):
    return 1.0 / (1.0 + jnp.exp(-v))


def _const_spec(shape):
    zeros = (0,) * len(shape)
    return pl.BlockSpec(shape, lambda *_: zeros, pipeline_mode=pl.Buffered(1))


def _mod_kernel(c_ref, w_ref, b_ref, o_ref):
    c = c_ref[...]
    c_act = c * _sigmoid(c)
    o_ref[0] = jnp.dot(c_act, w_ref[0], preferred_element_type=F32) + b_ref[0]


def _modulation(c_pad, w, b):
    n_layers, d, n = w.shape
    rows = c_pad.shape[0]
    nt = D_MODEL
    return pl.pallas_call(
        _mod_kernel,
        grid=(n_layers, n // nt),
        in_specs=[
            pl.BlockSpec((rows, d), lambda l, j: (0, 0)),
            pl.BlockSpec((1, d, nt), lambda l, j: (l, 0, j)),
            pl.BlockSpec((1, 1, nt), lambda l, j: (l, 0, j)),
        ],
        out_specs=pl.BlockSpec((1, rows, nt), lambda l, j: (l, 0, j)),
        out_shape=jax.ShapeDtypeStruct((n_layers, rows, n), F32),
        compiler_params=pltpu.CompilerParams(
            dimension_semantics=("arbitrary", "arbitrary"), vmem_limit_bytes=VMEM_LIMIT_BYTES),
        name="adaln_modulation",
    )(c_pad, w, b)


def _norm_modulate(x_ref, g_ref, shift_ref, scale_ref, h_ref, rows, start=0):
    gain = g_ref[...] * (1.0 + scale_ref[0])
    shift = shift_ref[0]
    for r in range(start, start + rows, NORM_ROWS):
        xv = x_ref[0, r:r + NORM_ROWS, :]
        ms = jnp.mean(xv * xv, axis=-1, keepdims=True)
        xn = xv * lax.rsqrt(ms + EPS)
        h_ref[r:r + NORM_ROWS, :] = (xn * gain + shift).astype(BF16)


def _store_slabs(ref, value):
    width = ref.shape[-1]
    for sb in range(ref.shape[0]):
        ref[sb] = value[:, sb * width:(sb + 1) * width]


def _load_slabs(ref):
    return jnp.concatenate([ref[sb] for sb in range(ref.shape[0])], axis=1)


def _split_bf16(a):
    hi = a.astype(BF16)
    lo = (a - hi.astype(F32)).astype(BF16)
    return hi, lo


def _fuse_kernel(wu_ref, wzin_ref, wg_ref, asc_ref, woutin_ref, wv_ref, wz_ref, wout_ref):
    a_hi, a_lo = _split_bf16(wu_ref[0])
    b_hi, b_lo = _split_bf16(wg_ref[0, 0])
    acc = jnp.dot(a_hi, b_hi, preferred_element_type=F32)
    acc = acc + jnp.dot(a_hi, b_lo, preferred_element_type=F32)
    acc = acc + jnp.dot(a_lo, b_hi, preferred_element_type=F32)
    _store_slabs(wv_ref.at[0], (acc * asc_ref[0]).astype(BF16))
    _store_slabs(wz_ref.at[0], wzin_ref[0].astype(BF16))
    wout_ref[0, 0] = woutin_ref[0].astype(BF16)


def _fuse_group_weights(a_w_in, a_w_group, a_scale, a_w_out):
    n_a, d, _ = a_w_in.shape
    assert d // K_SLAB == N_GROUPS
    group_slabs = GROUP_WIDTH // K_SLAB
    slab_block = pl.BlockSpec((1, group_slabs, d, K_SLAB), lambda l, g: (l, g, 0, 0))
    slab_shape = jax.ShapeDtypeStruct((n_a, A_WIDTH // K_SLAB, d, K_SLAB), BF16)
    return pl.pallas_call(
        _fuse_kernel,
        grid=(n_a, N_GROUPS),
        in_specs=[
            pl.BlockSpec((1, d, GROUP_WIDTH), lambda l, g: (l, 0, g)),
            pl.BlockSpec((1, d, GROUP_WIDTH), lambda l, g: (l, 0, N_GROUPS + g)),
            pl.BlockSpec((1, 1, GROUP_WIDTH, GROUP_WIDTH), lambda l, g: (l, g, 0, 0)),
            pl.BlockSpec((1, 1, GROUP_WIDTH), lambda l, g: (l, 0, g)),
            pl.BlockSpec((1, A_WIDTH, K_SLAB), lambda l, g: (l, 0, g)),
        ],
        out_specs=[slab_block, slab_block,
                   pl.BlockSpec((1, 1, A_WIDTH, K_SLAB), lambda l, g: (l, g, 0, 0))],
        out_shape=[slab_shape, slab_shape,
                   jax.ShapeDtypeStruct((n_a, d // K_SLAB, A_WIDTH, K_SLAB), BF16)],
        compiler_params=pltpu.CompilerParams(
            dimension_semantics=("arbitrary", "arbitrary"), vmem_limit_bytes=VMEM_LIMIT_BYTES),
        name="fuse_group_weights",
    )(a_w_in, a_w_in, a_w_group, a_scale.reshape(n_a, 1, A_WIDTH), a_w_out)


def _pool_kernel(x_ref, shift_ref, scale_ref, gate_ref, g_ref, wv_ref, wz_ref, wout_ref, *rest, emit_kv):
    if emit_kv:
        kshift_ref, kscale_ref, kg_ref, wk_ref, wvt_ref, o_ref, k_ref, vt_ref = rest[:8]
        hk_buf = rest[-1]
        rest = rest[:-1]
    else:
        o_ref = rest[0]
    h_buf, v_ext, z_buf, gated_buf, carry = rest[-5:]
    t = pl.program_id(1)
    ts = SEQ_TILE

    @pl.when(t == 0)
    def _():
        carry[...] = jnp.zeros_like(carry)

    _norm_modulate(x_ref, g_ref, shift_ref, scale_ref, h_buf, ts)
    h = h_buf[...]
    group_slabs = GROUP_WIDTH // K_SLAB

    for g, w in enumerate(POOL_WINDOWS):
        cols = slice(g * GROUP_WIDTH, (g + 1) * GROUP_WIDTH)
        w_slabs = slice(g * group_slabs, (g + 1) * group_slabs)
        z_buf[g] = jnp.dot(h, _load_slabs(wz_ref.at[0, w_slabs]), preferred_element_type=F32)
        v_ext[g, 0:MAX_HALO, :] = carry[g]
        v_ext[g, MAX_HALO:, :] = jnp.dot(h, _load_slabs(wv_ref.at[0, w_slabs]), preferred_element_type=F32)
        carry[g] = v_ext[g, ts:ts + MAX_HALO, :]

        halo = MAX_HALO if w > SUBLANES else SUBLANES
        for r in range(0, ts, POOL_ROWS):
            ev = v_ext[g, MAX_HALO + r - halo:MAX_HALO + r + POOL_ROWS, :]
            s = ev
            k = 1
            while k < w:
                s = s + pltpu.roll(s, k, 0)
                k *= 2
            s = s[halo:]
            if r == 0:
                pos = t * ts + lax.broadcasted_iota(jnp.int32, (POOL_ROWS, GROUP_WIDTH), 0)
                mean = s * (1.0 / jnp.minimum(pos + 1, w).astype(F32))
            else:
                mean = s * (1.0 / w)
            z = z_buf[g, r:r + POOL_ROWS, :]
            gated_buf[r:r + POOL_ROWS, cols] = ((mean - ev[halo:]) * (z * _sigmoid(z))).astype(BF16)

    w_out = _load_slabs(wout_ref.at[0])
    if not emit_kv:
        y = jnp.dot(gated_buf[...], w_out, preferred_element_type=F32)
        o_ref[0] = x_ref[0] + gate_ref[0] * y
        return

    def out_chunk(r):
        rows = slice(r, r + KV_CHUNK)
        y = jnp.dot(gated_buf[rows, :], w_out, preferred_element_type=F32)
        o_ref[0, rows, :] = x_ref[0, rows, :] + gate_ref[0] * y

    def kv_chunk(r):
        rows = slice(r, r + KV_CHUNK)
        _norm_modulate(o_ref, kg_ref, kshift_ref, kscale_ref, hk_buf, KV_CHUNK, start=r)
        hk = hk_buf[rows, :]
        k_ref[0, rows, :] = jnp.dot(hk, wk_ref[...], preferred_element_type=F32).astype(BF16)
        vt_ref[0, :, rows] = lax.dot_general(wvt_ref[...], hk, _NT_DIMS,
                                             preferred_element_type=F32).astype(BF16)

    starts = list(range(0, ts, KV_CHUNK))
    out_chunk(starts[0])
    for prev, cur in zip(starts[:-1], starts[1:]):
        out_chunk(cur)
        kv_chunk(prev)
    kv_chunk(starts[-1])


def _pool_layer(layer, x, shift, scale, gate, norm_g, w_v, w_z, w_out, kv=None):
    b, s, d = x.shape
    ts = SEQ_TILE
    vec_spec = pl.BlockSpec((1, 1, d), lambda i, j: (i, 0, 0))
    x_spec = pl.BlockSpec((1, ts, d), lambda i, j: (i, j, 0))
    def layer_slabs(rows, cols):
        return pl.BlockSpec((1, cols // K_SLAB, rows, K_SLAB), lambda i, j: (layer, 0, 0, 0),
                            pipeline_mode=pl.Buffered(1))

    in_specs = [
        x_spec,
        vec_spec, vec_spec, vec_spec,
        _const_spec((1, d)),
        layer_slabs(d, A_WIDTH),
        layer_slabs(d, A_WIDTH),
        layer_slabs(A_WIDTH, d),
    ]
    out_specs = x_spec
    out_shape = jax.ShapeDtypeStruct((b, s, d), F32)
    args = (x, shift, scale, gate, norm_g, w_v, w_z, w_out)
    if kv is not None:
        in_specs += [vec_spec, vec_spec, _const_spec((1, d)), _const_spec((d, KV_WIDTH)),
                     _const_spec((KV_WIDTH, d))]
        out_specs = [x_spec,
                     pl.BlockSpec((1, ts, KV_WIDTH), lambda i, j: (i, j, 0)),
                     pl.BlockSpec((1, KV_WIDTH, ts), lambda i, j: (i, 0, j))]
        out_shape = [out_shape,
                     jax.ShapeDtypeStruct((b, s, KV_WIDTH), BF16),
                     jax.ShapeDtypeStruct((b, KV_WIDTH, s), BF16)]
        args += tuple(kv)
    scratch_shapes = [
        pltpu.VMEM((ts, d), BF16),
        pltpu.VMEM((N_GROUPS, ts + MAX_HALO, GROUP_WIDTH), F32),
        pltpu.VMEM((N_GROUPS, ts, GROUP_WIDTH), F32),
        pltpu.VMEM((ts, A_WIDTH), BF16),
        pltpu.VMEM((N_GROUPS, MAX_HALO, GROUP_WIDTH), F32),
    ]
    if kv is not None:
        scratch_shapes.append(pltpu.VMEM((ts, d), BF16))
    return pl.pallas_call(
        functools.partial(_pool_kernel, emit_kv=kv is not None),
        grid=(b, s // ts),
        in_specs=in_specs,
        out_specs=out_specs,
        out_shape=out_shape,
        scratch_shapes=scratch_shapes,
        compiler_params=pltpu.CompilerParams(
            dimension_semantics=("arbitrary", "arbitrary"), vmem_limit_bytes=VMEM_LIMIT_BYTES),
        name="pool_layer",
    )(*args)


def _t5_causal_buckets():
    i = np.arange(BLOCK)[:, None]
    j = np.arange(2 * BLOCK)[None, :]
    n = np.maximum(i + BLOCK - j, 0)
    max_exact = N_BUCKETS // 2
    large = max_exact + (np.log(np.maximum(n, 1) / max_exact) / math.log(MAX_DISTANCE / max_exact)
                         * (N_BUCKETS - max_exact)).astype(np.int32)
    large = np.minimum(large, N_BUCKETS - 1)
    return np.where(n < max_exact, n, large).astype(np.int32)


def _band_codes_t():
    i = np.arange(BLOCK)[None, :]
    j = np.arange(2 * BLOCK)[:, None]
    rel = i + BLOCK - j
    band = (rel >= 0) & (rel < BLOCK)
    return np.where(band, np.where(j < BLOCK, 1.0, 0.0), 2.0).astype(np.float32)


def _bias_kernel(bucket_ref, rel_ref, o_ref):
    bucket = bucket_ref[...]
    for head in range(N_HEADS):
        acc = jnp.zeros((2 * BLOCK, BLOCK), F32)
        for b in range(N_BUCKETS):
            acc = jnp.where(bucket == b, rel_ref[b, head], acc)
        kv_head, g = divmod(head, GQA_GROUPS)
        o_ref[kv_head, g] = acc * LOG2E


def _position_bias_t(rel_bias):
    buckets_t = jnp.asarray(np.ascontiguousarray(_t5_causal_buckets().T))
    return pl.pallas_call(
        _bias_kernel,
        in_specs=[
            pl.BlockSpec(memory_space=pltpu.VMEM),
            pl.BlockSpec(memory_space=pltpu.SMEM),
        ],
        out_specs=pl.BlockSpec(memory_space=pltpu.VMEM),
        out_shape=jax.ShapeDtypeStruct((N_KV_HEADS, GQA_GROUPS, 2 * BLOCK, BLOCK), F32),
        name="position_bias",
    )(buckets_t, rel_bias)


def _attn_kernel(x_ref, shift_ref, scale_ref, gate_ref, g_ref, wqt_ref, wz_ref, kp_ref, kc_ref,
                 vtp_ref, vtc_ref, bias_ref, code_ref, sink_ref, wout_ref, fg_ref, o_ref,
                 h_buf, qt_buf, z_buf, og_buf, s_buf, p_buf, r_buf, y_buf, *, final_norm):
    t = pl.program_id(1)
    n_chunks = ATTN_TILE // ATTN_CHUNK
    pieces = B_WIDTH // PROJ_PIECE
    steps_per_chunk = (ATTN_CHUNK // BLOCK) * N_KV_HEADS
    n_steps = n_chunks * steps_per_chunk
    norm_rows = ATTN_CHUNK // steps_per_chunk

    def norm_slice(c, j):
        _norm_modulate(x_ref, g_ref, shift_ref, scale_ref, h_buf, norm_rows,
                       start=c * ATTN_CHUNK + j * norm_rows)

    h_chunks = {}

    def project_piece(c, n):
        chunk = slice(c * ATTN_CHUNK, (c + 1) * ATTN_CHUNK)
        if c not in h_chunks:
            h_chunks[c] = h_buf[chunk, :]
        h = h_chunks[c]
        if n < pieces:
            sl = slice(n * PROJ_PIECE, (n + 1) * PROJ_PIECE)
            qt = lax.dot_general(wqt_ref[0, sl, :], h, _NT_DIMS, preferred_element_type=F32)
            qt = qt.astype(BF16)
            for blk in range(ATTN_CHUNK // BLOCK):
                qt_buf[c * (ATTN_CHUNK // BLOCK) + blk, sl, :] = qt[:, blk * BLOCK:(blk + 1) * BLOCK]
        else:
            z = jnp.dot(h, wz_ref[0, n - pieces], preferred_element_type=F32)
            for sub in range(PROJ_PIECE // LANES):
                z_buf[(n - pieces) * (PROJ_PIECE // LANES) + sub, chunk, :] = z[:, sub * LANES:(sub + 1) * LANES]

    def out_piece(c, n):
        chunk = slice(c * ATTN_CHUNK, (c + 1) * ATTN_CHUNK)
        og = jnp.concatenate([og_buf[sb, chunk, :] for sb in range(B_WIDTH // LANES)], axis=1)
        y_buf[c % 2, n] = jnp.dot(og, wout_ref[0, n], preferred_element_type=F32)
        if n == pieces - 1:
            for r in range(0, ATTN_CHUNK, NORM_ROWS):
                rows = slice(c * ATTN_CHUNK + r, c * ATTN_CHUNK + r + NORM_ROWS)
                y = jnp.concatenate([y_buf[c % 2, m, r:r + NORM_ROWS, :] for m in range(pieces)], axis=1)
                out = x_ref[0, rows, :] + gate_ref[0] * y
                if final_norm:
                    ms = jnp.mean(out * out, axis=-1, keepdims=True)
                    out = (out * lax.rsqrt(ms + EPS)) * fg_ref[...]
                o_ref[0, rows, :] = out

    def stage_scores(step):
        i, kvh = divmod(step, N_KV_HEADS)
        _scores(i, kvh, qt_buf, kp_ref, kc_ref, s_buf.at[step % 2])

    def stage_softmax(step):
        i, kvh = divmod(step, N_KV_HEADS)
        _softmax(i, kvh, t, s_buf.at[step % 2], code_ref, bias_ref, sink_ref,
                 p_buf.at[step % 2], r_buf.at[step % 2])

    def stage_pv(step):
        i, kvh = divmod(step, N_KV_HEADS)
        _pv_gate(i, kvh, p_buf.at[step % 2], r_buf.at[step % 2], vtp_ref, vtc_ref, z_buf, og_buf)

    norm_slices = [(c, j) for c in range(n_chunks) for j in range(steps_per_chunk)]
    for c, j in norm_slices[:2 * steps_per_chunk]:
        norm_slice(c, j)
    for n in range(2 * pieces):
        project_piece(0, n)

    def out_slices(ko):
        co, no = divmod(ko, steps_per_chunk)
        for n in range(no * pieces // steps_per_chunk, (no + 1) * pieces // steps_per_chunk):
            out_piece(co, n)

    out_lag = steps_per_chunk + 2
    for k in range(n_steps + 2):
        c, j = divmod(k, steps_per_chunk)
        if k < n_steps:
            stage_scores(k)
        if 1 <= k <= n_steps:
            stage_softmax(k - 1)
        if 2 <= k <= n_steps + 1:
            stage_pv(k - 2)
        if c + 2 < n_chunks:
            norm_slice(c + 2, j)
        if c + 1 < n_chunks:
            for n in range(j * 2 * pieces // steps_per_chunk, (j + 1) * 2 * pieces // steps_per_chunk):
                project_piece(c + 1, n)
        if k >= out_lag:
            out_slices(k - out_lag)
    for ko in range(n_steps + 2 - out_lag, n_steps):
        out_slices(ko)


def _scores(i, kvh, qt_buf, kp_ref, kc_ref, s_ref):
    rows = slice(i * BLOCK, (i + 1) * BLOCK)
    k_prev = kp_ref[0] if i == 0 else kc_ref[0, (i - 1) * BLOCK:i * BLOCK, :]
    k2 = jnp.concatenate([k_prev, kc_ref[0, rows, :]], axis=0)
    q_cat = jnp.concatenate(
        [qt_buf[i, (kvh * GQA_GROUPS + g) * HEAD_DIM:(kvh * GQA_GROUPS + g + 1) * HEAD_DIM, :]
         for g in range(GQA_GROUPS)], axis=1)
    q_zeros = jnp.zeros((HEAD_DIM, GQA_GROUPS * BLOCK), BF16)
    q_sel = jnp.concatenate([q_cat, q_zeros] if kvh == 0 else [q_zeros, q_cat], axis=0)
    s = jnp.dot(k2, q_sel, preferred_element_type=F32)
    for g in range(GQA_GROUPS):
        s_ref[g] = s[:, g * BLOCK:(g + 1) * BLOCK]


def _softmax(i, kvh, t, s_ref, code_ref, bias_ref, sink_ref, p_ref, r_ref):
    limit = jnp.where(t == 0, 1.0, 2.0) if i == 0 else 2.0
    for g in range(GQA_GROUPS):
        sink = sink_ref[kvh, :, g * BLOCK:(g + 1) * BLOCK]
        sg = jnp.where(code_ref[...] < limit, s_ref[g] + bias_ref[kvh, g], NEG_INF)
        m = jnp.maximum(jnp.max(sg, axis=0, keepdims=True), sink)
        p_ref[g] = jnp.exp2(sg - m).astype(BF16)
        r_ref[g] = jnp.exp2(sink - m)


def _pv_gate(i, kvh, p_ref, r_ref, vtp_ref, vtc_ref, z_buf, og_buf):
    rows = slice(i * BLOCK, (i + 1) * BLOCK)
    vt_prev = vtp_ref[0] if i == 0 else vtc_ref[0, :, (i - 1) * BLOCK:i * BLOCK]
    vt2 = jnp.concatenate([vt_prev, vtc_ref[0, :, rows]], axis=1)
    ones_rows = (lax.broadcasted_iota(jnp.int32, (SUM_ROWS, 2 * BLOCK), 0) == 0).astype(BF16)
    vt = jnp.concatenate([vt2[kvh * HEAD_DIM:(kvh + 1) * HEAD_DIM, :], ones_rows], axis=0)
    pt = jnp.concatenate([p_ref[g] for g in range(GQA_GROUPS)], axis=1)
    ot = jnp.dot(vt, pt, preferred_element_type=F32)
    inv_den = [1.0 / (ot[HEAD_DIM:HEAD_DIM + 1, g * BLOCK:(g + 1) * BLOCK] + r_ref[g])
               for g in range(GQA_GROUPS)]
    for p in range(GQA_GROUPS // 2):
        pair_t = jnp.concatenate(
            [ot[:HEAD_DIM, g * BLOCK:(g + 1) * BLOCK] * inv_den[g] for g in (2 * p, 2 * p + 1)], axis=0)
        slab = kvh * (GQA_GROUPS // 2) + p
        z = z_buf[slab, rows, :]
        og_buf[slab, rows, :] = (pair_t.T * (z * _sigmoid(z))).astype(BF16)


def _attn_weights_kernel(wq_ref, wzin_ref, woutin_ref, wqt_ref, wz_ref, wout_ref):
    wqt_ref[0] = (wq_ref[0].T * (HEAD_DIM ** -0.5 * LOG2E)).astype(BF16)
    wz_ref[0, 0] = wzin_ref[0].astype(BF16)
    wout_ref[0, 0] = woutin_ref[0].astype(BF16)


def _attn_weights(b_w_in, b_w_out):
    n_b, d, _ = b_w_in.shape
    blocks = B_WIDTH // PROJ_PIECE
    slab_spec = pl.BlockSpec((1, 1, d, PROJ_PIECE), lambda l, j: (l, j, 0, 0))
    slab_shape = jax.ShapeDtypeStruct((n_b, blocks, d, PROJ_PIECE), BF16)
    return pl.pallas_call(
        _attn_weights_kernel,
        grid=(n_b, blocks),
        in_specs=[
            pl.BlockSpec((1, d, PROJ_PIECE), lambda l, j: (l, 0, j)),
            pl.BlockSpec((1, d, PROJ_PIECE), lambda l, j: (l, 0, blocks + j)),
            pl.BlockSpec((1, B_WIDTH, PROJ_PIECE), lambda l, j: (l, 0, j)),
        ],
        out_specs=[pl.BlockSpec((1, PROJ_PIECE, d), lambda l, j: (l, j, 0)), slab_spec, slab_spec],
        out_shape=[jax.ShapeDtypeStruct((n_b, B_WIDTH, d), BF16), slab_shape, slab_shape],
        compiler_params=pltpu.CompilerParams(
            dimension_semantics=("arbitrary", "arbitrary"), vmem_limit_bytes=VMEM_LIMIT_BYTES),
        name="attn_weights",
    )(b_w_in, b_w_in, b_w_out)


def _attn_layer(layer, x, shift, scale, gate, norm_g, w_q_t, w_z, k, v_t, bias, codes, sinks, w_out,
                final_g, final_norm):
    b, s, d = x.shape
    ts = ATTN_TILE
    blocks_per_tile = ts // BLOCK
    vec_spec = pl.BlockSpec((1, 1, d), lambda i, j: (i, 0, 0))
    score_shape = (2, GQA_GROUPS, 2 * BLOCK, BLOCK)
    weight_slabs = pl.BlockSpec((1, B_WIDTH // PROJ_PIECE, d, PROJ_PIECE), lambda i, j: (layer, 0, 0, 0),
                                pipeline_mode=pl.Buffered(1))

    def prev_block(j):
        return jnp.maximum(j * blocks_per_tile - 1, 0)

    return pl.pallas_call(
        functools.partial(_attn_kernel, final_norm=final_norm),
        grid=(b, s // ts),
        in_specs=[
            pl.BlockSpec((1, ts, d), lambda i, j: (i, j, 0)),
            vec_spec, vec_spec, vec_spec,
            _const_spec((1, d)),
            pl.BlockSpec((1, B_WIDTH, d), lambda i, j: (layer, 0, 0), pipeline_mode=pl.Buffered(1)),
            weight_slabs,
            pl.BlockSpec((1, BLOCK, KV_WIDTH), lambda i, j: (i, prev_block(j), 0)),
            pl.BlockSpec((1, ts, KV_WIDTH), lambda i, j: (i, j, 0)),
            pl.BlockSpec((1, KV_WIDTH, BLOCK), lambda i, j: (i, 0, prev_block(j))),
            pl.BlockSpec((1, KV_WIDTH, ts), lambda i, j: (i, 0, j)),
            _const_spec((N_KV_HEADS, GQA_GROUPS, 2 * BLOCK, BLOCK)),
            _const_spec((2 * BLOCK, BLOCK)),
            _const_spec((N_KV_HEADS, 1, GQA_GROUPS * BLOCK)),
            weight_slabs,
            _const_spec((1, d)),
        ],
        out_specs=pl.BlockSpec((1, ts, d), lambda i, j: (i, j, 0)),
        out_shape=jax.ShapeDtypeStruct((b, s, d), F32),
        scratch_shapes=[
            pltpu.VMEM((ts, d), BF16),
            pltpu.VMEM((blocks_per_tile, B_WIDTH, BLOCK), BF16),
            pltpu.VMEM((B_WIDTH // LANES, ts, LANES), F32),
            pltpu.VMEM((B_WIDTH // LANES, ts, LANES), BF16),
            pltpu.VMEM(score_shape, F32),
            pltpu.VMEM(score_shape, BF16),
            pltpu.VMEM((2, GQA_GROUPS, 1, BLOCK), F32),
            pltpu.VMEM((2, d // PROJ_PIECE, ATTN_CHUNK, PROJ_PIECE), F32),
        ],
        compiler_params=pltpu.CompilerParams(
            dimension_semantics=("arbitrary", "arbitrary"), vmem_limit_bytes=VMEM_LIMIT_BYTES),
        name="attn_layer",
    )(x, shift, scale, gate, norm_g, w_q_t, w_z, k, k, v_t, v_t, bias, codes, sinks, w_out, final_g)


def kernel(x, c, norm_g, ada_w, ada_b, a_w_in, a_w_group, a_scale, a_w_out, kv_norm_g, kv_ada_w,
           kv_ada_b, w_kv, b_w_in, b_sinks, b_w_out, rel_bias, final_g):
    b, s, d = x.shape
    depth = norm_g.shape[0]
    n_a = a_w_in.shape[0]
    assert (d, s % SEQ_TILE, s % ATTN_TILE) == (D_MODEL, 0, 0) and b <= SUBLANES

    c_pad = jnp.pad(c, ((0, SUBLANES - b), (0, 0)))
    mod = _modulation(c_pad, ada_w, ada_b.reshape(depth, 1, 3 * d))[:, :b]
    mod_kv = _modulation(c_pad, kv_ada_w[None], kv_ada_b.reshape(1, 1, 2 * d))[0, :b]

    def vec(m, k):
        return m[:, k * d:(k + 1) * d].reshape(b, 1, d)

    w_v, w_z, w_ao = _fuse_group_weights(a_w_in, a_w_group, a_scale, a_w_out)
    kv_args = (vec(mod_kv, 0), vec(mod_kv, 1), kv_norm_g.reshape(1, d),
               w_kv[:, :KV_WIDTH].astype(BF16), w_kv[:, KV_WIDTH:].T.astype(BF16))
    for l in range(n_a):
        res = _pool_layer(l, x, vec(mod[l], 0), vec(mod[l], 1), vec(mod[l], 2), norm_g[l].reshape(1, d),
                          w_v, w_z, w_ao,
                          kv=kv_args if l == n_a - 1 else None)
        x = res if l < n_a - 1 else res[0]
    k, v_t = res[1], res[2]
    bias = _position_bias_t(rel_bias)
    codes = jnp.asarray(_band_codes_t())
    w_q_t, w_qz, w_o = _attn_weights(b_w_in, b_w_out)
    for l in range(n_a, depth):
        j = l - n_a
        sinks = jnp.repeat(b_sinks[j] * LOG2E, BLOCK).reshape(N_KV_HEADS, 1, GQA_GROUPS * BLOCK)
        x = _attn_layer(j, x, vec(mod[l], 0), vec(mod[l], 1), vec(mod[l], 2), norm_g[l].reshape(1, d),
                        w_q_t, w_qz, k, v_t, bias, codes, sinks, w_o,
                        final_g.reshape(1, d), final_norm=(l == depth - 1))
    return x
```

```python
import functools
import math

import numpy as np
import jax
import jax.numpy as jnp
from jax import lax
from jax.experimental import pallas as pl
from jax.experimental.pallas import tpu as pltpu

F32 = jnp.float32
BF16 = jnp.bfloat16

D_MODEL = 1024
A_WIDTH = 2048
POOL_WINDOWS = (2, 4, 8, 16)
N_GROUPS = len(POOL_WINDOWS)
GROUP_WIDTH = A_WIDTH // N_GROUPS
HEAD_DIM = 64
N_HEADS = 16
N_KV_HEADS = 2
GQA_GROUPS = N_HEADS // N_KV_HEADS
KV_WIDTH = N_KV_HEADS * HEAD_DIM
B_WIDTH = N_HEADS * HEAD_DIM
BLOCK = 128
N_BUCKETS = 32
MAX_DISTANCE = 128
EPS = 1e-6
NEG_INF = -1e30
LOG2E = math.log2(math.e)

LANES = 128
SUBLANES = 8
VMEM_LIMIT_BYTES = 56 * 1024 * 1024

MAX_HALO = 16
SEQ_TILE = 1024
ATTN_TILE = 1024
ATTN_CHUNK = 256
PROJ_PIECE = 256
CODE_LEAD = SUBLANES
SCORE_LEAD = 2 * SUBLANES
SUM_ROWS = 16
KV_CHUNK = 256
NORM_ROWS = 32
POOL_ROWS = 64

_NT_DIMS = (((1,), (1,)), ((), ()))


def _sigmoid(v):
    return 1.0 / (1.0 + jnp.exp(-v))


def _const_spec(shape):
    zeros = (0,) * len(shape)
    return pl.BlockSpec(shape, lambda *_: zeros, pipeline_mode=pl.Buffered(1))


def _mod_kernel(c_ref, w_ref, b_ref, o_ref):
    c = c_ref[...]
    c_act = c * _sigmoid(c)
    o_ref[0] = jnp.dot(c_act, w_ref[0], preferred_element_type=F32) + b_ref[0]


def _modulation(c_pad, w, b):
    n_layers, d, n = w.shape
    rows = c_pad.shape[0]
    nt = D_MODEL
    return pl.pallas_call(
        _mod_kernel,
        grid=(n_layers, n // nt),
        in_specs=[
            pl.BlockSpec((rows, d), lambda l, j: (0, 0)),
            pl.BlockSpec((1, d, nt), lambda l, j: (l, 0, j)),
            pl.BlockSpec((1, 1, nt), lambda l, j: (l, 0, j)),
        ],
        out_specs=pl.BlockSpec((1, rows, nt), lambda l, j: (l, 0, j)),
        out_shape=jax.ShapeDtypeStruct((n_layers, rows, n), F32),
        compiler_params=pltpu.CompilerParams(
            dimension_semantics=("arbitrary", "arbitrary"), vmem_limit_bytes=VMEM_LIMIT_BYTES),
        name="adaln_modulation",
    )(c_pad, w, b)


def _norm_modulate(x_ref, g_ref, shift_ref, scale_ref, h_ref, rows, start=0):
    gain = g_ref[...] * (1.0 + scale_ref[0])
    shift = shift_ref[0]
    for r in range(start, start + rows, NORM_ROWS):
        xv = x_ref[0, r:r + NORM_ROWS, :]
        ms = jnp.mean(xv * xv, axis=-1, keepdims=True)
        xn = xv * lax.rsqrt(ms + EPS)
        h_ref[r:r + NORM_ROWS, :] = (xn * gain + shift).astype(BF16)


def _split_bf16(a):
    hi = a.astype(BF16)
    lo = (a - hi.astype(F32)).astype(BF16)
    return hi, lo


def _fuse_kernel(wu_ref, wzin_ref, wg_ref, asc_ref, wv_ref, wz_ref):
    a_hi, a_lo = _split_bf16(wu_ref[0])
    b_hi, b_lo = _split_bf16(wg_ref[0, 0])
    acc = jnp.dot(a_hi, b_hi, preferred_element_type=F32)
    acc = acc + jnp.dot(a_hi, b_lo, preferred_element_type=F32)
    acc = acc + jnp.dot(a_lo, b_hi, preferred_element_type=F32)
    wv_ref[0] = (acc * asc_ref[0]).astype(BF16)
    wz_ref[0] = wzin_ref[0].astype(BF16)


def _fuse_group_weights(a_w_in, a_w_group, a_scale):
    n_a, d, _ = a_w_in.shape
    col_block = pl.BlockSpec((1, d, GROUP_WIDTH), lambda l, g: (l, 0, g))
    return pl.pallas_call(
        _fuse_kernel,
        grid=(n_a, N_GROUPS),
        in_specs=[
            col_block,
            pl.BlockSpec((1, d, GROUP_WIDTH), lambda l, g: (l, 0, N_GROUPS + g)),
            pl.BlockSpec((1, 1, GROUP_WIDTH, GROUP_WIDTH), lambda l, g: (l, g, 0, 0)),
            pl.BlockSpec((1, 1, GROUP_WIDTH), lambda l, g: (l, 0, g)),
        ],
        out_specs=[col_block, col_block],
        out_shape=[jax.ShapeDtypeStruct((n_a, d, A_WIDTH), BF16)] * 2,
        compiler_params=pltpu.CompilerParams(
            dimension_semantics=("arbitrary", "arbitrary"), vmem_limit_bytes=VMEM_LIMIT_BYTES),
        name="fuse_group_weights",
    )(a_w_in, a_w_in, a_w_group, a_scale.reshape(n_a, 1, A_WIDTH))


def _pool_kernel(x_ref, shift_ref, scale_ref, gate_ref, g_ref, wv_ref, wz_ref, wout_ref, *rest, emit_kv):
    if emit_kv:
        kshift_ref, kscale_ref, kg_ref, wk_ref, wvt_ref, o_ref, k_ref, vt_ref = rest[:8]
        hk_buf = rest[-1]
        rest = rest[:-1]
    else:
        o_ref = rest[0]
    h_buf, v_ext, z_buf, gated_buf, carry = rest[-5:]
    t = pl.program_id(1)
    ts = SEQ_TILE

    @pl.when(t == 0)
    def _():
        carry[...] = jnp.zeros_like(carry)

    _norm_modulate(x_ref, g_ref, shift_ref, scale_ref, h_buf, ts)
    h = h_buf[...]

    for g, w in enumerate(POOL_WINDOWS):
        cols = slice(g * GROUP_WIDTH, (g + 1) * GROUP_WIDTH)
        z_buf[g] = jnp.dot(h, wz_ref[0, :, cols], preferred_element_type=F32)
        v_ext[g, 0:MAX_HALO, :] = carry[g]
        v_ext[g, MAX_HALO:, :] = jnp.dot(h, wv_ref[0, :, cols], preferred_element_type=F32)
        carry[g] = v_ext[g, ts:ts + MAX_HALO, :]

        halo = MAX_HALO if w > SUBLANES else SUBLANES
        for r in range(0, ts, POOL_ROWS):
            ev = v_ext[g, MAX_HALO + r - halo:MAX_HALO + r + POOL_ROWS, :]
            s = ev
            k = 1
            while k < w:
                s = s + pltpu.roll(s, k, 0)
                k *= 2
            s = s[halo:]
            if r == 0:
                pos = t * ts + lax.broadcasted_iota(jnp.int32, (POOL_ROWS, GROUP_WIDTH), 0)
                mean = s * (1.0 / jnp.minimum(pos + 1, w).astype(F32))
            else:
                mean = s * (1.0 / w)
            z = z_buf[g, r:r + POOL_ROWS, :]
            gated_buf[r:r + POOL_ROWS, cols] = ((mean - ev[halo:]) * (z * _sigmoid(z))).astype(BF16)

    if not emit_kv:
        y = jnp.dot(gated_buf[...], wout_ref[...], preferred_element_type=F32)
        o_ref[0] = x_ref[0] + gate_ref[0] * y
        return

    def out_chunk(r):
        rows = slice(r, r + KV_CHUNK)
        y = jnp.dot(gated_buf[rows, :], wout_ref[...], preferred_element_type=F32)
        o_ref[0, rows, :] = x_ref[0, rows, :] + gate_ref[0] * y

    def kv_chunk(r):
        rows = slice(r, r + KV_CHUNK)
        _norm_modulate(o_ref, kg_ref, kshift_ref, kscale_ref, hk_buf, KV_CHUNK, start=r)
        hk = hk_buf[rows, :]
        k_ref[0, rows, :] = jnp.dot(hk, wk_ref[...], preferred_element_type=F32).astype(BF16)
        vt_ref[0, :, rows] = lax.dot_general(wvt_ref[...], hk, _NT_DIMS,
                                             preferred_element_type=F32).astype(BF16)

    starts = list(range(0, ts, KV_CHUNK))
    out_chunk(starts[0])
    for prev, cur in zip(starts[:-1], starts[1:]):
        out_chunk(cur)
        kv_chunk(prev)
    kv_chunk(starts[-1])


def _pool_layer(layer, x, shift, scale, gate, norm_g, w_v, w_z, w_out, kv=None):
    b, s, d = x.shape
    ts = SEQ_TILE
    vec_spec = pl.BlockSpec((1, 1, d), lambda i, j: (i, 0, 0))
    x_spec = pl.BlockSpec((1, ts, d), lambda i, j: (i, j, 0))
    layer_weights = pl.BlockSpec((1, d, A_WIDTH), lambda i, j: (layer, 0, 0), pipeline_mode=pl.Buffered(1))
    in_specs = [
        x_spec,
        vec_spec, vec_spec, vec_spec,
        _const_spec((1, d)),
        layer_weights,
        layer_weights,
        _const_spec((A_WIDTH, d)),
    ]
    out_specs = x_spec
    out_shape = jax.ShapeDtypeStruct((b, s, d), F32)
    args = (x, shift, scale, gate, norm_g, w_v, w_z, w_out)
    if kv is not None:
        in_specs += [vec_spec, vec_spec, _const_spec((1, d)), _const_spec((d, KV_WIDTH)),
                     _const_spec((KV_WIDTH, d))]
        out_specs = [x_spec,
                     pl.BlockSpec((1, ts, KV_WIDTH), lambda i, j: (i, j, 0)),
                     pl.BlockSpec((1, KV_WIDTH, ts), lambda i, j: (i, 0, j))]
        out_shape = [out_shape,
                     jax.ShapeDtypeStruct((b, s, KV_WIDTH), BF16),
                     jax.ShapeDtypeStruct((b, KV_WIDTH, s), BF16)]
        args += tuple(kv)
    scratch_shapes = [
        pltpu.VMEM((ts, d), BF16),
        pltpu.VMEM((N_GROUPS, ts + MAX_HALO, GROUP_WIDTH), F32),
        pltpu.VMEM((N_GROUPS, ts, GROUP_WIDTH), F32),
        pltpu.VMEM((ts, A_WIDTH), BF16),
        pltpu.VMEM((N_GROUPS, MAX_HALO, GROUP_WIDTH), F32),
    ]
    if kv is not None:
        scratch_shapes.append(pltpu.VMEM((ts, d), BF16))
    return pl.pallas_call(
        functools.partial(_pool_kernel, emit_kv=kv is not None),
        grid=(b, s // ts),
        in_specs=in_specs,
        out_specs=out_specs,
        out_shape=out_shape,
        scratch_shapes=scratch_shapes,
        compiler_params=pltpu.CompilerParams(
            dimension_semantics=("arbitrary", "arbitrary"), vmem_limit_bytes=VMEM_LIMIT_BYTES),
        name="pool_layer",
    )(*args)


def _t5_causal_buckets():
    i = np.arange(BLOCK)[:, None]
    j = np.arange(2 * BLOCK)[None, :]
    n = np.maximum(i + BLOCK - j, 0)
    max_exact = N_BUCKETS // 2
    large = max_exact + (np.log(np.maximum(n, 1) / max_exact) / math.log(MAX_DISTANCE / max_exact)
                         * (N_BUCKETS - max_exact)).astype(np.int32)
    large = np.minimum(large, N_BUCKETS - 1)
    return np.where(n < max_exact, n, large).astype(np.int32)


def _band_codes_t():
    i = np.arange(BLOCK)[None, :]
    j = np.arange(2 * BLOCK)[:, None]
    rel = i + BLOCK - j
    band = (rel >= 0) & (rel < BLOCK)
    return np.where(band, np.where(j < BLOCK, 1.0, 0.0), 2.0).astype(np.float32)


def _bias_kernel(bucket_ref, rel_ref, o_ref):
    bucket = bucket_ref[...]
    for head in range(N_HEADS):
        acc = jnp.zeros((2 * BLOCK, BLOCK), F32)
        for b in range(N_BUCKETS):
            acc = jnp.where(bucket == b, rel_ref[b, head], acc)
        kv_head, g = divmod(head, GQA_GROUPS)
        o_ref[kv_head, g] = acc * LOG2E


def _position_bias_t(rel_bias):
    buckets_t = jnp.asarray(np.ascontiguousarray(_t5_causal_buckets().T))
    return pl.pallas_call(
        _bias_kernel,
        in_specs=[
            pl.BlockSpec(memory_space=pltpu.VMEM),
            pl.BlockSpec(memory_space=pltpu.SMEM),
        ],
        out_specs=pl.BlockSpec(memory_space=pltpu.VMEM),
        out_shape=jax.ShapeDtypeStruct((N_KV_HEADS, GQA_GROUPS, 2 * BLOCK, BLOCK), F32),
        name="position_bias",
    )(buckets_t, rel_bias)


def _attn_kernel(x_ref, shift_ref, scale_ref, gate_ref, g_ref, wqt_ref, wz_ref, kp_ref, kc_ref,
                 vtp_ref, vtc_ref, bias_ref, code_ref, sink_ref, wout_ref, fg_ref, o_ref,
                 h_buf, qt_buf, z_buf, og_buf, s_buf, p_buf, r_buf, y_buf, *, final_norm):
    t = pl.program_id(1)
    n_chunks = ATTN_TILE // ATTN_CHUNK
    pieces = B_WIDTH // PROJ_PIECE
    steps_per_chunk = (ATTN_CHUNK // BLOCK) * N_KV_HEADS
    n_steps = n_chunks * steps_per_chunk
    norm_rows = ATTN_CHUNK // steps_per_chunk

    def norm_slice(c, j):
        _norm_modulate(x_ref, g_ref, shift_ref, scale_ref, h_buf, norm_rows,
                       start=c * ATTN_CHUNK + j * norm_rows)

    h_chunks = {}

    def project_piece(c, n):
        chunk = slice(c * ATTN_CHUNK, (c + 1) * ATTN_CHUNK)
        if c not in h_chunks:
            h_chunks[c] = h_buf[chunk, :]
        h = h_chunks[c]
        if n < pieces:
            sl = slice(n * PROJ_PIECE, (n + 1) * PROJ_PIECE)
            qt = lax.dot_general(wqt_ref[0, sl, :], h, _NT_DIMS, preferred_element_type=F32)
            qt = qt.astype(BF16)
            for blk in range(ATTN_CHUNK // BLOCK):
                qt_buf[c * (ATTN_CHUNK // BLOCK) + blk, sl, :] = qt[:, blk * BLOCK:(blk + 1) * BLOCK]
        else:
            z = jnp.dot(h, wz_ref[0, n - pieces], preferred_element_type=F32)
            for sub in range(PROJ_PIECE // LANES):
                z_buf[(n - pieces) * (PROJ_PIECE // LANES) + sub, chunk, :] = z[:, sub * LANES:(sub + 1) * LANES]

    def out_piece(c, n):
        chunk = slice(c * ATTN_CHUNK, (c + 1) * ATTN_CHUNK)
        og = jnp.concatenate([og_buf[sb, chunk, :] for sb in range(B_WIDTH // LANES)], axis=1)
        y_buf[c % 2, n] = jnp.dot(og, wout_ref[0, n], preferred_element_type=F32)
        if n == pieces - 1:
            for r in range(0, ATTN_CHUNK, NORM_ROWS):
                rows = slice(c * ATTN_CHUNK + r, c * ATTN_CHUNK + r + NORM_ROWS)
                y = jnp.concatenate([y_buf[c % 2, m, r:r + NORM_ROWS, :] for m in range(pieces)], axis=1)
                out = x_ref[0, rows, :] + gate_ref[0] * y
                if final_norm:
                    ms = jnp.mean(out * out, axis=-1, keepdims=True)
                    out = (out * lax.rsqrt(ms + EPS)) * fg_ref[...]
                o_ref[0, rows, :] = out

    def stage_scores(step):
        i, kvh = divmod(step, N_KV_HEADS)
        _scores(i, kvh, qt_buf, kp_ref, kc_ref, s_buf.at[step % 2])

    def stage_softmax(step):
        i, kvh = divmod(step, N_KV_HEADS)
        _softmax(i, kvh, t, s_buf.at[step % 2], code_ref, bias_ref, sink_ref,
                 p_buf.at[step % 2], r_buf.at[step % 2])

    def stage_pv(step):
        i, kvh = divmod(step, N_KV_HEADS)
        _pv_gate(i, kvh, p_buf.at[step % 2], r_buf.at[step % 2], vtp_ref, vtc_ref, z_buf, og_buf)

    norm_slices = [(c, j) for c in range(n_chunks) for j in range(steps_per_chunk)]
    for c, j in norm_slices[:2 * steps_per_chunk]:
        norm_slice(c, j)
    for n in range(2 * pieces):
        project_piece(0, n)

    def out_slices(ko):
        co, no = divmod(ko, steps_per_chunk)
        for n in range(no * pieces // steps_per_chunk, (no + 1) * pieces // steps_per_chunk):
            out_piece(co, n)

    out_lag = steps_per_chunk + 2
    for k in range(n_steps + 2):
        c, j = divmod(k, steps_per_chunk)
        if k < n_steps:
            stage_scores(k)
        if 1 <= k <= n_steps:
            stage_softmax(k - 1)
        if 2 <= k <= n_steps + 1:
            stage_pv(k - 2)
        if c + 2 < n_chunks:
            norm_slice(c + 2, j)
        if c + 1 < n_chunks:
            for n in range(j * 2 * pieces // steps_per_chunk, (j + 1) * 2 * pieces // steps_per_chunk):
                project_piece(c + 1, n)
        if k >= out_lag:
            out_slices(k - out_lag)
    for ko in range(n_steps + 2 - out_lag, n_steps):
        out_slices(ko)


def _score_rows(g):
    return slice(SCORE_LEAD + g * 2 * BLOCK, SCORE_LEAD + (g + 1) * 2 * BLOCK)


def _scores(i, kvh, qt_buf, kp_ref, kc_ref, s_ref):
    rows = slice(i * BLOCK, (i + 1) * BLOCK)
    k_prev = kp_ref[0] if i == 0 else kc_ref[0, (i - 1) * BLOCK:i * BLOCK, :]
    k2 = jnp.concatenate([k_prev, kc_ref[0, rows, :]], axis=0)
    q_cat = jnp.concatenate(
        [qt_buf[i, (kvh * GQA_GROUPS + g) * HEAD_DIM:(kvh * GQA_GROUPS + g + 1) * HEAD_DIM, :]
         for g in range(GQA_GROUPS)], axis=1)
    q_zeros = jnp.zeros((HEAD_DIM, GQA_GROUPS * BLOCK), BF16)
    q_sel = jnp.concatenate([q_cat, q_zeros] if kvh == 0 else [q_zeros, q_cat], axis=0)
    s = jnp.dot(k2, q_sel, preferred_element_type=F32)
    for g in range(GQA_GROUPS):
        s_ref[_score_rows(g), :] = s[:, g * BLOCK:(g + 1) * BLOCK]


def _softmax(i, kvh, t, s_ref, code_ref, bias_ref, sink_ref, p_ref, r_ref):
    limit = jnp.where(t == 0, 1.0, 2.0) if i == 0 else 2.0
    for g in range(GQA_GROUPS):
        sink = sink_ref[kvh, :, g * BLOCK:(g + 1) * BLOCK]
        sg = jnp.where(code_ref[CODE_LEAD:, :] < limit, s_ref[_score_rows(g), :] + bias_ref[kvh, g], NEG_INF)
        m = jnp.maximum(jnp.max(sg, axis=0, keepdims=True), sink)
        p_ref[g] = jnp.exp2(sg - m).astype(BF16)
        r_ref[g] = jnp.exp2(sink - m)


def _pv_gate(i, kvh, p_ref, r_ref, vtp_ref, vtc_ref, z_buf, og_buf):
    rows = slice(i * BLOCK, (i + 1) * BLOCK)
    vt_prev = vtp_ref[0] if i == 0 else vtc_ref[0, :, (i - 1) * BLOCK:i * BLOCK]
    vt2 = jnp.concatenate([vt_prev, vtc_ref[0, :, rows]], axis=1)
    ones_rows = (lax.broadcasted_iota(jnp.int32, (SUM_ROWS, 2 * BLOCK), 0) == 0).astype(BF16)
    vt = jnp.concatenate([vt2[kvh * HEAD_DIM:(kvh + 1) * HEAD_DIM, :], ones_rows], axis=0)
    pt = jnp.concatenate([p_ref[g] for g in range(GQA_GROUPS)], axis=1)
    ot = jnp.dot(vt, pt, preferred_element_type=F32)
    inv_den = [1.0 / (ot[HEAD_DIM:HEAD_DIM + 1, g * BLOCK:(g + 1) * BLOCK] + r_ref[g])
               for g in range(GQA_GROUPS)]
    for p in range(GQA_GROUPS // 2):
        pair_t = jnp.concatenate(
            [ot[:HEAD_DIM, g * BLOCK:(g + 1) * BLOCK] * inv_den[g] for g in (2 * p, 2 * p + 1)], axis=0)
        slab = kvh * (GQA_GROUPS // 2) + p
        z = z_buf[slab, rows, :]
        og_buf[slab, rows, :] = (pair_t.T * (z * _sigmoid(z))).astype(BF16)


def _attn_weights_kernel(wq_ref, wzin_ref, woutin_ref, wqt_ref, wz_ref, wout_ref):
    wqt_ref[0] = (wq_ref[0].T * (HEAD_DIM ** -0.5 * LOG2E)).astype(BF16)
    wz_ref[0, 0] = wzin_ref[0].astype(BF16)
    wout_ref[0, 0] = woutin_ref[0].astype(BF16)


def _attn_weights(b_w_in, b_w_out):
    n_b, d, _ = b_w_in.shape
    blocks = B_WIDTH // PROJ_PIECE
    slab_spec = pl.BlockSpec((1, 1, d, PROJ_PIECE), lambda l, j: (l, j, 0, 0))
    slab_shape = jax.ShapeDtypeStruct((n_b, blocks, d, PROJ_PIECE), BF16)
    return pl.pallas_call(
        _attn_weights_kernel,
        grid=(n_b, blocks),
        in_specs=[
            pl.BlockSpec((1, d, PROJ_PIECE), lambda l, j: (l, 0, j)),
            pl.BlockSpec((1, d, PROJ_PIECE), lambda l, j: (l, 0, blocks + j)),
            pl.BlockSpec((1, B_WIDTH, PROJ_PIECE), lambda l, j: (l, 0, j)),
        ],
        out_specs=[pl.BlockSpec((1, PROJ_PIECE, d), lambda l, j: (l, j, 0)), slab_spec, slab_spec],
        out_shape=[jax.ShapeDtypeStruct((n_b, B_WIDTH, d), BF16), slab_shape, slab_shape],
        compiler_params=pltpu.CompilerParams(
            dimension_semantics=("arbitrary", "arbitrary"), vmem_limit_bytes=VMEM_LIMIT_BYTES),
        name="attn_weights",
    )(b_w_in, b_w_in, b_w_out)


def _attn_layer(layer, x, shift, scale, gate, norm_g, w_q_t, w_z, k, v_t, bias, codes, sinks, w_out,
                final_g, final_norm):
    b, s, d = x.shape
    ts = ATTN_TILE
    blocks_per_tile = ts // BLOCK
    vec_spec = pl.BlockSpec((1, 1, d), lambda i, j: (i, 0, 0))
    score_shape = (2, GQA_GROUPS, 2 * BLOCK, BLOCK)
    weight_slabs = pl.BlockSpec((1, B_WIDTH // PROJ_PIECE, d, PROJ_PIECE), lambda i, j: (layer, 0, 0, 0),
                                pipeline_mode=pl.Buffered(1))

    def prev_block(j):
        return jnp.maximum(j * blocks_per_tile - 1, 0)

    return pl.pallas_call(
        functools.partial(_attn_kernel, final_norm=final_norm),
        grid=(b, s // ts),
        in_specs=[
            pl.BlockSpec((1, ts, d), lambda i, j: (i, j, 0)),
            vec_spec, vec_spec, vec_spec,
            _const_spec((1, d)),
            pl.BlockSpec((1, B_WIDTH, d), lambda i, j: (layer, 0, 0), pipeline_mode=pl.Buffered(1)),
            weight_slabs,
            pl.BlockSpec((1, BLOCK, KV_WIDTH), lambda i, j: (i, prev_block(j), 0)),
            pl.BlockSpec((1, ts, KV_WIDTH), lambda i, j: (i, j, 0)),
            pl.BlockSpec((1, KV_WIDTH, BLOCK), lambda i, j: (i, 0, prev_block(j))),
            pl.BlockSpec((1, KV_WIDTH, ts), lambda i, j: (i, 0, j)),
            _const_spec((N_KV_HEADS, GQA_GROUPS, 2 * BLOCK, BLOCK)),
            _const_spec((CODE_LEAD + 2 * BLOCK, BLOCK)),
            _const_spec((N_KV_HEADS, 1, GQA_GROUPS * BLOCK)),
            weight_slabs,
            _const_spec((1, d)),
        ],
        out_specs=pl.BlockSpec((1, ts, d), lambda i, j: (i, j, 0)),
        out_shape=jax.ShapeDtypeStruct((b, s, d), F32),
        scratch_shapes=[
            pltpu.VMEM((ts, d), BF16),
            pltpu.VMEM((blocks_per_tile, B_WIDTH, BLOCK), BF16),
            pltpu.VMEM((B_WIDTH // LANES, ts, LANES), F32),
            pltpu.VMEM((B_WIDTH // LANES, ts, LANES), BF16),
            pltpu.VMEM((2, GQA_GROUPS * 2 * BLOCK + 2 * SCORE_LEAD, BLOCK), F32),
            pltpu.VMEM(score_shape, BF16),
            pltpu.VMEM((2, GQA_GROUPS, 1, BLOCK), F32),
            pltpu.VMEM((2, d // PROJ_PIECE, ATTN_CHUNK, PROJ_PIECE), F32),
        ],
        compiler_params=pltpu.CompilerParams(
            dimension_semantics=("arbitrary", "arbitrary"), vmem_limit_bytes=VMEM_LIMIT_BYTES),
        name="attn_layer",
    )(x, shift, scale, gate, norm_g, w_q_t, w_z, k, k, v_t, v_t, bias, codes, sinks, w_out, final_g)


def kernel(x, c, norm_g, ada_w, ada_b, a_w_in, a_w_group, a_scale, a_w_out, kv_norm_g, kv_ada_w,
           kv_ada_b, w_kv, b_w_in, b_sinks, b_w_out, rel_bias, final_g):
    b, s, d = x.shape
    depth = norm_g.shape[0]
    n_a = a_w_in.shape[0]
    assert (d, s % SEQ_TILE, s % ATTN_TILE) == (D_MODEL, 0, 0) and b <= SUBLANES

    c_pad = jnp.pad(c, ((0, SUBLANES - b), (0, 0)))
    mod = _modulation(c_pad, ada_w, ada_b.reshape(depth, 1, 3 * d))[:, :b]
    mod_kv = _modulation(c_pad, kv_ada_w[None], kv_ada_b.reshape(1, 1, 2 * d))[0, :b]

    def vec(m, k):
        return m[:, k * d:(k + 1) * d].reshape(b, 1, d)

    w_v, w_z = _fuse_group_weights(a_w_in, a_w_group, a_scale)
    kv_args = (vec(mod_kv, 0), vec(mod_kv, 1), kv_norm_g.reshape(1, d),
               w_kv[:, :KV_WIDTH].astype(BF16), w_kv[:, KV_WIDTH:].T.astype(BF16))
    for l in range(n_a):
        res = _pool_layer(l, x, vec(mod[l], 0), vec(mod[l], 1), vec(mod[l], 2), norm_g[l].reshape(1, d),
                          w_v, w_z, a_w_out[l].astype(BF16),
                          kv=kv_args if l == n_a - 1 else None)
        x = res if l < n_a - 1 else res[0]
    k, v_t = res[1], res[2]
    bias = _position_bias_t(rel_bias)
    codes = jnp.asarray(np.pad(_band_codes_t(), ((CODE_LEAD, 0), (0, 0))))
    w_q_t, w_qz, w_o = _attn_weights(b_w_in, b_w_out)
    for l in range(n_a, depth):
        j = l - n_a
        sinks = jnp.repeat(b_sinks[j] * LOG2E, BLOCK).reshape(N_KV_HEADS, 1, GQA_GROUPS * BLOCK)
        x = _attn_layer(j, x, vec(mod[l], 0), vec(mod[l], 1), vec(mod[l], 2), norm_g[l].reshape(1, d),
                        w_q_t, w_qz, k, v_t, bias, codes, sinks, w_o,
                        final_g.reshape(1, d), final_norm=(l == depth - 1))
    return x
```

```python
import functools
import math

import numpy as np
import jax
import jax.numpy as jnp
from jax import lax
from jax.experimental import pallas as pl
from jax.experimental.pallas import tpu as pltpu

F32 = jnp.float32
BF16 = jnp.bfloat16

D_MODEL = 1024
A_WIDTH = 2048
POOL_WINDOWS = (2, 4, 8, 16)
N_GROUPS = len(POOL_WINDOWS)
GROUP_WIDTH = A_WIDTH // N_GROUPS
HEAD_DIM = 64
N_HEADS = 16
N_KV_HEADS = 2
GQA_GROUPS = N_HEADS // N_KV_HEADS
KV_WIDTH = N_KV_HEADS * HEAD_DIM
B_WIDTH = N_HEADS * HEAD_DIM
BLOCK = 128
N_BUCKETS = 32
MAX_DISTANCE = 128
EPS = 1e-6
NEG_INF = -1e30
LOG2E = math.log2(math.e)

LANES = 128
SUBLANES = 8
VMEM_LIMIT_BYTES = 56 * 1024 * 1024

MAX_HALO = 16
SEQ_TILE = 1024
ATTN_TILE = 1024
ATTN_CHUNK = 256
PROJ_PIECE = 256
SUM_ROWS = 16
KV_CHUNK = 256
NORM_ROWS = 32
POOL_ROWS = 64

_NT_DIMS = (((1,), (1,)), ((), ()))


def _sigmoid(v):
    return 1.0 / (1.0 + jnp.exp(-v))


def _const_spec(shape):
    zeros = (0,) * len(shape)
    return pl.BlockSpec(shape, lambda *_: zeros, pipeline_mode=pl.Buffered(1))


def _mod_kernel(c_ref, w_ref, b_ref, o_ref):
    c = c_ref[...]
    c_act = c * _sigmoid(c)
    o_ref[0] = jnp.dot(c_act, w_ref[0], preferred_element_type=F32) + b_ref[0]


def _modulation(c_pad, w, b):
    n_layers, d, n = w.shape
    rows = c_pad.shape[0]
    nt = D_MODEL
    return pl.pallas_call(
        _mod_kernel,
        grid=(n_layers, n // nt),
        in_specs=[
            pl.BlockSpec((rows, d), lambda l, j: (0, 0)),
            pl.BlockSpec((1, d, nt), lambda l, j: (l, 0, j)),
            pl.BlockSpec((1, 1, nt), lambda l, j: (l, 0, j)),
        ],
        out_specs=pl.BlockSpec((1, rows, nt), lambda l, j: (l, 0, j)),
        out_shape=jax.ShapeDtypeStruct((n_layers, rows, n), F32),
        compiler_params=pltpu.CompilerParams(
            dimension_semantics=("arbitrary", "arbitrary"), vmem_limit_bytes=VMEM_LIMIT_BYTES),
        name="adaln_modulation",
    )(c_pad, w, b)


def _norm_modulate(x_ref, g_ref, shift_ref, scale_ref, h_ref, rows, start=0):
    gain = g_ref[...] * (1.0 + scale_ref[0])
    shift = shift_ref[0]
    for r in range(start, start + rows, NORM_ROWS):
        xv = x_ref[0, r:r + NORM_ROWS, :]
        ms = jnp.mean(xv * xv, axis=-1, keepdims=True)
        xn = xv * lax.rsqrt(ms + EPS)
        h_ref[r:r + NORM_ROWS, :] = (xn * gain + shift).astype(BF16)


def _split_bf16(a):
    hi = a.astype(BF16)
    lo = (a - hi.astype(F32)).astype(BF16)
    return hi, lo


def _fuse_kernel(wu_ref, wzin_ref, wg_ref, asc_ref, wv_ref, wz_ref):
    a_hi, a_lo = _split_bf16(wu_ref[0])
    b_hi, b_lo = _split_bf16(wg_ref[0, 0])
    acc = jnp.dot(a_hi, b_hi, preferred_element_type=F32)
    acc = acc + jnp.dot(a_hi, b_lo, preferred_element_type=F32)
    acc = acc + jnp.dot(a_lo, b_hi, preferred_element_type=F32)
    wv_ref[0] = (acc * asc_ref[0]).astype(BF16)
    wz_ref[0] = wzin_ref[0].astype(BF16)


def _fuse_group_weights(a_w_in, a_w_group, a_scale):
    n_a, d, _ = a_w_in.shape
    col_block = pl.BlockSpec((1, d, GROUP_WIDTH), lambda l, g: (l, 0, g))
    return pl.pallas_call(
        _fuse_kernel,
        grid=(n_a, N_GROUPS),
        in_specs=[
            col_block,
            pl.BlockSpec((1, d, GROUP_WIDTH), lambda l, g: (l, 0, N_GROUPS + g)),
            pl.BlockSpec((1, 1, GROUP_WIDTH, GROUP_WIDTH), lambda l, g: (l, g, 0, 0)),
            pl.BlockSpec((1, 1, GROUP_WIDTH), lambda l, g: (l, 0, g)),
        ],
        out_specs=[col_block, col_block],
        out_shape=[jax.ShapeDtypeStruct((n_a, d, A_WIDTH), BF16)] * 2,
        compiler_params=pltpu.CompilerParams(
            dimension_semantics=("arbitrary", "arbitrary"), vmem_limit_bytes=VMEM_LIMIT_BYTES),
        name="fuse_group_weights",
    )(a_w_in, a_w_in, a_w_group, a_scale.reshape(n_a, 1, A_WIDTH))


def _pool_kernel(x_ref, shift_ref, scale_ref, gate_ref, g_ref, wv_ref, wz_ref, wout_ref, *rest, emit_kv):
    if emit_kv:
        kshift_ref, kscale_ref, kg_ref, wk_ref, wvt_ref, o_ref, k_ref, vt_ref = rest[:8]
        hk_buf = rest[-1]
        rest = rest[:-1]
    else:
        o_ref = rest[0]
    h_buf, v_ext, z_buf, gated_buf, carry = rest[-5:]
    t = pl.program_id(1)
    ts = SEQ_TILE

    @pl.when(t == 0)
    def _():
        carry[...] = jnp.zeros_like(carry)

    _norm_modulate(x_ref, g_ref, shift_ref, scale_ref, h_buf, ts)
    h = h_buf[...]

    for g, w in enumerate(POOL_WINDOWS):
        cols = slice(g * GROUP_WIDTH, (g + 1) * GROUP_WIDTH)
        z_buf[g] = jnp.dot(h, wz_ref[0, :, cols], preferred_element_type=F32)
        v_ext[g, 0:MAX_HALO, :] = carry[g]
        v_ext[g, MAX_HALO:, :] = jnp.dot(h, wv_ref[0, :, cols], preferred_element_type=F32)
        carry[g] = v_ext[g, ts:ts + MAX_HALO, :]

        halo = MAX_HALO if w > SUBLANES else SUBLANES
        for r in range(0, ts, POOL_ROWS):
            ev = v_ext[g, MAX_HALO + r - halo:MAX_HALO + r + POOL_ROWS, :]
            s = ev
            k = 1
            while k < w:
                s = s + pltpu.roll(s, k, 0)
                k *= 2
            s = s[halo:]
            if r == 0:
                pos = t * ts + lax.broadcasted_iota(jnp.int32, (POOL_ROWS, GROUP_WIDTH), 0)
                mean = s * (1.0 / jnp.minimum(pos + 1, w).astype(F32))
            else:
                mean = s * (1.0 / w)
            z = z_buf[g, r:r + POOL_ROWS, :]
            gated_buf[r:r + POOL_ROWS, cols] = ((mean - ev[halo:]) * (z * _sigmoid(z))).astype(BF16)

    if not emit_kv:
        y = jnp.dot(gated_buf[...], wout_ref[...], preferred_element_type=F32)
        o_ref[0] = x_ref[0] + gate_ref[0] * y
        return

    def out_chunk(r):
        rows = slice(r, r + KV_CHUNK)
        y = jnp.dot(gated_buf[rows, :], wout_ref[...], preferred_element_type=F32)
        o_ref[0, rows, :] = x_ref[0, rows, :] + gate_ref[0] * y

    def kv_chunk(r):
        rows = slice(r, r + KV_CHUNK)
        _norm_modulate(o_ref, kg_ref, kshift_ref, kscale_ref, hk_buf, KV_CHUNK, start=r)
        hk = hk_buf[rows, :]
        k_ref[0, rows, :] = jnp.dot(hk, wk_ref[...], preferred_element_type=F32).astype(BF16)
        vt_ref[0, :, rows] = lax.dot_general(wvt_ref[...], hk, _NT_DIMS,
                                             preferred_element_type=F32).astype(BF16)

    starts = list(range(0, ts, KV_CHUNK))
    out_chunk(starts[0])
    for prev, cur in zip(starts[:-1], starts[1:]):
        out_chunk(cur)
        kv_chunk(prev)
    kv_chunk(starts[-1])


def _pool_layer(layer, x, shift, scale, gate, norm_g, w_v, w_z, w_out, kv=None):
    b, s, d = x.shape
    ts = SEQ_TILE
    vec_spec = pl.BlockSpec((1, 1, d), lambda i, j: (i, 0, 0))
    x_spec = pl.BlockSpec((1, ts, d), lambda i, j: (i, j, 0))
    layer_weights = pl.BlockSpec((1, d, A_WIDTH), lambda i, j: (layer, 0, 0), pipeline_mode=pl.Buffered(1))
    in_specs = [
        x_spec,
        vec_spec, vec_spec, vec_spec,
        _const_spec((1, d)),
        layer_weights,
        layer_weights,
        _const_spec((A_WIDTH, d)),
    ]
    out_specs = x_spec
    out_shape = jax.ShapeDtypeStruct((b, s, d), F32)
    args = (x, shift, scale, gate, norm_g, w_v, w_z, w_out)
    if kv is not None:
        in_specs += [vec_spec, vec_spec, _const_spec((1, d)), _const_spec((d, KV_WIDTH)),
                     _const_spec((KV_WIDTH, d))]
        out_specs = [x_spec,
                     pl.BlockSpec((1, ts, KV_WIDTH), lambda i, j: (i, j, 0)),
                     pl.BlockSpec((1, KV_WIDTH, ts), lambda i, j: (i, 0, j))]
        out_shape = [out_shape,
                     jax.ShapeDtypeStruct((b, s, KV_WIDTH), BF16),
                     jax.ShapeDtypeStruct((b, KV_WIDTH, s), BF16)]
        args += tuple(kv)
    scratch_shapes = [
        pltpu.VMEM((ts, d), BF16),
        pltpu.VMEM((N_GROUPS, ts + MAX_HALO, GROUP_WIDTH), F32),
        pltpu.VMEM((N_GROUPS, ts, GROUP_WIDTH), F32),
        pltpu.VMEM((ts, A_WIDTH), BF16),
        pltpu.VMEM((N_GROUPS, MAX_HALO, GROUP_WIDTH), F32),
    ]
    if kv is not None:
        scratch_shapes.append(pltpu.VMEM((ts, d), BF16))
    return pl.pallas_call(
        functools.partial(_pool_kernel, emit_kv=kv is not None),
        grid=(b, s // ts),
        in_specs=in_specs,
        out_specs=out_specs,
        out_shape=out_shape,
        scratch_shapes=scratch_shapes,
        compiler_params=pltpu.CompilerParams(
            dimension_semantics=("arbitrary", "arbitrary"), vmem_limit_bytes=VMEM_LIMIT_BYTES),
        name="pool_layer",
    )(*args)


def _t5_causal_buckets():
    i = np.arange(BLOCK)[:, None]
    j = np.arange(2 * BLOCK)[None, :]
    n = np.maximum(i + BLOCK - j, 0)
    max_exact = N_BUCKETS // 2
    large = max_exact + (np.log(np.maximum(n, 1) / max_exact) / math.log(MAX_DISTANCE / max_exact)
                         * (N_BUCKETS - max_exact)).astype(np.int32)
    large = np.minimum(large, N_BUCKETS - 1)
    return np.where(n < max_exact, n, large).astype(np.int32)


def _band_codes_t():
    i = np.arange(BLOCK)[None, :]
    j = np.arange(2 * BLOCK)[:, None]
    rel = i + BLOCK - j
    band = (rel >= 0) & (rel < BLOCK)
    return np.where(band, np.where(j < BLOCK, 1.0, 0.0), 2.0).astype(np.float32)


def _bias_kernel(bucket_ref, rel_ref, o_ref):
    bucket = bucket_ref[...]
    for head in range(N_HEADS):
        acc = jnp.zeros((2 * BLOCK, BLOCK), F32)
        for b in range(N_BUCKETS):
            acc = jnp.where(bucket == b, rel_ref[b, head], acc)
        kv_head, g = divmod(head, GQA_GROUPS)
        o_ref[kv_head, g] = acc * LOG2E


def _position_bias_t(rel_bias):
    buckets_t = jnp.asarray(np.ascontiguousarray(_t5_causal_buckets().T))
    return pl.pallas_call(
        _bias_kernel,
        in_specs=[
            pl.BlockSpec(memory_space=pltpu.VMEM),
            pl.BlockSpec(memory_space=pltpu.SMEM),
        ],
        out_specs=pl.BlockSpec(memory_space=pltpu.VMEM),
        out_shape=jax.ShapeDtypeStruct((N_KV_HEADS, GQA_GROUPS, 2 * BLOCK, BLOCK), F32),
        name="position_bias",
    )(buckets_t, rel_bias)


def _attn_kernel(x_ref, shift_ref, scale_ref, gate_ref, g_ref, wqt_ref, wz_ref, kp_ref, kc_ref,
                 vtp_ref, vtc_ref, bias_ref, code_ref, sink_ref, wout_ref, fg_ref, o_ref,
                 h_buf, qt_buf, z_buf, og_buf, s_buf, p_buf, r_buf, y_buf, *, final_norm):
    t = pl.program_id(1)
    n_chunks = ATTN_TILE // ATTN_CHUNK
    pieces = B_WIDTH // PROJ_PIECE
    steps_per_chunk = (ATTN_CHUNK // BLOCK) * N_KV_HEADS
    n_steps = n_chunks * steps_per_chunk
    norm_rows = ATTN_CHUNK // steps_per_chunk

    def norm_slice(c, j):
        _norm_modulate(x_ref, g_ref, shift_ref, scale_ref, h_buf, norm_rows,
                       start=c * ATTN_CHUNK + j * norm_rows)

    h_chunks = {}
    og_chunks = {}

    def project_piece(c, n):
        chunk = slice(c * ATTN_CHUNK, (c + 1) * ATTN_CHUNK)
        if c not in h_chunks:
            h_chunks[c] = h_buf[chunk, :]
        h = h_chunks[c]
        if n < pieces:
            sl = slice(n * PROJ_PIECE, (n + 1) * PROJ_PIECE)
            qt = lax.dot_general(wqt_ref[0, sl, :], h, _NT_DIMS, preferred_element_type=F32)
            qt = qt.astype(BF16)
            for blk in range(ATTN_CHUNK // BLOCK):
                qt_buf[c * (ATTN_CHUNK // BLOCK) + blk, sl, :] = qt[:, blk * BLOCK:(blk + 1) * BLOCK]
        else:
            z = jnp.dot(h, wz_ref[0, n - pieces], preferred_element_type=F32)
            for sub in range(PROJ_PIECE // LANES):
                z_buf[(n - pieces) * (PROJ_PIECE // LANES) + sub, chunk, :] = z[:, sub * LANES:(sub + 1) * LANES]

    def out_piece(c, n):
        chunk = slice(c * ATTN_CHUNK, (c + 1) * ATTN_CHUNK)
        if c not in og_chunks:
            og_chunks[c] = jnp.concatenate([og_buf[sb, chunk, :] for sb in range(B_WIDTH // LANES)], axis=1)
        y_buf[c % 2, n] = jnp.dot(og_chunks[c], wout_ref[0, n], preferred_element_type=F32)
        if n == pieces - 1:
            for r in range(0, ATTN_CHUNK, NORM_ROWS):
                rows = slice(c * ATTN_CHUNK + r, c * ATTN_CHUNK + r + NORM_ROWS)
                y = jnp.concatenate([y_buf[c % 2, m, r:r + NORM_ROWS, :] for m in range(pieces)], axis=1)
                out = x_ref[0, rows, :] + gate_ref[0] * y
                if final_norm:
                    ms = jnp.mean(out * out, axis=-1, keepdims=True)
                    out = (out * lax.rsqrt(ms + EPS)) * fg_ref[...]
                o_ref[0, rows, :] = out

    def stage_scores(step):
        i, kvh = divmod(step, N_KV_HEADS)
        _scores(i, kvh, qt_buf, kp_ref, kc_ref, s_buf.at[step % 2])

    def stage_softmax(step):
        i, kvh = divmod(step, N_KV_HEADS)
        _softmax(i, kvh, t, s_buf.at[step % 2], code_ref, bias_ref, sink_ref,
                 p_buf.at[step % 2], r_buf.at[step % 2])

    def stage_pv(step):
        i, kvh = divmod(step, N_KV_HEADS)
        _pv_gate(i, kvh, p_buf.at[step % 2], r_buf.at[step % 2], vtp_ref, vtc_ref, z_buf, og_buf)

    norm_slices = [(c, j) for c in range(n_chunks) for j in range(steps_per_chunk)]
    for c, j in norm_slices[:2 * steps_per_chunk]:
        norm_slice(c, j)
    for n in range(2 * pieces):
        project_piece(0, n)

    def out_slices(ko):
        co, no = divmod(ko, steps_per_chunk)
        for n in range(no * pieces // steps_per_chunk, (no + 1) * pieces // steps_per_chunk):
            out_piece(co, n)

    out_lag = steps_per_chunk + 2
    for k in range(n_steps + 2):
        c, j = divmod(k, steps_per_chunk)
        if k < n_steps:
            stage_scores(k)
        if 1 <= k <= n_steps:
            stage_softmax(k - 1)
        if 2 <= k <= n_steps + 1:
            stage_pv(k - 2)
        if c + 2 < n_chunks:
            norm_slice(c + 2, j)
        if c + 1 < n_chunks:
            for n in range(j * 2 * pieces // steps_per_chunk, (j + 1) * 2 * pieces // steps_per_chunk):
                project_piece(c + 1, n)
        if k >= out_lag:
            out_slices(k - out_lag)
    for ko in range(n_steps + 2 - out_lag, n_steps):
        out_slices(ko)


def _scores(i, kvh, qt_buf, kp_ref, kc_ref, s_ref):
    rows = slice(i * BLOCK, (i + 1) * BLOCK)
    k_prev = kp_ref[0] if i == 0 else kc_ref[0, (i - 1) * BLOCK:i * BLOCK, :]
    k2 = jnp.concatenate([k_prev, kc_ref[0, rows, :]], axis=0)
    q_cat = jnp.concatenate(
        [qt_buf[i, (kvh * GQA_GROUPS + g) * HEAD_DIM:(kvh * GQA_GROUPS + g + 1) * HEAD_DIM, :]
         for g in range(GQA_GROUPS)], axis=1)
    q_zeros = jnp.zeros((HEAD_DIM, GQA_GROUPS * BLOCK), BF16)
    q_sel = jnp.concatenate([q_cat, q_zeros] if kvh == 0 else [q_zeros, q_cat], axis=0)
    s = jnp.dot(k2, q_sel, preferred_element_type=F32)
    for g in range(GQA_GROUPS):
        s_ref[g] = s[:, g * BLOCK:(g + 1) * BLOCK]


def _softmax(i, kvh, t, s_ref, code_ref, bias_ref, sink_ref, p_ref, r_ref):
    limit = jnp.where(t == 0, 1.0, 2.0) if i == 0 else 2.0
    for g in range(GQA_GROUPS):
        sink = sink_ref[kvh, :, g * BLOCK:(g + 1) * BLOCK]
        sg = jnp.where(code_ref[...] < limit, s_ref[g] + bias_ref[kvh, g], NEG_INF)
        m = jnp.maximum(jnp.max(sg, axis=0, keepdims=True), sink)
        p_ref[g] = jnp.exp2(sg - m).astype(BF16)
        r_ref[g] = jnp.exp2(sink - m)


def _pv_gate(i, kvh, p_ref, r_ref, vtp_ref, vtc_ref, z_buf, og_buf):
    rows = slice(i * BLOCK, (i + 1) * BLOCK)
    vt_prev = vtp_ref[0] if i == 0 else vtc_ref[0, :, (i - 1) * BLOCK:i * BLOCK]
    vt2 = jnp.concatenate([vt_prev, vtc_ref[0, :, rows]], axis=1)
    ones_rows = (lax.broadcasted_iota(jnp.int32, (SUM_ROWS, 2 * BLOCK), 0) == 0).astype(BF16)
    vt = jnp.concatenate([vt2[kvh * HEAD_DIM:(kvh + 1) * HEAD_DIM, :], ones_rows], axis=0)
    pt = jnp.concatenate([p_ref[g] for g in range(GQA_GROUPS)], axis=1)
    ot = jnp.dot(vt, pt, preferred_element_type=F32)
    inv_den = [1.0 / (ot[HEAD_DIM:HEAD_DIM + 1, g * BLOCK:(g + 1) * BLOCK] + r_ref[g])
               for g in range(GQA_GROUPS)]
    for p in range(GQA_GROUPS // 2):
        pair_t = jnp.concatenate(
            [ot[:HEAD_DIM, g * BLOCK:(g + 1) * BLOCK] * inv_den[g] for g in (2 * p, 2 * p + 1)], axis=0)
        slab = kvh * (GQA_GROUPS // 2) + p
        z = z_buf[slab, rows, :]
        og_buf[slab, rows, :] = (pair_t.T * (z * _sigmoid(z))).astype(BF16)


def _attn_weights_kernel(wq_ref, wzin_ref, woutin_ref, wqt_ref, wz_ref, wout_ref):
    wqt_ref[0] = (wq_ref[0].T * (HEAD_DIM ** -0.5 * LOG2E)).astype(BF16)
    wz_ref[0, 0] = wzin_ref[0].astype(BF16)
    wout_ref[0, 0] = woutin_ref[0].astype(BF16)


def _attn_weights(b_w_in, b_w_out):
    n_b, d, _ = b_w_in.shape
    blocks = B_WIDTH // PROJ_PIECE
    slab_spec = pl.BlockSpec((1, 1, d, PROJ_PIECE), lambda l, j: (l, j, 0, 0))
    slab_shape = jax.ShapeDtypeStruct((n_b, blocks, d, PROJ_PIECE), BF16)
    return pl.pallas_call(
        _attn_weights_kernel,
        grid=(n_b, blocks),
        in_specs=[
            pl.BlockSpec((1, d, PROJ_PIECE), lambda l, j: (l, 0, j)),
            pl.BlockSpec((1, d, PROJ_PIECE), lambda l, j: (l, 0, blocks + j)),
            pl.BlockSpec((1, B_WIDTH, PROJ_PIECE), lambda l, j: (l, 0, j)),
        ],
        out_specs=[pl.BlockSpec((1, PROJ_PIECE, d), lambda l, j: (l, j, 0)), slab_spec, slab_spec],
        out_shape=[jax.ShapeDtypeStruct((n_b, B_WIDTH, d), BF16), slab_shape, slab_shape],
        compiler_params=pltpu.CompilerParams(
            dimension_semantics=("arbitrary", "arbitrary"), vmem_limit_bytes=VMEM_LIMIT_BYTES),
        name="attn_weights",
    )(b_w_in, b_w_in, b_w_out)


def _attn_layer(layer, x, shift, scale, gate, norm_g, w_q_t, w_z, k, v_t, bias, codes, sinks, w_out,
                final_g, final_norm):
    b, s, d = x.shape
    ts = ATTN_TILE
    blocks_per_tile = ts // BLOCK
    vec_spec = pl.BlockSpec((1, 1, d), lambda i, j: (i, 0, 0))
    score_shape = (2, GQA_GROUPS, 2 * BLOCK, BLOCK)
    weight_slabs = pl.BlockSpec((1, B_WIDTH // PROJ_PIECE, d, PROJ_PIECE), lambda i, j: (layer, 0, 0, 0),
                                pipeline_mode=pl.Buffered(1))

    def prev_block(j):
        return jnp.maximum(j * blocks_per_tile - 1, 0)

    return pl.pallas_call(
        functools.partial(_attn_kernel, final_norm=final_norm),
        grid=(b, s // ts),
        in_specs=[
            pl.BlockSpec((1, ts, d), lambda i, j: (i, j, 0)),
            vec_spec, vec_spec, vec_spec,
            _const_spec((1, d)),
            pl.BlockSpec((1, B_WIDTH, d), lambda i, j: (layer, 0, 0), pipeline_mode=pl.Buffered(1)),
            weight_slabs,
            pl.BlockSpec((1, BLOCK, KV_WIDTH), lambda i, j: (i, prev_block(j), 0)),
            pl.BlockSpec((1, ts, KV_WIDTH), lambda i, j: (i, j, 0)),
            pl.BlockSpec((1, KV_WIDTH, BLOCK), lambda i, j: (i, 0, prev_block(j))),
            pl.BlockSpec((1, KV_WIDTH, ts), lambda i, j: (i, 0, j)),
            _const_spec((N_KV_HEADS, GQA_GROUPS, 2 * BLOCK, BLOCK)),
            _const_spec((2 * BLOCK, BLOCK)),
            _const_spec((N_KV_HEADS, 1, GQA_GROUPS * BLOCK)),
            weight_slabs,
            _const_spec((1, d)),
        ],
        out_specs=pl.BlockSpec((1, ts, d), lambda i, j: (i, j, 0)),
        out_shape=jax.ShapeDtypeStruct((b, s, d), F32),
        scratch_shapes=[
            pltpu.VMEM((ts, d), BF16),
            pltpu.VMEM((blocks_per_tile, B_WIDTH, BLOCK), BF16),
            pltpu.VMEM((B_WIDTH // LANES, ts, LANES), F32),
            pltpu.VMEM((B_WIDTH // LANES, ts, LANES), BF16),
            pltpu.VMEM(score_shape, F32),
            pltpu.VMEM(score_shape, BF16),
            pltpu.VMEM((2, GQA_GROUPS, 1, BLOCK), F32),
            pltpu.VMEM((2, d // PROJ_PIECE, ATTN_CHUNK, PROJ_PIECE), F32),
        ],
        compiler_params=pltpu.CompilerParams(
            dimension_semantics=("arbitrary", "arbitrary"), vmem_limit_bytes=VMEM_LIMIT_BYTES),
        name="attn_layer",
    )(x, shift, scale, gate, norm_g, w_q_t, w_z, k, k, v_t, v_t, bias, codes, sinks, w_out, final_g)


def kernel(x, c, norm_g, ada_w, ada_b, a_w_in, a_w_group, a_scale, a_w_out, kv_norm_g, kv_ada_w,
           kv_ada_b, w_kv, b_w_in, b_sinks, b_w_out, rel_bias, final_g):
    b, s, d = x.shape
    depth = norm_g.shape[0]
    n_a = a_w_in.shape[0]
    assert (d, s % SEQ_TILE, s % ATTN_TILE) == (D_MODEL, 0, 0) and b <= SUBLANES

    c_pad = jnp.pad(c, ((0, SUBLANES - b), (0, 0)))
    mod = _modulation(c_pad, ada_w, ada_b.reshape(depth, 1, 3 * d))[:, :b]
    mod_kv = _modulation(c_pad, kv_ada_w[None], kv_ada_b.reshape(1, 1, 2 * d))[0, :b]

    def vec(m, k):
        return m[:, k * d:(k + 1) * d].reshape(b, 1, d)

    w_v, w_z = _fuse_group_weights(a_w_in, a_w_group, a_scale)
    kv_args = (vec(mod_kv, 0), vec(mod_kv, 1), kv_norm_g.reshape(1, d),
               w_kv[:, :KV_WIDTH].astype(BF16), w_kv[:, KV_WIDTH:].T.astype(BF16))
    for l in range(n_a):
        res = _pool_layer(l, x, vec(mod[l], 0), vec(mod[l], 1), vec(mod[l], 2), norm_g[l].reshape(1, d),
                          w_v, w_z, a_w_out[l].astype(BF16),
                          kv=kv_args if l == n_a - 1 else None)
        x = res if l < n_a - 1 else res[0]
    k, v_t = res[1], res[2]
    bias = _position_bias_t(rel_bias)
    codes = jnp.asarray(_band_codes_t())
    w_q_t, w_qz, w_o = _attn_weights(b_w_in, b_w_out)
    for l in range(n_a, depth):
        j = l - n_a
        sinks = jnp.repeat(b_sinks[j] * LOG2E, BLOCK).reshape(N_KV_HEADS, 1, GQA_GROUPS * BLOCK)
        x = _attn_layer(j, x, vec(mod[l], 0), vec(mod[l], 1), vec(mod[l], 2), norm_g[l].reshape(1, d),
                        w_q_t, w_qz, k, v_t, bias, codes, sinks, w_o,
                        final_g.reshape(1, d), final_norm=(l == depth - 1))
    return x
```

```python
import functools
import math

import numpy as np
import jax
import jax.numpy as jnp
from jax import lax
from jax.experimental import pallas as pl
from jax.experimental.pallas import tpu as pltpu

F32 = jnp.float32
BF16 = jnp.bfloat16

D_MODEL = 1024
A_WIDTH = 2048
POOL_WINDOWS = (2, 4, 8, 16)
N_GROUPS = len(POOL_WINDOWS)
GROUP_WIDTH = A_WIDTH // N_GROUPS
HEAD_DIM = 64
N_HEADS = 16
N_KV_HEADS = 2
GQA_GROUPS = N_HEADS // N_KV_HEADS
KV_WIDTH = N_KV_HEADS * HEAD_DIM
B_WIDTH = N_HEADS * HEAD_DIM
BLOCK = 128
N_BUCKETS = 32
MAX_DISTANCE = 128
EPS = 1e-6
NEG_INF = -1e30
LOG2E = math.log2(math.e)

LANES = 128
SUBLANES = 8
VMEM_LIMIT_BYTES = 56 * 1024 * 1024

MAX_HALO = 16
SEQ_TILE = 1024
ATTN_TILE = 1024
ATTN_CHUNK = 256
PROJ_PIECE = 256
SUM_ROWS = 16
KV_CHUNK = 256
NORM_ROWS = 32
POOL_ROWS = 64

_NT_DIMS = (((1,), (1,)), ((), ()))


def _sigmoid(v):
    return 1.0 / (1.0 + jnp.exp(-v))


def _const_spec(shape):
    zeros = (0,) * len(shape)
    return pl.BlockSpec(shape, lambda *_: zeros, pipeline_mode=pl.Buffered(1))


def _mod_kernel(c_ref, w_ref, b_ref, o_ref):
    c = c_ref[...]
    c_act = c * _sigmoid(c)
    o_ref[0] = jnp.dot(c_act, w_ref[0], preferred_element_type=F32) + b_ref[0]


def _modulation(c_pad, w, b):
    n_layers, d, n = w.shape
    rows = c_pad.shape[0]
    nt = D_MODEL
    return pl.pallas_call(
        _mod_kernel,
        grid=(n_layers, n // nt),
        in_specs=[
            pl.BlockSpec((rows, d), lambda l, j: (0, 0)),
            pl.BlockSpec((1, d, nt), lambda l, j: (l, 0, j)),
            pl.BlockSpec((1, 1, nt), lambda l, j: (l, 0, j)),
        ],
        out_specs=pl.BlockSpec((1, rows, nt), lambda l, j: (l, 0, j)),
        out_shape=jax.ShapeDtypeStruct((n_layers, rows, n), F32),
        compiler_params=pltpu.CompilerParams(
            dimension_semantics=("arbitrary", "arbitrary"), vmem_limit_bytes=VMEM_LIMIT_BYTES),
        name="adaln_modulation",
    )(c_pad, w, b)


def _norm_modulate(x_ref, g_ref, shift_ref, scale_ref, h_ref, rows, start=0):
    gain = g_ref[...] * (1.0 + scale_ref[0])
    shift = shift_ref[0]
    for r in range(start, start + rows, NORM_ROWS):
        xv = x_ref[0, r:r + NORM_ROWS, :]
        ms = jnp.mean(xv * xv, axis=-1, keepdims=True)
        xn = xv * lax.rsqrt(ms + EPS)
        h_ref[r:r + NORM_ROWS, :] = (xn * gain + shift).astype(BF16)


def _split_bf16(a):
    hi = a.astype(BF16)
    lo = (a - hi.astype(F32)).astype(BF16)
    return hi, lo


def _fuse_kernel(wu_ref, wzin_ref, wg_ref, asc_ref, wv_ref, wz_ref):
    a_hi, a_lo = _split_bf16(wu_ref[0])
    b_hi, b_lo = _split_bf16(wg_ref[0, 0])
    acc = jnp.dot(a_hi, b_hi, preferred_element_type=F32)
    acc = acc + jnp.dot(a_hi, b_lo, preferred_element_type=F32)
    acc = acc + jnp.dot(a_lo, b_hi, preferred_element_type=F32)
    wv_ref[0] = (acc * asc_ref[0]).astype(BF16)
    wz_ref[0] = wzin_ref[0].astype(BF16)


def _fuse_group_weights(a_w_in, a_w_group, a_scale):
    n_a, d, _ = a_w_in.shape
    col_block = pl.BlockSpec((1, d, GROUP_WIDTH), lambda l, g: (l, 0, g))
    return pl.pallas_call(
        _fuse_kernel,
        grid=(n_a, N_GROUPS),
        in_specs=[
            col_block,
            pl.BlockSpec((1, d, GROUP_WIDTH), lambda l, g: (l, 0, N_GROUPS + g)),
            pl.BlockSpec((1, 1, GROUP_WIDTH, GROUP_WIDTH), lambda l, g: (l, g, 0, 0)),
            pl.BlockSpec((1, 1, GROUP_WIDTH), lambda l, g: (l, 0, g)),
        ],
        out_specs=[col_block, col_block],
        out_shape=[jax.ShapeDtypeStruct((n_a, d, A_WIDTH), BF16)] * 2,
        compiler_params=pltpu.CompilerParams(
            dimension_semantics=("arbitrary", "arbitrary"), vmem_limit_bytes=VMEM_LIMIT_BYTES),
        name="fuse_group_weights",
    )(a_w_in, a_w_in, a_w_group, a_scale.reshape(n_a, 1, A_WIDTH))


def _pool_kernel(x_ref, shift_ref, scale_ref, gate_ref, g_ref, wv_ref, wz_ref, wout_ref, *rest, emit_kv):
    if emit_kv:
        kshift_ref, kscale_ref, kg_ref, wk_ref, wvt_ref, o_ref, k_ref, vt_ref = rest[:8]
        hk_buf = rest[-1]
        rest = rest[:-1]
    else:
        o_ref = rest[0]
    h_buf, v_ext, z_buf, gated_buf, carry = rest[-5:]
    t = pl.program_id(1)
    ts = SEQ_TILE

    @pl.when(t == 0)
    def _():
        carry[...] = jnp.zeros_like(carry)

    _norm_modulate(x_ref, g_ref, shift_ref, scale_ref, h_buf, ts)
    h = h_buf[...]

    for g, w in enumerate(POOL_WINDOWS):
        cols = slice(g * GROUP_WIDTH, (g + 1) * GROUP_WIDTH)
        z_buf[g] = jnp.dot(h, wz_ref[0, :, cols], preferred_element_type=F32)
        v_ext[g, 0:MAX_HALO, :] = carry[g]
        v_ext[g, MAX_HALO:, :] = jnp.dot(h, wv_ref[0, :, cols], preferred_element_type=F32)
        carry[g] = v_ext[g, ts:ts + MAX_HALO, :]

        halo = MAX_HALO if w > SUBLANES else SUBLANES
        for r in range(0, ts, POOL_ROWS):
            ev = v_ext[g, MAX_HALO + r - halo:MAX_HALO + r + POOL_ROWS, :]
            s = ev
            k = 1
            while k < w:
                s = s + pltpu.roll(s, k, 0)
                k *= 2
            s = s[halo:]
            if r == 0:
                pos = t * ts + lax.broadcasted_iota(jnp.int32, (POOL_ROWS, GROUP_WIDTH), 0)
                mean = s * (1.0 / jnp.minimum(pos + 1, w).astype(F32))
            else:
                mean = s * (1.0 / w)
            z = z_buf[g, r:r + POOL_ROWS, :]
            gated_buf[r:r + POOL_ROWS, cols] = ((mean - ev[halo:]) * (z * _sigmoid(z))).astype(BF16)

    if not emit_kv:
        y = jnp.dot(gated_buf[...], wout_ref[...], preferred_element_type=F32)
        o_ref[0] = x_ref[0] + gate_ref[0] * y
        return

    def out_chunk(r):
        rows = slice(r, r + KV_CHUNK)
        y = jnp.dot(gated_buf[rows, :], wout_ref[...], preferred_element_type=F32)
        o_ref[0, rows, :] = x_ref[0, rows, :] + gate_ref[0] * y

    def kv_chunk(r):
        rows = slice(r, r + KV_CHUNK)
        _norm_modulate(o_ref, kg_ref, kshift_ref, kscale_ref, hk_buf, KV_CHUNK, start=r)
        hk = hk_buf[rows, :]
        k_ref[0, rows, :] = jnp.dot(hk, wk_ref[...], preferred_element_type=F32).astype(BF16)
        vt_ref[0, :, rows] = lax.dot_general(wvt_ref[...], hk, _NT_DIMS,
                                             preferred_element_type=F32).astype(BF16)

    starts = list(range(0, ts, KV_CHUNK))
    out_chunk(starts[0])
    for prev, cur in zip(starts[:-1], starts[1:]):
        out_chunk(cur)
        kv_chunk(prev)
    kv_chunk(starts[-1])


def _pool_layer(layer, x, shift, scale, gate, norm_g, w_v, w_z, w_out, kv=None):
    b, s, d = x.shape
    ts = SEQ_TILE
    vec_spec = pl.BlockSpec((1, 1, d), lambda i, j: (i, 0, 0))
    x_spec = pl.BlockSpec((1, ts, d), lambda i, j: (i, j, 0))
    layer_weights = pl.BlockSpec((1, d, A_WIDTH), lambda i, j: (layer, 0, 0), pipeline_mode=pl.Buffered(1))
    in_specs = [
        x_spec,
        vec_spec, vec_spec, vec_spec,
        _const_spec((1, d)),
        layer_weights,
        layer_weights,
        _const_spec((A_WIDTH, d)),
    ]
    out_specs = x_spec
    out_shape = jax.ShapeDtypeStruct((b, s, d), F32)
    args = (x, shift, scale, gate, norm_g, w_v, w_z, w_out)
    if kv is not None:
        in_specs += [vec_spec, vec_spec, _const_spec((1, d)), _const_spec((d, KV_WIDTH)),
                     _const_spec((KV_WIDTH, d))]
        out_specs = [x_spec,
                     pl.BlockSpec((1, ts, KV_WIDTH), lambda i, j: (i, j, 0)),
                     pl.BlockSpec((1, KV_WIDTH, ts), lambda i, j: (i, 0, j))]
        out_shape = [out_shape,
                     jax.ShapeDtypeStruct((b, s, KV_WIDTH), BF16),
                     jax.ShapeDtypeStruct((b, KV_WIDTH, s), BF16)]
        args += tuple(kv)
    scratch_shapes = [
        pltpu.VMEM((ts, d), BF16),
        pltpu.VMEM((N_GROUPS, ts + MAX_HALO, GROUP_WIDTH), F32),
        pltpu.VMEM((N_GROUPS, ts, GROUP_WIDTH), F32),
        pltpu.VMEM((ts, A_WIDTH), BF16),
        pltpu.VMEM((N_GROUPS, MAX_HALO, GROUP_WIDTH), F32),
    ]
    if kv is not None:
        scratch_shapes.append(pltpu.VMEM((ts, d), BF16))
    return pl.pallas_call(
        functools.partial(_pool_kernel, emit_kv=kv is not None),
        grid=(b, s // ts),
        in_specs=in_specs,
        out_specs=out_specs,
        out_shape=out_shape,
        scratch_shapes=scratch_shapes,
        compiler_params=pltpu.CompilerParams(
            dimension_semantics=("arbitrary", "arbitrary"), vmem_limit_bytes=VMEM_LIMIT_BYTES),
        name="pool_layer",
    )(*args)


def _t5_causal_buckets():
    i = np.arange(BLOCK)[:, None]
    j = np.arange(2 * BLOCK)[None, :]
    n = np.maximum(i + BLOCK - j, 0)
    max_exact = N_BUCKETS // 2
    large = max_exact + (np.log(np.maximum(n, 1) / max_exact) / math.log(MAX_DISTANCE / max_exact)
                         * (N_BUCKETS - max_exact)).astype(np.int32)
    large = np.minimum(large, N_BUCKETS - 1)
    return np.where(n < max_exact, n, large).astype(np.int32)


def _band_codes_t():
    i = np.arange(BLOCK)[None, :]
    j = np.arange(2 * BLOCK)[:, None]
    rel = i + BLOCK - j
    band = (rel >= 0) & (rel < BLOCK)
    return np.where(band, np.where(j < BLOCK, 1.0, 0.0), 2.0).astype(np.float32)


def _bias_kernel(bucket_ref, rel_ref, o_ref):
    bucket = bucket_ref[...]
    for head in range(N_HEADS):
        acc = jnp.zeros((2 * BLOCK, BLOCK), F32)
        for b in range(N_BUCKETS):
            acc = jnp.where(bucket == b, rel_ref[b, head], acc)
        kv_head, g = divmod(head, GQA_GROUPS)
        o_ref[kv_head, g] = acc * LOG2E


def _position_bias_t(rel_bias):
    buckets_t = jnp.asarray(np.ascontiguousarray(_t5_causal_buckets().T))
    return pl.pallas_call(
        _bias_kernel,
        in_specs=[
            pl.BlockSpec(memory_space=pltpu.VMEM),
            pl.BlockSpec(memory_space=pltpu.SMEM),
        ],
        out_specs=pl.BlockSpec(memory_space=pltpu.VMEM),
        out_shape=jax.ShapeDtypeStruct((N_KV_HEADS, GQA_GROUPS, 2 * BLOCK, BLOCK), F32),
        name="position_bias",
    )(buckets_t, rel_bias)


def _attn_kernel(x_ref, shift_ref, scale_ref, gate_ref, g_ref, wqt_ref, wz_ref, kp_ref, kc_ref,
                 vtp_ref, vtc_ref, bias_ref, code_ref, sink_ref, wout_ref, fg_ref, o_ref,
                 h_buf, qt_buf, z_buf, og_buf, s_buf, p_buf, r_buf, y_buf, *, final_norm):
    t = pl.program_id(1)
    n_chunks = ATTN_TILE // ATTN_CHUNK
    pieces = B_WIDTH // PROJ_PIECE
    steps_per_chunk = ATTN_CHUNK // BLOCK
    n_steps = n_chunks * steps_per_chunk
    norm_rows = ATTN_CHUNK // steps_per_chunk

    def norm_slice(c, j):
        _norm_modulate(x_ref, g_ref, shift_ref, scale_ref, h_buf, norm_rows,
                       start=c * ATTN_CHUNK + j * norm_rows)

    h_chunks = {}

    def project_piece(c, n):
        chunk = slice(c * ATTN_CHUNK, (c + 1) * ATTN_CHUNK)
        if c not in h_chunks:
            h_chunks[c] = h_buf[chunk, :]
        h = h_chunks[c]
        if n < pieces:
            sl = slice(n * PROJ_PIECE, (n + 1) * PROJ_PIECE)
            qt = lax.dot_general(wqt_ref[0, sl, :], h, _NT_DIMS, preferred_element_type=F32)
            qt = qt.astype(BF16)
            for blk in range(ATTN_CHUNK // BLOCK):
                qt_buf[c * (ATTN_CHUNK // BLOCK) + blk, sl, :] = qt[:, blk * BLOCK:(blk + 1) * BLOCK]
        else:
            z = jnp.dot(h, wz_ref[0, n - pieces], preferred_element_type=F32)
            for sub in range(PROJ_PIECE // LANES):
                z_buf[(n - pieces) * (PROJ_PIECE // LANES) + sub, chunk, :] = z[:, sub * LANES:(sub + 1) * LANES]

    def out_piece(c, n):
        chunk = slice(c * ATTN_CHUNK, (c + 1) * ATTN_CHUNK)
        og = jnp.concatenate([og_buf[sb, chunk, :] for sb in range(B_WIDTH // LANES)], axis=1)
        y_buf[c % 2, n] = jnp.dot(og, wout_ref[0, n], preferred_element_type=F32)
        if n == pieces - 1:
            for r in range(0, ATTN_CHUNK, NORM_ROWS):
                rows = slice(c * ATTN_CHUNK + r, c * ATTN_CHUNK + r + NORM_ROWS)
                y = jnp.concatenate([y_buf[c % 2, m, r:r + NORM_ROWS, :] for m in range(pieces)], axis=1)
                out = x_ref[0, rows, :] + gate_ref[0] * y
                if final_norm:
                    ms = jnp.mean(out * out, axis=-1, keepdims=True)
                    out = (out * lax.rsqrt(ms + EPS)) * fg_ref[...]
                o_ref[0, rows, :] = out

    def stage_scores(step):
        _scores(step, qt_buf, kp_ref, kc_ref, s_buf.at[step % 2])

    def stage_softmax(step):
        for kvh in range(N_KV_HEADS):
            _softmax(step, kvh, t, s_buf.at[step % 2, kvh], code_ref, bias_ref, sink_ref,
                     p_buf.at[step % 2, kvh], r_buf.at[step % 2, kvh])

    def stage_pv(step):
        for kvh in range(N_KV_HEADS):
            _pv_gate(step, kvh, p_buf.at[step % 2, kvh], r_buf.at[step % 2, kvh], vtp_ref, vtc_ref,
                     z_buf, og_buf)

    norm_slices = [(c, j) for c in range(n_chunks) for j in range(steps_per_chunk)]
    for c, j in norm_slices[:2 * steps_per_chunk]:
        norm_slice(c, j)
    for n in range(2 * pieces):
        project_piece(0, n)

    def out_slices(ko):
        co, no = divmod(ko, steps_per_chunk)
        for n in range(no * pieces // steps_per_chunk, (no + 1) * pieces // steps_per_chunk):
            out_piece(co, n)

    out_lag = steps_per_chunk + 2
    for k in range(n_steps + 2):
        c, j = divmod(k, steps_per_chunk)
        if k < n_steps:
            stage_scores(k)
        if 1 <= k <= n_steps:
            stage_softmax(k - 1)
        if 2 <= k <= n_steps + 1:
            stage_pv(k - 2)
        if c + 2 < n_chunks:
            norm_slice(c + 2, j)
        if c + 1 < n_chunks:
            for n in range(j * 2 * pieces // steps_per_chunk, (j + 1) * 2 * pieces // steps_per_chunk):
                project_piece(c + 1, n)
        if k >= out_lag:
            out_slices(k - out_lag)
    for ko in range(n_steps + 2 - out_lag, n_steps):
        out_slices(ko)


def _scores(i, qt_buf, kp_ref, kc_ref, s_ref):
    rows = slice(i * BLOCK, (i + 1) * BLOCK)
    k_prev = kp_ref[0] if i == 0 else kc_ref[0, (i - 1) * BLOCK:i * BLOCK, :]
    k2 = jnp.concatenate([k_prev, kc_ref[0, rows, :]], axis=0)
    q_zeros = jnp.zeros((HEAD_DIM, GQA_GROUPS * BLOCK), BF16)
    q_rows = []
    for kvh in range(N_KV_HEADS):
        q_cat = jnp.concatenate(
            [qt_buf[i, (kvh * GQA_GROUPS + g) * HEAD_DIM:(kvh * GQA_GROUPS + g + 1) * HEAD_DIM, :]
             for g in range(GQA_GROUPS)], axis=1)
        q_rows.append(jnp.concatenate([q_cat, q_zeros] if kvh == 0 else [q_zeros, q_cat], axis=1))
    s = jnp.dot(k2, jnp.concatenate(q_rows, axis=0), preferred_element_type=F32)
    for head in range(N_HEADS):
        s_ref[head // GQA_GROUPS, head % GQA_GROUPS] = s[:, head * BLOCK:(head + 1) * BLOCK]


def _softmax(i, kvh, t, s_ref, code_ref, bias_ref, sink_ref, p_ref, r_ref):
    limit = jnp.where(t == 0, 1.0, 2.0) if i == 0 else 2.0
    for g in range(GQA_GROUPS):
        sink = sink_ref[kvh, :, g * BLOCK:(g + 1) * BLOCK]
        sg = jnp.where(code_ref[...] < limit, s_ref[g] + bias_ref[kvh, g], NEG_INF)
        m = jnp.maximum(jnp.max(sg, axis=0, keepdims=True), sink)
        p_ref[g] = jnp.exp2(sg - m).astype(BF16)
        r_ref[g] = jnp.exp2(sink - m)


def _pv_gate(i, kvh, p_ref, r_ref, vtp_ref, vtc_ref, z_buf, og_buf):
    rows = slice(i * BLOCK, (i + 1) * BLOCK)
    vt_prev = vtp_ref[0] if i == 0 else vtc_ref[0, :, (i - 1) * BLOCK:i * BLOCK]
    vt2 = jnp.concatenate([vt_prev, vtc_ref[0, :, rows]], axis=1)
    ones_rows = (lax.broadcasted_iota(jnp.int32, (SUM_ROWS, 2 * BLOCK), 0) == 0).astype(BF16)
    vt = jnp.concatenate([vt2[kvh * HEAD_DIM:(kvh + 1) * HEAD_DIM, :], ones_rows], axis=0)
    pt = jnp.concatenate([p_ref[g] for g in range(GQA_GROUPS)], axis=1)
    ot = jnp.dot(vt, pt, preferred_element_type=F32)
    inv_den = [1.0 / (ot[HEAD_DIM:HEAD_DIM + 1, g * BLOCK:(g + 1) * BLOCK] + r_ref[g])
               for g in range(GQA_GROUPS)]
    for p in range(GQA_GROUPS // 2):
        pair_t = jnp.concatenate(
            [ot[:HEAD_DIM, g * BLOCK:(g + 1) * BLOCK] * inv_den[g] for g in (2 * p, 2 * p + 1)], axis=0)
        slab = kvh * (GQA_GROUPS // 2) + p
        z = z_buf[slab, rows, :]
        og_buf[slab, rows, :] = (pair_t.T * (z * _sigmoid(z))).astype(BF16)


def _attn_weights_kernel(wq_ref, wzin_ref, woutin_ref, wqt_ref, wz_ref, wout_ref):
    wqt_ref[0] = (wq_ref[0].T * (HEAD_DIM ** -0.5 * LOG2E)).astype(BF16)
    wz_ref[0, 0] = wzin_ref[0].astype(BF16)
    wout_ref[0, 0] = woutin_ref[0].astype(BF16)


def _attn_weights(b_w_in, b_w_out):
    n_b, d, _ = b_w_in.shape
    blocks = B_WIDTH // PROJ_PIECE
    slab_spec = pl.BlockSpec((1, 1, d, PROJ_PIECE), lambda l, j: (l, j, 0, 0))
    slab_shape = jax.ShapeDtypeStruct((n_b, blocks, d, PROJ_PIECE), BF16)
    return pl.pallas_call(
        _attn_weights_kernel,
        grid=(n_b, blocks),
        in_specs=[
            pl.BlockSpec((1, d, PROJ_PIECE), lambda l, j: (l, 0, j)),
            pl.BlockSpec((1, d, PROJ_PIECE), lambda l, j: (l, 0, blocks + j)),
            pl.BlockSpec((1, B_WIDTH, PROJ_PIECE), lambda l, j: (l, 0, j)),
        ],
        out_specs=[pl.BlockSpec((1, PROJ_PIECE, d), lambda l, j: (l, j, 0)), slab_spec, slab_spec],
        out_shape=[jax.ShapeDtypeStruct((n_b, B_WIDTH, d), BF16), slab_shape, slab_shape],
        compiler_params=pltpu.CompilerParams(
            dimension_semantics=("arbitrary", "arbitrary"), vmem_limit_bytes=VMEM_LIMIT_BYTES),
        name="attn_weights",
    )(b_w_in, b_w_in, b_w_out)


def _attn_layer(layer, x, shift, scale, gate, norm_g, w_q_t, w_z, k, v_t, bias, codes, sinks, w_out,
                final_g, final_norm):
    b, s, d = x.shape
    ts = ATTN_TILE
    blocks_per_tile = ts // BLOCK
    vec_spec = pl.BlockSpec((1, 1, d), lambda i, j: (i, 0, 0))
    score_shape = (2, N_KV_HEADS, GQA_GROUPS, 2 * BLOCK, BLOCK)
    weight_slabs = pl.BlockSpec((1, B_WIDTH // PROJ_PIECE, d, PROJ_PIECE), lambda i, j: (layer, 0, 0, 0),
                                pipeline_mode=pl.Buffered(1))

    def prev_block(j):
        return jnp.maximum(j * blocks_per_tile - 1, 0)

    return pl.pallas_call(
        functools.partial(_attn_kernel, final_norm=final_norm),
        grid=(b, s // ts),
        in_specs=[
            pl.BlockSpec((1, ts, d), lambda i, j: (i, j, 0)),
            vec_spec, vec_spec, vec_spec,
            _const_spec((1, d)),
            pl.BlockSpec((1, B_WIDTH, d), lambda i, j: (layer, 0, 0), pipeline_mode=pl.Buffered(1)),
            weight_slabs,
            pl.BlockSpec((1, BLOCK, KV_WIDTH), lambda i, j: (i, prev_block(j), 0)),
            pl.BlockSpec((1, ts, KV_WIDTH), lambda i, j: (i, j, 0)),
            pl.BlockSpec((1, KV_WIDTH, BLOCK), lambda i, j: (i, 0, prev_block(j))),
            pl.BlockSpec((1, KV_WIDTH, ts), lambda i, j: (i, 0, j)),
            _const_spec((N_KV_HEADS, GQA_GROUPS, 2 * BLOCK, BLOCK)),
            _const_spec((2 * BLOCK, BLOCK)),
            _const_spec((N_KV_HEADS, 1, GQA_GROUPS * BLOCK)),
            weight_slabs,
            _const_spec((1, d)),
        ],
        out_specs=pl.BlockSpec((1, ts, d), lambda i, j: (i, j, 0)),
        out_shape=jax.ShapeDtypeStruct((b, s, d), F32),
        scratch_shapes=[
            pltpu.VMEM((ts, d), BF16),
            pltpu.VMEM((blocks_per_tile, B_WIDTH, BLOCK), BF16),
            pltpu.VMEM((B_WIDTH // LANES, ts, LANES), F32),
            pltpu.VMEM((B_WIDTH // LANES, ts, LANES), BF16),
            pltpu.VMEM(score_shape, F32),
            pltpu.VMEM(score_shape, BF16),
            pltpu.VMEM((2, N_KV_HEADS, GQA_GROUPS, 1, BLOCK), F32),
            pltpu.VMEM((2, d // PROJ_PIECE, ATTN_CHUNK, PROJ_PIECE), F32),
        ],
        compiler_params=pltpu.CompilerParams(
            dimension_semantics=("arbitrary", "arbitrary"), vmem_limit_bytes=VMEM_LIMIT_BYTES),
        name="attn_layer",
    )(x, shift, scale, gate, norm_g, w_q_t, w_z, k, k, v_t, v_t, bias, codes, sinks, w_out, final_g)


def kernel(x, c, norm_g, ada_w, ada_b, a_w_in, a_w_group, a_scale, a_w_out, kv_norm_g, kv_ada_w,
           kv_ada_b, w_kv, b_w_in, b_sinks, b_w_out, rel_bias, final_g):
    b, s, d = x.shape
    depth = norm_g.shape[0]
    n_a = a_w_in.shape[0]
    assert (d, s % SEQ_TILE, s % ATTN_TILE) == (D_MODEL, 0, 0) and b <= SUBLANES

    c_pad = jnp.pad(c, ((0, SUBLANES - b), (0, 0)))
    mod = _modulation(c_pad, ada_w, ada_b.reshape(depth, 1, 3 * d))[:, :b]
    mod_kv = _modulation(c_pad, kv_ada_w[None], kv_ada_b.reshape(1, 1, 2 * d))[0, :b]

    def vec(m, k):
        return m[:, k * d:(k + 1) * d].reshape(b, 1, d)

    w_v, w_z = _fuse_group_weights(a_w_in, a_w_group, a_scale)
    kv_args = (vec(mod_kv, 0), vec(mod_kv, 1), kv_norm_g.reshape(1, d),
               w_kv[:, :KV_WIDTH].astype(BF16), w_kv[:, KV_WIDTH:].T.astype(BF16))
    for l in range(n_a):
        res = _pool_layer(l, x, vec(mod[l], 0), vec(mod[l], 1), vec(mod[l], 2), norm_g[l].reshape(1, d),
                          w_v, w_z, a_w_out[l].astype(BF16),
                          kv=kv_args if l == n_a - 1 else None)
        x = res if l < n_a - 1 else res[0]
    k, v_t = res[1], res[2]
    bias = _position_bias_t(rel_bias)
    codes = jnp.asarray(_band_codes_t())
    w_q_t, w_qz, w_o = _attn_weights(b_w_in, b_w_out)
    for l in range(n_a, depth):
        j = l - n_a
        sinks = jnp.repeat(b_sinks[j] * LOG2E, BLOCK).reshape(N_KV_HEADS, 1, GQA_GROUPS * BLOCK)
        x = _attn_layer(j, x, vec(mod[l], 0), vec(mod[l], 1), vec(mod[l], 2), norm_g[l].reshape(1, d),
                        w_q_t, w_qz, k, v_t, bias, codes, sinks, w_o,
                        final_g.reshape(1, d), final_norm=(l == depth - 1))
    return x
```

```python
import functools
import math

import numpy as np
import jax
import jax.numpy as jnp
from jax import lax
from jax.experimental import pallas as pl
from jax.experimental.pallas import tpu as pltpu

F32 = jnp.float32
BF16 = jnp.bfloat16

D_MODEL = 1024
A_WIDTH = 2048
POOL_WINDOWS = (2, 4, 8, 16)
N_GROUPS = len(POOL_WINDOWS)
GROUP_WIDTH = A_WIDTH // N_GROUPS
HEAD_DIM = 64
N_HEADS = 16
N_KV_HEADS = 2
GQA_GROUPS = N_HEADS // N_KV_HEADS
KV_WIDTH = N_KV_HEADS * HEAD_DIM
B_WIDTH = N_HEADS * HEAD_DIM
BLOCK = 128
N_BUCKETS = 32
MAX_DISTANCE = 128
EPS = 1e-6
NEG_INF = -1e30
LOG2E = math.log2(math.e)

LANES = 128
SUBLANES = 8
VMEM_LIMIT_BYTES = 56 * 1024 * 1024

MAX_HALO = 16
SEQ_TILE = 1024
ATTN_TILE = 1024
ATTN_CHUNK = 256
PROJ_PIECE = 256
STEP_HEADS = 4
SUM_ROWS = 16
KV_CHUNK = 256
NORM_ROWS = 32
POOL_ROWS = 64

_NT_DIMS = (((1,), (1,)), ((), ()))


def _sigmoid(v):
    return 1.0 / (1.0 + jnp.exp(-v))


def _const_spec(shape):
    zeros = (0,) * len(shape)
    return pl.BlockSpec(shape, lambda *_: zeros, pipeline_mode=pl.Buffered(1))


def _mod_kernel(c_ref, w_ref, b_ref, o_ref):
    c = c_ref[...]
    c_act = c * _sigmoid(c)
    o_ref[0] = jnp.dot(c_act, w_ref[0], preferred_element_type=F32) + b_ref[0]


def _modulation(c_pad, w, b):
    n_layers, d, n = w.shape
    rows = c_pad.shape[0]
    nt = D_MODEL
    return pl.pallas_call(
        _mod_kernel,
        grid=(n_layers, n // nt),
        in_specs=[
            pl.BlockSpec((rows, d), lambda l, j: (0, 0)),
            pl.BlockSpec((1, d, nt), lambda l, j: (l, 0, j)),
            pl.BlockSpec((1, 1, nt), lambda l, j: (l, 0, j)),
        ],
        out_specs=pl.BlockSpec((1, rows, nt), lambda l, j: (l, 0, j)),
        out_shape=jax.ShapeDtypeStruct((n_layers, rows, n), F32),
        compiler_params=pltpu.CompilerParams(
            dimension_semantics=("arbitrary", "arbitrary"), vmem_limit_bytes=VMEM_LIMIT_BYTES),
        name="adaln_modulation",
    )(c_pad, w, b)


def _norm_modulate(x_ref, g_ref, shift_ref, scale_ref, h_ref, rows, start=0):
    gain = g_ref[...] * (1.0 + scale_ref[0])
    shift = shift_ref[0]
    for r in range(start, start + rows, NORM_ROWS):
        xv = x_ref[0, r:r + NORM_ROWS, :]
        ms = jnp.mean(xv * xv, axis=-1, keepdims=True)
        xn = xv * lax.rsqrt(ms + EPS)
        h_ref[r:r + NORM_ROWS, :] = (xn * gain + shift).astype(BF16)


def _split_bf16(a):
    hi = a.astype(BF16)
    lo = (a - hi.astype(F32)).astype(BF16)
    return hi, lo


def _fuse_kernel(wu_ref, wzin_ref, wg_ref, asc_ref, wv_ref, wz_ref):
    a_hi, a_lo = _split_bf16(wu_ref[0])
    b_hi, b_lo = _split_bf16(wg_ref[0, 0])
    acc = jnp.dot(a_hi, b_hi, preferred_element_type=F32)
    acc = acc + jnp.dot(a_hi, b_lo, preferred_element_type=F32)
    acc = acc + jnp.dot(a_lo, b_hi, preferred_element_type=F32)
    wv_ref[0] = (acc * asc_ref[0]).astype(BF16)
    wz_ref[0] = wzin_ref[0].astype(BF16)


def _fuse_group_weights(a_w_in, a_w_group, a_scale):
    n_a, d, _ = a_w_in.shape
    col_block = pl.BlockSpec((1, d, GROUP_WIDTH), lambda l, g: (l, 0, g))
    return pl.pallas_call(
        _fuse_kernel,
        grid=(n_a, N_GROUPS),
        in_specs=[
            col_block,
            pl.BlockSpec((1, d, GROUP_WIDTH), lambda l, g: (l, 0, N_GROUPS + g)),
            pl.BlockSpec((1, 1, GROUP_WIDTH, GROUP_WIDTH), lambda l, g: (l, g, 0, 0)),
            pl.BlockSpec((1, 1, GROUP_WIDTH), lambda l, g: (l, 0, g)),
        ],
        out_specs=[col_block, col_block],
        out_shape=[jax.ShapeDtypeStruct((n_a, d, A_WIDTH), BF16)] * 2,
        compiler_params=pltpu.CompilerParams(
            dimension_semantics=("arbitrary", "arbitrary"), vmem_limit_bytes=VMEM_LIMIT_BYTES),
        name="fuse_group_weights",
    )(a_w_in, a_w_in, a_w_group, a_scale.reshape(n_a, 1, A_WIDTH))


def _pool_kernel(x_ref, shift_ref, scale_ref, gate_ref, g_ref, wv_ref, wz_ref, wout_ref, *rest, emit_kv):
    if emit_kv:
        kshift_ref, kscale_ref, kg_ref, wk_ref, wvt_ref, o_ref, k_ref, vt_ref = rest[:8]
        hk_buf = rest[-1]
        rest = rest[:-1]
    else:
        o_ref = rest[0]
    h_buf, v_ext, z_buf, gated_buf, carry = rest[-5:]
    t = pl.program_id(1)
    ts = SEQ_TILE

    @pl.when(t == 0)
    def _():
        carry[...] = jnp.zeros_like(carry)

    _norm_modulate(x_ref, g_ref, shift_ref, scale_ref, h_buf, ts)
    h = h_buf[...]

    for g, w in enumerate(POOL_WINDOWS):
        cols = slice(g * GROUP_WIDTH, (g + 1) * GROUP_WIDTH)
        z_buf[g] = jnp.dot(h, wz_ref[0, :, cols], preferred_element_type=F32)
        v_ext[g, 0:MAX_HALO, :] = carry[g]
        v_ext[g, MAX_HALO:, :] = jnp.dot(h, wv_ref[0, :, cols], preferred_element_type=F32)
        carry[g] = v_ext[g, ts:ts + MAX_HALO, :]

        halo = MAX_HALO if w > SUBLANES else SUBLANES
        for r in range(0, ts, POOL_ROWS):
            ev = v_ext[g, MAX_HALO + r - halo:MAX_HALO + r + POOL_ROWS, :]
            s = ev
            k = 1
            while k < w:
                s = s + pltpu.roll(s, k, 0)
                k *= 2
            s = s[halo:]
            if r == 0:
                pos = t * ts + lax.broadcasted_iota(jnp.int32, (POOL_ROWS, GROUP_WIDTH), 0)
                mean = s * (1.0 / jnp.minimum(pos + 1, w).astype(F32))
            else:
                mean = s * (1.0 / w)
            z = z_buf[g, r:r + POOL_ROWS, :]
            gated_buf[r:r + POOL_ROWS, cols] = ((mean - ev[halo:]) * (z * _sigmoid(z))).astype(BF16)

    if not emit_kv:
        y = jnp.dot(gated_buf[...], wout_ref[...], preferred_element_type=F32)
        o_ref[0] = x_ref[0] + gate_ref[0] * y
        return

    def out_chunk(r):
        rows = slice(r, r + KV_CHUNK)
        y = jnp.dot(gated_buf[rows, :], wout_ref[...], preferred_element_type=F32)
        o_ref[0, rows, :] = x_ref[0, rows, :] + gate_ref[0] * y

    def kv_chunk(r):
        rows = slice(r, r + KV_CHUNK)
        _norm_modulate(o_ref, kg_ref, kshift_ref, kscale_ref, hk_buf, KV_CHUNK, start=r)
        hk = hk_buf[rows, :]
        k_ref[0, rows, :] = jnp.dot(hk, wk_ref[...], preferred_element_type=F32).astype(BF16)
        vt_ref[0, :, rows] = lax.dot_general(wvt_ref[...], hk, _NT_DIMS,
                                             preferred_element_type=F32).astype(BF16)

    starts = list(range(0, ts, KV_CHUNK))
    out_chunk(starts[0])
    for prev, cur in zip(starts[:-1], starts[1:]):
        out_chunk(cur)
        kv_chunk(prev)
    kv_chunk(starts[-1])


def _pool_layer(layer, x, shift, scale, gate, norm_g, w_v, w_z, w_out, kv=None):
    b, s, d = x.shape
    ts = SEQ_TILE
    vec_spec = pl.BlockSpec((1, 1, d), lambda i, j: (i, 0, 0))
    x_spec = pl.BlockSpec((1, ts, d), lambda i, j: (i, j, 0))
    layer_weights = pl.BlockSpec((1, d, A_WIDTH), lambda i, j: (layer, 0, 0), pipeline_mode=pl.Buffered(1))
    in_specs = [
        x_spec,
        vec_spec, vec_spec, vec_spec,
        _const_spec((1, d)),
        layer_weights,
        layer_weights,
        _const_spec((A_WIDTH, d)),
    ]
    out_specs = x_spec
    out_shape = jax.ShapeDtypeStruct((b, s, d), F32)
    args = (x, shift, scale, gate, norm_g, w_v, w_z, w_out)
    if kv is not None:
        in_specs += [vec_spec, vec_spec, _const_spec((1, d)), _const_spec((d, KV_WIDTH)),
                     _const_spec((KV_WIDTH, d))]
        out_specs = [x_spec,
                     pl.BlockSpec((1, ts, KV_WIDTH), lambda i, j: (i, j, 0)),
                     pl.BlockSpec((1, KV_WIDTH, ts), lambda i, j: (i, 0, j))]
        out_shape = [out_shape,
                     jax.ShapeDtypeStruct((b, s, KV_WIDTH), BF16),
                     jax.ShapeDtypeStruct((b, KV_WIDTH, s), BF16)]
        args += tuple(kv)
    scratch_shapes = [
        pltpu.VMEM((ts, d), BF16),
        pltpu.VMEM((N_GROUPS, ts + MAX_HALO, GROUP_WIDTH), F32),
        pltpu.VMEM((N_GROUPS, ts, GROUP_WIDTH), F32),
        pltpu.VMEM((ts, A_WIDTH), BF16),
        pltpu.VMEM((N_GROUPS, MAX_HALO, GROUP_WIDTH), F32),
    ]
    if kv is not None:
        scratch_shapes.append(pltpu.VMEM((ts, d), BF16))
    return pl.pallas_call(
        functools.partial(_pool_kernel, emit_kv=kv is not None),
        grid=(b, s // ts),
        in_specs=in_specs,
        out_specs=out_specs,
        out_shape=out_shape,
        scratch_shapes=scratch_shapes,
        compiler_params=pltpu.CompilerParams(
            dimension_semantics=("arbitrary", "arbitrary"), vmem_limit_bytes=VMEM_LIMIT_BYTES),
        name="pool_layer",
    )(*args)


def _t5_causal_buckets():
    i = np.arange(BLOCK)[:, None]
    j = np.arange(2 * BLOCK)[None, :]
    n = np.maximum(i + BLOCK - j, 0)
    max_exact = N_BUCKETS // 2
    large = max_exact + (np.log(np.maximum(n, 1) / max_exact) / math.log(MAX_DISTANCE / max_exact)
                         * (N_BUCKETS - max_exact)).astype(np.int32)
    large = np.minimum(large, N_BUCKETS - 1)
    return np.where(n < max_exact, n, large).astype(np.int32)


def _band_codes_t():
    i = np.arange(BLOCK)[None, :]
    j = np.arange(2 * BLOCK)[:, None]
    rel = i + BLOCK - j
    band = (rel >= 0) & (rel < BLOCK)
    return np.where(band, np.where(j < BLOCK, 1.0, 0.0), 2.0).astype(np.float32)


def _bias_kernel(bucket_ref, rel_ref, o_ref):
    bucket = bucket_ref[...]
    for head in range(N_HEADS):
        acc = jnp.zeros((2 * BLOCK, BLOCK), F32)
        for b in range(N_BUCKETS):
            acc = jnp.where(bucket == b, rel_ref[b, head], acc)
        kv_head, g = divmod(head, GQA_GROUPS)
        o_ref[kv_head, g] = acc * LOG2E


def _position_bias_t(rel_bias):
    buckets_t = jnp.asarray(np.ascontiguousarray(_t5_causal_buckets().T))
    return pl.pallas_call(
        _bias_kernel,
        in_specs=[
            pl.BlockSpec(memory_space=pltpu.VMEM),
            pl.BlockSpec(memory_space=pltpu.SMEM),
        ],
        out_specs=pl.BlockSpec(memory_space=pltpu.VMEM),
        out_shape=jax.ShapeDtypeStruct((N_KV_HEADS, GQA_GROUPS, 2 * BLOCK, BLOCK), F32),
        name="position_bias",
    )(buckets_t, rel_bias)


def _attn_kernel(x_ref, shift_ref, scale_ref, gate_ref, g_ref, wqt_ref, wz_ref, kp_ref, kc_ref,
                 vtp_ref, vtc_ref, bias_ref, code_ref, sink_ref, wout_ref, fg_ref, o_ref,
                 h_buf, qt_buf, z_buf, og_buf, s_buf, p_buf, r_buf, y_buf, *, final_norm):
    t = pl.program_id(1)
    n_chunks = ATTN_TILE // ATTN_CHUNK
    pieces = B_WIDTH // PROJ_PIECE
    head_groups = GQA_GROUPS // STEP_HEADS
    steps_per_chunk = (ATTN_CHUNK // BLOCK) * N_KV_HEADS * head_groups
    n_steps = n_chunks * steps_per_chunk
    norm_rows = ATTN_CHUNK // steps_per_chunk

    def norm_slice(c, j):
        _norm_modulate(x_ref, g_ref, shift_ref, scale_ref, h_buf, norm_rows,
                       start=c * ATTN_CHUNK + j * norm_rows)

    h_chunks = {}

    def project_piece(c, n):
        chunk = slice(c * ATTN_CHUNK, (c + 1) * ATTN_CHUNK)
        if c not in h_chunks:
            h_chunks[c] = h_buf[chunk, :]
        h = h_chunks[c]
        if n < pieces:
            sl = slice(n * PROJ_PIECE, (n + 1) * PROJ_PIECE)
            qt = lax.dot_general(wqt_ref[0, sl, :], h, _NT_DIMS, preferred_element_type=F32)
            qt = qt.astype(BF16)
            for blk in range(ATTN_CHUNK // BLOCK):
                qt_buf[c * (ATTN_CHUNK // BLOCK) + blk, sl, :] = qt[:, blk * BLOCK:(blk + 1) * BLOCK]
        else:
            z = jnp.dot(h, wz_ref[0, n - pieces], preferred_element_type=F32)
            for sub in range(PROJ_PIECE // LANES):
                z_buf[(n - pieces) * (PROJ_PIECE // LANES) + sub, chunk, :] = z[:, sub * LANES:(sub + 1) * LANES]

    def out_piece(c, n):
        chunk = slice(c * ATTN_CHUNK, (c + 1) * ATTN_CHUNK)
        og = jnp.concatenate([og_buf[sb, chunk, :] for sb in range(B_WIDTH // LANES)], axis=1)
        y_buf[c % 2, n] = jnp.dot(og, wout_ref[0, n], preferred_element_type=F32)
        if n == pieces - 1:
            for r in range(0, ATTN_CHUNK, NORM_ROWS):
                rows = slice(c * ATTN_CHUNK + r, c * ATTN_CHUNK + r + NORM_ROWS)
                y = jnp.concatenate([y_buf[c % 2, m, r:r + NORM_ROWS, :] for m in range(pieces)], axis=1)
                out = x_ref[0, rows, :] + gate_ref[0] * y
                if final_norm:
                    ms = jnp.mean(out * out, axis=-1, keepdims=True)
                    out = (out * lax.rsqrt(ms + EPS)) * fg_ref[...]
                o_ref[0, rows, :] = out

    def split(step):
        i, rem = divmod(step, N_KV_HEADS * head_groups)
        return (i, *divmod(rem, head_groups))

    def stage_scores(step):
        _scores(*split(step), qt_buf, kp_ref, kc_ref, s_buf.at[step % 2])

    def stage_softmax(step):
        _softmax(*split(step), t, s_buf.at[step % 2], code_ref, bias_ref, sink_ref,
                 p_buf.at[step % 2], r_buf.at[step % 2])

    def stage_pv(step):
        _pv_gate(*split(step), p_buf.at[step % 2], r_buf.at[step % 2], vtp_ref, vtc_ref, z_buf, og_buf)

    norm_slices = [(c, j) for c in range(n_chunks) for j in range(steps_per_chunk)]
    for c, j in norm_slices[:2 * steps_per_chunk]:
        norm_slice(c, j)
    for n in range(2 * pieces):
        project_piece(0, n)

    def out_slices(ko):
        co, no = divmod(ko, steps_per_chunk)
        for n in range(no * pieces // steps_per_chunk, (no + 1) * pieces // steps_per_chunk):
            out_piece(co, n)

    out_lag = steps_per_chunk + 2
    for k in range(n_steps + 2):
        c, j = divmod(k, steps_per_chunk)
        if k < n_steps:
            stage_scores(k)
        if 1 <= k <= n_steps:
            stage_softmax(k - 1)
        if 2 <= k <= n_steps + 1:
            stage_pv(k - 2)
        if c + 2 < n_chunks:
            norm_slice(c + 2, j)
        if c + 1 < n_chunks:
            for n in range(j * 2 * pieces // steps_per_chunk, (j + 1) * 2 * pieces // steps_per_chunk):
                project_piece(c + 1, n)
        if k >= out_lag:
            out_slices(k - out_lag)
    for ko in range(n_steps + 2 - out_lag, n_steps):
        out_slices(ko)


def _scores(i, kvh, sub, qt_buf, kp_ref, kc_ref, s_ref):
    rows = slice(i * BLOCK, (i + 1) * BLOCK)
    k_prev = kp_ref[0] if i == 0 else kc_ref[0, (i - 1) * BLOCK:i * BLOCK, :]
    k2 = jnp.concatenate([k_prev, kc_ref[0, rows, :]], axis=0)
    first = kvh * GQA_GROUPS + sub * STEP_HEADS
    q_cat = jnp.concatenate(
        [qt_buf[i, (first + g) * HEAD_DIM:(first + g + 1) * HEAD_DIM, :] for g in range(STEP_HEADS)], axis=1)
    q_zeros = jnp.zeros((HEAD_DIM, STEP_HEADS * BLOCK), BF16)
    q_sel = jnp.concatenate([q_cat, q_zeros] if kvh == 0 else [q_zeros, q_cat], axis=0)
    s = jnp.dot(k2, q_sel, preferred_element_type=F32)
    for g in range(STEP_HEADS):
        s_ref[g] = s[:, g * BLOCK:(g + 1) * BLOCK]


def _softmax(i, kvh, sub, t, s_ref, code_ref, bias_ref, sink_ref, p_ref, r_ref):
    limit = jnp.where(t == 0, 1.0, 2.0) if i == 0 else 2.0
    for gi in range(STEP_HEADS):
        g = sub * STEP_HEADS + gi
        sink = sink_ref[kvh, :, g * BLOCK:(g + 1) * BLOCK]
        sg = jnp.where(code_ref[...] < limit, s_ref[gi] + bias_ref[kvh, g], NEG_INF)
        m = jnp.maximum(jnp.max(sg, axis=0, keepdims=True), sink)
        p_ref[gi] = jnp.exp2(sg - m).astype(BF16)
        r_ref[gi] = jnp.exp2(sink - m)


def _pv_gate(i, kvh, sub, p_ref, r_ref, vtp_ref, vtc_ref, z_buf, og_buf):
    rows = slice(i * BLOCK, (i + 1) * BLOCK)
    vt_prev = vtp_ref[0] if i == 0 else vtc_ref[0, :, (i - 1) * BLOCK:i * BLOCK]
    vt2 = jnp.concatenate([vt_prev, vtc_ref[0, :, rows]], axis=1)
    ones_rows = (lax.broadcasted_iota(jnp.int32, (SUM_ROWS, 2 * BLOCK), 0) == 0).astype(BF16)
    vt = jnp.concatenate([vt2[kvh * HEAD_DIM:(kvh + 1) * HEAD_DIM, :], ones_rows], axis=0)
    pt = jnp.concatenate([p_ref[g] for g in range(STEP_HEADS)], axis=1)
    ot = jnp.dot(vt, pt, preferred_element_type=F32)
    inv_den = [1.0 / (ot[HEAD_DIM:HEAD_DIM + 1, g * BLOCK:(g + 1) * BLOCK] + r_ref[g])
               for g in range(STEP_HEADS)]
    for p in range(STEP_HEADS // 2):
        pair_t = jnp.concatenate(
            [ot[:HEAD_DIM, g * BLOCK:(g + 1) * BLOCK] * inv_den[g] for g in (2 * p, 2 * p + 1)], axis=0)
        slab = kvh * (GQA_GROUPS // 2) + sub * (STEP_HEADS // 2) + p
        z = z_buf[slab, rows, :]
        og_buf[slab, rows, :] = (pair_t.T * (z * _sigmoid(z))).astype(BF16)


def _attn_weights_kernel(wq_ref, wzin_ref, woutin_ref, wqt_ref, wz_ref, wout_ref):
    wqt_ref[0] = (wq_ref[0].T * (HEAD_DIM ** -0.5 * LOG2E)).astype(BF16)
    wz_ref[0, 0] = wzin_ref[0].astype(BF16)
    wout_ref[0, 0] = woutin_ref[0].astype(BF16)


def _attn_weights(b_w_in, b_w_out):
    n_b, d, _ = b_w_in.shape
    blocks = B_WIDTH // PROJ_PIECE
    slab_spec = pl.BlockSpec((1, 1, d, PROJ_PIECE), lambda l, j: (l, j, 0, 0))
    slab_shape = jax.ShapeDtypeStruct((n_b, blocks, d, PROJ_PIECE), BF16)
    return pl.pallas_call(
        _attn_weights_kernel,
        grid=(n_b, blocks),
        in_specs=[
            pl.BlockSpec((1, d, PROJ_PIECE), lambda l, j: (l, 0, j)),
            pl.BlockSpec((1, d, PROJ_PIECE), lambda l, j: (l, 0, blocks + j)),
            pl.BlockSpec((1, B_WIDTH, PROJ_PIECE), lambda l, j: (l, 0, j)),
        ],
        out_specs=[pl.BlockSpec((1, PROJ_PIECE, d), lambda l, j: (l, j, 0)), slab_spec, slab_spec],
        out_shape=[jax.ShapeDtypeStruct((n_b, B_WIDTH, d), BF16), slab_shape, slab_shape],
        compiler_params=pltpu.CompilerParams(
            dimension_semantics=("arbitrary", "arbitrary"), vmem_limit_bytes=VMEM_LIMIT_BYTES),
        name="attn_weights",
    )(b_w_in, b_w_in, b_w_out)


def _attn_layer(layer, x, shift, scale, gate, norm_g, w_q_t, w_z, k, v_t, bias, codes, sinks, w_out,
                final_g, final_norm):
    b, s, d = x.shape
    ts = ATTN_TILE
    blocks_per_tile = ts // BLOCK
    vec_spec = pl.BlockSpec((1, 1, d), lambda i, j: (i, 0, 0))
    score_shape = (2, STEP_HEADS, 2 * BLOCK, BLOCK)
    weight_slabs = pl.BlockSpec((1, B_WIDTH // PROJ_PIECE, d, PROJ_PIECE), lambda i, j: (layer, 0, 0, 0),
                                pipeline_mode=pl.Buffered(1))

    def prev_block(j):
        return jnp.maximum(j * blocks_per_tile - 1, 0)

    return pl.pallas_call(
        functools.partial(_attn_kernel, final_norm=final_norm),
        grid=(b, s // ts),
        in_specs=[
            pl.BlockSpec((1, ts, d), lambda i, j: (i, j, 0)),
            vec_spec, vec_spec, vec_spec,
            _const_spec((1, d)),
            pl.BlockSpec((1, B_WIDTH, d), lambda i, j: (layer, 0, 0), pipeline_mode=pl.Buffered(1)),
            weight_slabs,
            pl.BlockSpec((1, BLOCK, KV_WIDTH), lambda i, j: (i, prev_block(j), 0)),
            pl.BlockSpec((1, ts, KV_WIDTH), lambda i, j: (i, j, 0)),
            pl.BlockSpec((1, KV_WIDTH, BLOCK), lambda i, j: (i, 0, prev_block(j))),
            pl.BlockSpec((1, KV_WIDTH, ts), lambda i, j: (i, 0, j)),
            _const_spec((N_KV_HEADS, GQA_GROUPS, 2 * BLOCK, BLOCK)),
            _const_spec((2 * BLOCK, BLOCK)),
            _const_spec((N_KV_HEADS, 1, GQA_GROUPS * BLOCK)),
            weight_slabs,
            _const_spec((1, d)),
        ],
        out_specs=pl.BlockSpec((1, ts, d), lambda i, j: (i, j, 0)),
        out_shape=jax.ShapeDtypeStruct((b, s, d), F32),
        scratch_shapes=[
            pltpu.VMEM((ts, d), BF16),
            pltpu.VMEM((blocks_per_tile, B_WIDTH, BLOCK), BF16),
            pltpu.VMEM((B_WIDTH // LANES, ts, LANES), F32),
            pltpu.VMEM((B_WIDTH // LANES, ts, LANES), BF16),
            pltpu.VMEM(score_shape, F32),
            pltpu.VMEM(score_shape, BF16),
            pltpu.VMEM((2, STEP_HEADS, 1, BLOCK), F32),
            pltpu.VMEM((2, d // PROJ_PIECE, ATTN_CHUNK, PROJ_PIECE), F32),
        ],
        compiler_params=pltpu.CompilerParams(
            dimension_semantics=("arbitrary", "arbitrary"), vmem_limit_bytes=VMEM_LIMIT_BYTES),
        name="attn_layer",
    )(x, shift, scale, gate, norm_g, w_q_t, w_z, k, k, v_t, v_t, bias, codes, sinks, w_out, final_g)


def kernel(x, c, norm_g, ada_w, ada_b, a_w_in, a_w_group, a_scale, a_w_out, kv_norm_g, kv_ada_w,
           kv_ada_b, w_kv, b_w_in, b_sinks, b_w_out, rel_bias, final_g):
    b, s, d = x.shape
    depth = norm_g.shape[0]
    n_a = a_w_in.shape[0]
    assert (d, s % SEQ_TILE, s % ATTN_TILE) == (D_MODEL, 0, 0) and b <= SUBLANES

    c_pad = jnp.pad(c, ((0, SUBLANES - b), (0, 0)))
    mod = _modulation(c_pad, ada_w, ada_b.reshape(depth, 1, 3 * d))[:, :b]
    mod_kv = _modulation(c_pad, kv_ada_w[None], kv_ada_b.reshape(1, 1, 2 * d))[0, :b]

    def vec(m, k):
        return m[:, k * d:(k + 1) * d].reshape(b, 1, d)

    w_v, w_z = _fuse_group_weights(a_w_in, a_w_group, a_scale)
    kv_args = (vec(mod_kv, 0), vec(mod_kv, 1), kv_norm_g.reshape(1, d),
               w_kv[:, :KV_WIDTH].astype(BF16), w_kv[:, KV_WIDTH:].T.astype(BF16))
    for l in range(n_a):
        res = _pool_layer(l, x, vec(mod[l], 0), vec(mod[l], 1), vec(mod[l], 2), norm_g[l].reshape(1, d),
                          w_v, w_z, a_w_out[l].astype(BF16),
                          kv=kv_args if l == n_a - 1 else None)
        x = res if l < n_a - 1 else res[0]
    k, v_t = res[1], res[2]
    bias = _position_bias_t(rel_bias)
    codes = jnp.asarray(_band_codes_t())
    w_q_t, w_qz, w_o = _attn_weights(b_w_in, b_w_out)
    for l in range(n_a, depth):
        j = l - n_a
        sinks = jnp.repeat(b_sinks[j] * LOG2E, BLOCK).reshape(N_KV_HEADS, 1, GQA_GROUPS * BLOCK)
        x = _attn_layer(j, x, vec(mod[l], 0), vec(mod[l], 1), vec(mod[l], 2), norm_g[l].reshape(1, d),
                        w_q_t, w_qz, k, v_t, bias, codes, sinks, w_o,
                        final_g.reshape(1, d), final_norm=(l == depth - 1))
    return x
```
